```python
import math, functools
import jax, jax.numpy as jnp
from jax import lax
import numpy as np

D_MODEL = 2048
BATCH = 4
SEQ = 2048
DEPTH = 2
DEC_BATCH = 32
DEC_SEQ = 4
PAST_LEN = 8192
PAGE_SIZE = 128

MIX_WIDTH = D_MODEL
ATTN_WIDTH = MIX_WIDTH // 2
CONV_WIDTH = MIX_WIDTH - ATTN_WIDTH
N_HEADS = 8
HEAD_DIM = ATTN_WIDTH // (2 * N_HEADS)
V_DIM = 2 * HEAD_DIM
ROT_DIM = HEAD_DIM // 4
ROPE_THETA = 500000.0
CONV_K = 3
PLE_DIM = 256
PEER_HEADS = 8
PEER_NKEYS = 128
PEER_N = PEER_NKEYS * PEER_NKEYS
PEER_QDIM = 256
PEER_HALF = PEER_QDIM // 2
PEER_TOPK = 16
Q_BLOCK = 128
TOKEN_BLOCK = 128
EPS = 1e-6
IN_COLS = 3 * ATTN_WIDTH + 3 * CONV_WIDTH
SPLITS = (ATTN_WIDTH, 2 * ATTN_WIDTH, 3 * ATTN_WIDTH, 3 * ATTN_WIDTH + CONV_WIDTH, 3 * ATTN_WIDTH + 2 * CONV_WIDTH)

kernel_name = 'hybrid_diffattn_shortconv_peer_decode_step'


def rmsnorm(x, g):
    xf = x.astype(jnp.float32)
    y = xf * lax.rsqrt(jnp.mean(xf * xf, axis=-1, keepdims=True) + EPS)
    return (y * g.astype(jnp.float32)).astype(x.dtype)


def lambda_init(layer_idx):
    return 0.8 - 0.6 * math.exp(-0.3 * layer_idx)


def rope(x, pos):
    inv_freq = ROPE_THETA ** (-jnp.arange(0, ROT_DIM, 2, dtype=jnp.float32) / ROT_DIM)
    ang = pos.astype(jnp.float32)[:, None] * inv_freq[None, :]
    cos = jnp.cos(ang)[:, None, None, :].astype(x.dtype)
    sin = jnp.sin(ang)[:, None, None, :].astype(x.dtype)
    half = ROT_DIM // 2
    x1 = x[..., :half]
    x2 = x[..., half:ROT_DIM]
    return jnp.concatenate([x1 * cos - x2 * sin, x2 * cos + x1 * sin, x[..., ROT_DIM:]], axis=-1)


def diff_logits(q, k, q_pos, k_pos):
    s = jnp.einsum('bqhcd,bkhcd->bhcqk', q, k).astype(jnp.float32) * (HEAD_DIM ** -0.5)
    mask = k_pos[None, :] <= q_pos[:, None]
    return jnp.where(mask, s, jnp.finfo(jnp.float32).min)


def diff_weights(logits, lam):
    p = jax.nn.softmax(logits, axis=-1)
    return p[:, :, 0] - lam * p[:, :, 1]


def prompt_attention(q, k, v, pos, lam):
    B, S = q.shape[:2]
    nb = S // Q_BLOCK
    qb = q.reshape(B, nb, Q_BLOCK, N_HEADS, 2, HEAD_DIM).swapaxes(0, 1)
    pb = pos.reshape(nb, Q_BLOCK)

    def block(args):
        qi, pi = args
        w = diff_weights(diff_logits(qi, k, pi, pos), lam)
        return jnp.einsum('bhqk,bkhe->bqhe', w.astype(v.dtype), v)

    out = lax.map(block, (qb, pb))
    return out.swapaxes(0, 1).reshape(B, S, N_HEADS, V_DIM)


def sample_attention(q, k, v, pos, lam, k_past, v_past):
    P = k_past.shape[1]
    past_pos = jnp.arange(P)
    logits = jnp.concatenate([diff_logits(q, k_past, pos, past_pos), diff_logits(q, k, pos, pos)], axis=-1)
    w = diff_weights(logits, lam).astype(v.dtype)
    return (jnp.einsum('bhqk,bkhe->bqhe', w[..., :P], v_past)
            + jnp.einsum('bhqk,bkhe->bqhe', w[..., P:], v))


def short_conv(u, prev, w):
    S = u.shape[1]
    ext = jnp.concatenate([prev.astype(u.dtype), u], axis=1)
    y = w[0] * ext[:, 0:S]
    for j in range(1, CONV_K):
        y = y + w[j] * ext[:, j:j + S]
    return y, ext[:, ext.shape[1] - (CONV_K - 1):]


def peer(h, wq, subkeys, u_tab, v_tab):
    lead = h.shape[:-1]
    t = h.reshape(-1, D_MODEL)
    T = t.shape[0]
    nb = -(-T // TOKEN_BLOCK)
    t = jnp.pad(t, ((0, nb * TOKEN_BLOCK - T), (0, 0))).reshape(nb, TOKEN_BLOCK, D_MODEL)

    def block(xb):
        q = (xb @ wq).reshape(TOKEN_BLOCK, PEER_HEADS, 2, PEER_HALF)
        s = jnp.einsum('thcd,hcnd->thcn', q, subkeys).astype(jnp.float32)
        s1, i1 = lax.top_k(s[:, :, 0], PEER_TOPK)
        s2, i2 = lax.top_k(s[:, :, 1], PEER_TOPK)
        cand = (s1[..., :, None] + s2[..., None, :]).reshape(TOKEN_BLOCK, PEER_HEADS, PEER_TOPK * PEER_TOPK)
        sc, flat = lax.top_k(cand, PEER_TOPK)
        idx = (jnp.take_along_axis(i1, flat // PEER_TOPK, axis=-1) * PEER_NKEYS
               + jnp.take_along_axis(i2, flat % PEER_TOPK, axis=-1))
        g = jax.nn.softmax(sc, axis=-1)
        u = jnp.take(u_tab, idx, axis=0)
        vv = jnp.take(v_tab, idx, axis=0)
        a = jax.nn.gelu(jnp.einsum('td,thkd->thk', xb, u), approximate=False)
        return jnp.einsum('thk,thkd->td', (g.astype(a.dtype) * a), vv)

    out = lax.map(block, t).reshape(nb * TOKEN_BLOCK, D_MODEL)[:T]
    return out.reshape(*lead, D_MODEL)


def layer(x, p_l, pos, conv_prev, attend, lam0, attn_norm_g, w_in, q_norm_g, k_norm_g,
          lam_q1, lam_k1, lam_q2, lam_k2, subln_g, conv_w, w_out, ffn_norm_g,
          peer_wq, peer_subkeys, peer_u, peer_v, ple_norm_g, ple_gate_w, ple_proj_w):
    B, S, _ = x.shape
    h = rmsnorm(x, attn_norm_g)
    q, k, v, gb, gc, xc = jnp.split(h @ w_in, SPLITS, axis=-1)
    q = rope(rmsnorm(q.reshape(B, S, N_HEADS, 2, HEAD_DIM), q_norm_g), pos)
    k = rope(rmsnorm(k.reshape(B, S, N_HEADS, 2, HEAD_DIM), k_norm_g), pos)
    v = v.reshape(B, S, N_HEADS, V_DIM)
    f32 = jnp.float32
    lam = (jnp.exp(jnp.sum(lam_q1.astype(f32) * lam_k1.astype(f32)))
           - jnp.exp(jnp.sum(lam_q2.astype(f32) * lam_k2.astype(f32))) + lam0)
    a = attend(q, k, v, pos, lam)
    a = (rmsnorm(a, subln_g) * (1.0 - lam0)).reshape(B, S, ATTN_WIDTH)
    conv_out, conv_state = short_conv(gc * xc, conv_prev, conv_w)
    c = gb * conv_out
    x = x + jnp.concatenate([a, c], axis=-1) @ w_out
    x = x + peer(rmsnorm(x, ffn_norm_g), peer_wq, peer_subkeys, peer_u, peer_v)
    gate = jax.nn.sigmoid(rmsnorm(x, ple_norm_g) @ ple_gate_w)
    x = x + gate * (p_l @ ple_proj_w)
    return x, k, v, conv_state


def setup_inputs(seed: int = 0) -> dict:
    key = jax.random.key(seed)
    ks = iter(jax.random.split(key, 40))
    nrm = lambda shape: jax.random.normal(next(ks), shape, jnp.float32)
    n_pages = PAST_LEN // PAGE_SIZE
    n_used = DEC_BATCH * n_pages
    n_pool = n_used + max(1, n_used // 4)
    page_table = jax.random.permutation(next(ks), n_pool)[:n_used].reshape(DEC_BATCH, n_pages).astype(jnp.int32)
    gain = lambda n: 1.0 + 0.1 * nrm((DEPTH, n))
    return {
        'x_prompt': nrm((BATCH, SEQ, D_MODEL)),
        'x_sample': nrm((DEC_BATCH, DEC_SEQ, D_MODEL)),
        'cache_k': nrm((DEPTH, n_pool, PAGE_SIZE, N_HEADS, 2, HEAD_DIM)),
        'cache_v': nrm((DEPTH, n_pool, PAGE_SIZE, N_HEADS, V_DIM)),
        'state_conv': nrm((DEPTH, DEC_BATCH, CONV_K - 1, CONV_WIDTH)),
        'page_table': page_table,
        'p_prompt': nrm((DEPTH, BATCH, SEQ, PLE_DIM)),
        'p_sample': nrm((DEPTH, DEC_BATCH, DEC_SEQ, PLE_DIM)),
        'attn_norm_g': gain(D_MODEL),
        'w_in': nrm((DEPTH, D_MODEL, IN_COLS)) * D_MODEL ** -0.5,
        'q_norm_g': gain(HEAD_DIM),
        'k_norm_g': gain(HEAD_DIM),
        'lam_q1': 0.1 * nrm((DEPTH, HEAD_DIM)),
        'lam_k1': 0.1 * nrm((DEPTH, HEAD_DIM)),
        'lam_q2': 0.1 * nrm((DEPTH, HEAD_DIM)),
        'lam_k2': 0.1 * nrm((DEPTH, HEAD_DIM)),
        'subln_g': gain(V_DIM),
        'conv_w': nrm((DEPTH, CONV_K, CONV_WIDTH)) * CONV_K ** -0.5,
        'w_out': nrm((DEPTH, MIX_WIDTH, D_MODEL)) * MIX_WIDTH ** -0.5,
        'ffn_norm_g': gain(D_MODEL),
        'peer_wq': nrm((DEPTH, D_MODEL, PEER_HEADS * PEER_QDIM)) * D_MODEL ** -0.5,
        'peer_subkeys': nrm((DEPTH, PEER_HEADS, 2, PEER_NKEYS, PEER_HALF)) * PEER_HALF ** -0.5,
        'peer_u': nrm((DEPTH, PEER_N, D_MODEL)) * D_MODEL ** -0.5,
        'peer_v': nrm((DEPTH, PEER_N, D_MODEL)) * PEER_HEADS ** -0.5,
        'ple_norm_g': gain(D_MODEL),
        'ple_gate_w': nrm((DEPTH, D_MODEL, D_MODEL)) * D_MODEL ** -0.5,
        'ple_proj_w': nrm((DEPTH, PLE_DIM, D_MODEL)) * PLE_DIM ** -0.5,
    }


def reference(x_prompt, x_sample, cache_k, cache_v, state_conv, page_table, p_prompt, p_sample,
              attn_norm_g, w_in, q_norm_g, k_norm_g, lam_q1, lam_k1, lam_q2, lam_k2, subln_g,
              conv_w, w_out, ffn_norm_g, peer_wq, peer_subkeys, peer_u, peer_v,
              ple_norm_g, ple_gate_w, ple_proj_w):
    dec_b, n_pages = page_table.shape
    past_len = n_pages * cache_k.shape[2]
    pos_p = jnp.arange(x_prompt.shape[1])
    pos_s = past_len + jnp.arange(x_sample.shape[1])
    conv0 = jnp.zeros((x_prompt.shape[0], CONV_K - 1, CONV_WIDTH), x_prompt.dtype)
    yp, ys = x_prompt, x_sample
    kps, vps, cps, kss, vss, css = [], [], [], [], [], []
    for l in range(DEPTH):
        lam0 = lambda_init(l)
        lw = (attn_norm_g[l], w_in[l], q_norm_g[l], k_norm_g[l], lam_q1[l], lam_k1[l], lam_q2[l],
              lam_k2[l], subln_g[l], conv_w[l], w_out[l], ffn_norm_g[l], peer_wq[l], peer_subkeys[l],
              peer_u[l], peer_v[l], ple_norm_g[l], ple_gate_w[l], ple_proj_w[l])
        yp, kp, vp, cp = layer(yp, p_prompt[l], pos_p, conv0, prompt_attention, lam0, *lw)
        k_past = cache_k[l, page_table].reshape(dec_b, past_len, N_HEADS, 2, HEAD_DIM)
        v_past = cache_v[l, page_table].reshape(dec_b, past_len, N_HEADS, V_DIM)
        attend_s = functools.partial(sample_attention, k_past=k_past, v_past=v_past)
        ys, k_s, v_s, c_s = layer(ys, p_sample[l], pos_s, state_conv[l], attend_s, lam0, *lw)
        kps.append(kp); vps.append(vp); cps.append(cp)
        kss.append(k_s); vss.append(v_s); css.append(c_s)
    return (yp, ys, jnp.stack(kps), jnp.stack(vps), jnp.stack(cps), jnp.stack(kss), jnp.stack(vss), jnp.stack(css))
```

```python
import functools
import math

import jax
import jax.numpy as jnp
from jax import lax
from jax.experimental import pallas as pl
from jax.experimental.pallas import tpu as pltpu

F32 = jnp.float32
BF16 = jnp.bfloat16

D_MODEL = 2048
N_HEADS = 8
HEAD_DIM = 64
V_DIM = 128
ATTN_WIDTH = 1024
CONV_WIDTH = 1024
IN_COLS = 3 * ATTN_WIDTH + 3 * CONV_WIDTH
ROT_DIM = 16
ROPE_THETA = 500000.0
CONV_K = 3
PLE_DIM = 256
PEER_HEADS = 8
PEER_NKEYS = 128
PEER_N = PEER_NKEYS * PEER_NKEYS
PEER_TOPK = 16
EPS = 1e-6
PAGE_SIZE = 128

LANES = 128
VMEM_LIMIT_BYTES = 56 * 1024 * 1024
NEG_INF = float("-inf")
MASK_VALUE = float(jnp.finfo(jnp.float32).min)
SQRT_HALF = 0.7071067811865476

PROMPT_TM = 512
ATTN_TQ = 512
PAGES_PER_STEP = 8
PEER_EC = 512
PLE_TM = 256


def _params(*sem):
    return pltpu.CompilerParams(dimension_semantics=sem, vmem_limit_bytes=VMEM_LIMIT_BYTES)


def _rms_rows(x, g):
    ms = jnp.mean(x * x, axis=-1, keepdims=True)
    return x * lax.rsqrt(ms + EPS) * g


def _dot(a, b):
    return jnp.dot(a, b, preferred_element_type=F32)


def _dot_nt(a, b):
    return lax.dot_general(a, b, (((1,), (1,)), ((), ())), preferred_element_type=F32)


def _norm_mm_kernel(x_ref, g_ref, w_ref, o_ref, hn_ref):
    @pl.when(pl.program_id(1) == 0)
    def _():
        hn_ref[...] = _rms_rows(x_ref[...], g_ref[...]).astype(BF16)

    o_ref[...] = _dot(hn_ref[...], w_ref[...])


def _inproj_kernel(x_ref, g_ref, w_ref, qkg_ref, ra_ref, rp_ref, rm_ref, o_ref, hn_ref):
    j = pl.program_id(1)

    @pl.when(j == 0)
    def _():
        hn_ref[...] = _rms_rows(x_ref[...], g_ref[...]).astype(BF16)

    y = _dot(hn_ref[...], w_ref[...])

    @pl.when(j < 2)
    def _():
        r = lax.broadcasted_iota(jnp.int32, (LANES, LANES), 0) // HEAD_DIM
        c = lax.broadcasted_iota(jnp.int32, (LANES, LANES), 1) // HEAD_DIM
        group_sum = jnp.where(r == c, 1.0, 0.0).astype(BF16)
        for h in range(N_HEADS):
            yh = y[:, h * LANES:(h + 1) * LANES]
            sq = yh * yh
            hi = sq.astype(BF16)
            lo = (sq - hi.astype(F32)).astype(BF16)
            ss = _dot(hi, group_sum) + _dot(lo, group_sum)
            yn = yh * lax.rsqrt(ss * (1.0 / HEAD_DIM) + EPS) * qkg_ref[...]
            o_ref[:, h * LANES:(h + 1) * LANES] = (
                yn * ra_ref[...]
                + pltpu.roll(yn, ROT_DIM // 2, 1) * rp_ref[...]
                + pltpu.roll(yn, LANES - ROT_DIM // 2, 1) * rm_ref[...])

    @pl.when(j >= 2)
    def _():
        o_ref[...] = y


def _norm_mm(x, g, w, tm, tn, name):
    t, d = x.shape
    n = w.shape[1]
    return pl.pallas_call(
        _norm_mm_kernel,
        grid=(t // tm, n // tn),
        in_specs=[pl.BlockSpec((tm, d), lambda i, j: (i, 0)),
                  pl.BlockSpec((1, d), lambda i, j: (0, 0)),
                  pl.BlockSpec((d, tn), lambda i, j: (0, j))],
        out_specs=pl.BlockSpec((tm, tn), lambda i, j: (i, j)),
        out_shape=jax.ShapeDtypeStruct((t, n), F32),
        scratch_shapes=[pltpu.VMEM((tm, d), BF16)],
        compiler_params=_params("parallel", "arbitrary"),
        name=name,
    )(x, g, w)


def _inproj(x, g, w, qkg, rope, tm, name):
    t, d = x.shape
    tn = ATTN_WIDTH
    ra, rp, rm = rope
    nrep = ra.shape[0] // tm
    rope_spec = pl.BlockSpec((tm, LANES), lambda i, j: (i % nrep, 0))
    return pl.pallas_call(
        _inproj_kernel,
        grid=(t // tm, IN_COLS // tn),
        in_specs=[pl.BlockSpec((tm, d), lambda i, j: (i, 0)),
                  pl.BlockSpec((1, d), lambda i, j: (0, 0)),
                  pl.BlockSpec((d, tn), lambda i, j: (0, j)),
                  pl.BlockSpec((None, 1, LANES), lambda i, j: (jnp.minimum(j, 1), 0, 0)),
                  rope_spec, rope_spec, rope_spec],
        out_specs=pl.BlockSpec((tm, tn), lambda i, j: (i, j)),
        out_shape=jax.ShapeDtypeStruct((t, IN_COLS), F32),
        scratch_shapes=[pltpu.VMEM((tm, d), BF16)],
        compiler_params=_params("parallel", "arbitrary"),
        name=name,
    )(x, g, w, qkg, ra, rp, rm)


def _rope_tables(pos):
    inv_freq = ROPE_THETA ** (-jnp.arange(0, ROT_DIM, 2, dtype=F32) / ROT_DIM)
    ang = pos.astype(F32)[:, None] * inv_freq[None, :]
    cos, sin = jnp.cos(ang), jnp.sin(ang)
    p = pos.shape[0]
    half = ROT_DIM // 2
    one = jnp.ones((p, HEAD_DIM - ROT_DIM), F32)
    zero = jnp.zeros((p, HEAD_DIM - ROT_DIM), F32)
    zh = jnp.zeros((p, half), F32)
    a = jnp.concatenate([cos, cos, one], axis=-1)
    bp = jnp.concatenate([zh, sin, zero], axis=-1)
    bm = jnp.concatenate([-sin, zh, zero], axis=-1)
    return tuple(jnp.concatenate([t, t], axis=-1) for t in (a, bp, bm))


def _lambda_full(lamv_ref, lam0):
    lv = lamv_ref[...]
    s1 = jnp.sum(lv[0:1] * lv[1:2], axis=-1, keepdims=True)
    s2 = jnp.sum(lv[2:3] * lv[3:4], axis=-1, keepdims=True)
    return jnp.exp(s1) - jnp.exp(s2) + lam0


def _pattn_kernel(q_ref, k_ref, v_ref, lamv_ref, sg_ref, o_ref, kb_ref, vb_ref, m_ref, acc_ref,
                  *, lam0, tq):
    qi = pl.program_id(2)
    seq = k_ref.shape[0]

    @pl.when(qi == 0)
    def _():
        kb_ref[...] = k_ref[...].astype(BF16)
        vb_ref[:, :V_DIM] = v_ref[...].astype(BF16)
        lane = lax.broadcasted_iota(jnp.int32, (seq, LANES), 1)
        vb_ref[:, V_DIM:] = jnp.where(lane == 0, 1.0, 0.0).astype(BF16)

    q = q_ref[...] * (HEAD_DIM ** -0.5)
    lane = lax.broadcasted_iota(jnp.int32, (tq, LANES), 1)
    q0 = jnp.where(lane < HEAD_DIM, q, 0.0).astype(BF16)
    q1 = jnp.where(lane >= HEAD_DIM, q, 0.0).astype(BF16)
    qq = jnp.concatenate([q0, q1], axis=0)

    m_ref[...] = jnp.full(m_ref.shape, NEG_INF, F32)
    acc_ref[...] = jnp.zeros(acc_ref.shape, F32)

    def chunk(kj, masked):
        off = pl.multiple_of(kj * tq, tq)
        s = _dot_nt(qq, kb_ref[pl.ds(off, tq), :])
        if masked:
            row = lax.broadcasted_iota(jnp.int32, s.shape, 0)
            col = lax.broadcasted_iota(jnp.int32, s.shape, 1)
            s = jnp.where(col <= (row & (tq - 1)), s, MASK_VALUE)
        m_old = m_ref[...]
        m_new = jnp.maximum(m_old, jnp.max(s, axis=-1, keepdims=True))
        alpha = jnp.exp(m_old - m_new)
        p = jnp.exp(s - m_new).astype(BF16)
        acc_ref[...] = alpha * acc_ref[...] + _dot(p, vb_ref[pl.ds(off, tq), :])
        m_ref[...] = m_new

    def body(kj, carry):
        chunk(kj, False)
        return carry

    lax.fori_loop(0, qi, body, 0)
    chunk(qi, True)

    acc = acc_ref[...]
    o = acc[:, :V_DIM] / acc[:, V_DIM:V_DIM + 1]
    lam = _lambda_full(lamv_ref, lam0)
    a = o[:tq] - lam * o[tq:]
    o_ref[...] = (_rms_rows(a, sg_ref[...]) * (1.0 - lam0)).astype(o_ref.dtype)


def _prompt_attention(y, lamv, sg, lam0, batch, seq):
    tq = ATTN_TQ
    nq = seq // tq
    kern = functools.partial(_pattn_kernel, lam0=lam0, tq=tq)
    return pl.pallas_call(
        kern,
        grid=(batch, N_HEADS, nq),
        in_specs=[pl.BlockSpec((tq, LANES), lambda b, h, i: (b * nq + i, h)),
                  pl.BlockSpec((seq, LANES), lambda b, h, i: (b, N_HEADS + h)),
                  pl.BlockSpec((seq, LANES), lambda b, h, i: (b, 2 * N_HEADS + h)),
                  pl.BlockSpec((4, LANES), lambda b, h, i: (0, 0)),
                  pl.BlockSpec((1, LANES), lambda b, h, i: (0, 0))],
        out_specs=pl.BlockSpec((tq, LANES), lambda b, h, i: (b * nq + i, h)),
        out_shape=jax.ShapeDtypeStruct((batch * seq, ATTN_WIDTH), BF16),
        scratch_shapes=[pltpu.VMEM((seq, LANES), BF16),
                        pltpu.VMEM((seq, 2 * LANES), BF16),
                        pltpu.VMEM((2 * tq, 1), F32),
                        pltpu.VMEM((2 * tq, 2 * LANES), F32)],
        compiler_params=_params("parallel", "parallel", "arbitrary"),
        name="prompt_attention",
    )(y, y, y, lamv, sg)


def _sattn_kernel(pt_ref, q_ref, kn_ref, vn_ref, lamv_ref, sg_ref, *rest, lam0, npg):
    k_refs = rest[:npg]
    v_refs = rest[npg:2 * npg]
    o_ref, qblk_ref, knp_ref, vnp_ref, m_ref, l_ref, acc_ref = rest[2 * npg:]
    g = pl.program_id(1)
    dec_seq = q_ref.shape[0]
    nrow = N_HEADS * 2 * dec_seq

    @pl.when(g == 0)
    def _():
        q = q_ref[...] * (HEAD_DIM ** -0.5)
        row = lax.broadcasted_iota(jnp.int32, (nrow, ATTN_WIDTH), 0)
        lane = lax.broadcasted_iota(jnp.int32, (nrow, ATTN_WIDTH), 1)
        qq = jnp.zeros((nrow, ATTN_WIDTH), F32)
        for i in range(dec_seq):
            qq = jnp.where((row & (dec_seq - 1)) == i, q[i:i + 1, :], qq)
        qt = jnp.where(lane // HEAD_DIM == row // dec_seq, qq, 0.0)
        qt = jnp.concatenate([qt, jnp.zeros((LANES - nrow, ATTN_WIDTH), F32)], axis=0)
        qblk_ref[...] = qt.T.astype(BF16)
        knp_ref[...] = jnp.zeros(knp_ref.shape, F32)
        vnp_ref[...] = jnp.zeros(vnp_ref.shape, F32)
        m_ref[...] = jnp.full(m_ref.shape, NEG_INF, F32)
        l_ref[...] = jnp.zeros(l_ref.shape, F32)
        acc_ref[...] = jnp.zeros(acc_ref.shape, F32)

    def attend(st, vb):
        m_old = m_ref[...]
        m_new = jnp.maximum(m_old, jnp.max(st, axis=-1, keepdims=True))
        alpha = jnp.exp(m_old - m_new)
        p = jnp.exp(st - m_new)
        l_ref[...] = alpha * l_ref[...] + jnp.sum(p, axis=-1, keepdims=True)
        o = _dot(p.astype(BF16), vb)
        rows = 2 * dec_seq
        od = jnp.concatenate(
            [o[h * rows:(h + 1) * rows, h * V_DIM:(h + 1) * V_DIM] for h in range(N_HEADS)], axis=0)
        acc_ref[...] = alpha * acc_ref[...] + od
        m_ref[...] = m_new

    kcat = jnp.concatenate([r[...].astype(BF16) for r in k_refs], axis=0)
    vcat = jnp.concatenate([r[...].astype(BF16) for r in v_refs], axis=0)
    s = _dot(kcat, qblk_ref[...])
    st = jnp.concatenate(
        [s[p * PAGE_SIZE:(p + 1) * PAGE_SIZE].T[:nrow] for p in range(npg)], axis=1)
    attend(st, vcat)

    @pl.when(g == pl.num_programs(1) - 1)
    def _():
        knp_ref[0:dec_seq, :] = kn_ref[...]
        vnp_ref[0:dec_seq, :] = vn_ref[...]
        sn = _dot(knp_ref[...].astype(BF16), qblk_ref[...]).T[:nrow]
        row = lax.broadcasted_iota(jnp.int32, sn.shape, 0)
        col = lax.broadcasted_iota(jnp.int32, sn.shape, 1)
        sn = jnp.where(col <= (row & (dec_seq - 1)), sn, MASK_VALUE)
        attend(sn, vnp_ref[...].astype(BF16))

        o = acc_ref[...] / l_ref[...]
        lam = _lambda_full(lamv_ref, lam0)
        a = o - lam * pltpu.roll(o, nrow - dec_seq, 0)
        a = _rms_rows(a, sg_ref[...]) * (1.0 - lam0)
        rows = 2 * dec_seq
        for h in range(N_HEADS):
            o_ref[:, h * V_DIM:(h + 1) * V_DIM] = a[h * rows:h * rows + dec_seq, :]


def _sample_attention(y3, cache_k, cache_v, page_table, layer, lamv, sg, lam0):
    dec_b, dec_seq, _ = y3.shape
    n_pages = page_table.shape[1]
    npg = PAGES_PER_STEP
    assert n_pages % npg == 0 and dec_seq & (dec_seq - 1) == 0
    nrow = N_HEADS * 2 * dec_seq
    assert nrow <= LANES
    pt = page_table.reshape(-1)

    def page_spec(p):
        return pl.BlockSpec((None, None, PAGE_SIZE, ATTN_WIDTH),
                            lambda b, g, pt_ref: (layer, pt_ref[b * n_pages + g * npg + p], 0, 0))

    def col_spec(c):
        return pl.BlockSpec((None, dec_seq, ATTN_WIDTH), lambda b, g, pt_ref: (b, 0, c))

    grid_spec = pltpu.PrefetchScalarGridSpec(
        num_scalar_prefetch=1,
        grid=(dec_b, n_pages // npg),
        in_specs=[col_spec(0), col_spec(1), col_spec(2),
                  pl.BlockSpec((4, LANES), lambda b, g, pt_ref: (0, 0)),
                  pl.BlockSpec((1, LANES), lambda b, g, pt_ref: (0, 0))]
                 + [page_spec(p) for p in range(npg)] + [page_spec(p) for p in range(npg)],
        out_specs=pl.BlockSpec((None, dec_seq, ATTN_WIDTH), lambda b, g, pt_ref: (b, 0, 0)),
        scratch_shapes=[pltpu.VMEM((ATTN_WIDTH, LANES), BF16),
                        pltpu.VMEM((PAGE_SIZE, ATTN_WIDTH), F32),
                        pltpu.VMEM((PAGE_SIZE, ATTN_WIDTH), F32),
                        pltpu.VMEM((nrow, 1), F32),
                        pltpu.VMEM((nrow, 1), F32),
                        pltpu.VMEM((nrow, V_DIM), F32)],
    )
    kern = functools.partial(_sattn_kernel, lam0=lam0, npg=npg)
    return pl.pallas_call(
        kern,
        grid_spec=grid_spec,
        out_shape=jax.ShapeDtypeStruct((dec_b, dec_seq, ATTN_WIDTH), F32),
        compiler_params=_params("parallel", "arbitrary"),
        name="sample_attention",
    )(pt, y3, y3, y3, lamv, sg, *([cache_k] * npg), *([cache_v] * npg))


def _conv_prompt_kernel(gb_ref, gc_ref, xc_ref, hgc_ref, hxc_ref, w_ref, c_ref, st_ref, *, tiles_per_seq):
    i = pl.program_id(0)
    u = gc_ref[...] * xc_ref[...]
    tm = u.shape[0]
    uh = hgc_ref[...] * hxc_ref[...]
    uh = jnp.where(i % tiles_per_seq == 0, 0.0, uh)
    row = lax.broadcasted_iota(jnp.int32, u.shape, 0)
    u1 = jnp.where(row == 0, uh[7:8], pltpu.roll(u, 1, 0))
    u2 = jnp.where(row == 0, uh[6:7], jnp.where(row == 1, uh[7:8], pltpu.roll(u, 2, 0)))
    w = w_ref[...]
    conv = w[0:1] * u2 + w[1:2] * u1 + w[2:3] * u
    c_ref[...] = (gb_ref[...] * conv).astype(c_ref.dtype)
    st_ref[...] = u[tm - (CONV_K - 1):tm]


def _conv_prompt(y, conv_w, batch, seq, tm):
    t = y.shape[0]
    tps = seq // tm
    hb = tm // 8
    kern = functools.partial(_conv_prompt_kernel, tiles_per_seq=tps)
    wide = lambda c: pl.BlockSpec((tm, CONV_WIDTH), lambda i: (i, c))
    halo = lambda c: pl.BlockSpec((8, CONV_WIDTH), lambda i: (jnp.maximum(i * hb - 1, 0), c))
    return pl.pallas_call(
        kern,
        grid=(t // tm,),
        in_specs=[wide(3), wide(4), wide(5), halo(4), halo(5),
                  pl.BlockSpec((CONV_K, CONV_WIDTH), lambda i: (0, 0))],
        out_specs=[pl.BlockSpec((tm, CONV_WIDTH), lambda i: (i, 0)),
                   pl.BlockSpec((None, CONV_K - 1, CONV_WIDTH), lambda i: (i // tps, 0, 0))],
        out_shape=[jax.ShapeDtypeStruct((t, CONV_WIDTH), BF16),
                   jax.ShapeDtypeStruct((batch, CONV_K - 1, CONV_WIDTH), F32)],
        compiler_params=_params("arbitrary"),
        name="conv_prompt",
    )(y, y, y, y, y, conv_w)


def _conv_sample_kernel(gb_ref, gc_ref, xc_ref, e1_ref, e2_ref, w_ref, c_ref, u_ref, *, dec_seq):
    u = gc_ref[...] * xc_ref[...]
    pos = lax.broadcasted_iota(jnp.int32, u.shape, 0) & (dec_seq - 1)
    u1 = jnp.where(pos == 0, e1_ref[...], pltpu.roll(u, 1, 0))
    u2 = jnp.where(pos <= 1, e2_ref[...], pltpu.roll(u, 2, 0))
    w = w_ref[...]
    conv = w[0:1] * u2 + w[1:2] * u1 + w[2:3] * u
    c_ref[...] = (gb_ref[...] * conv).astype(c_ref.dtype)
    u_ref[...] = u


def _conv_sample(y, conv_w, prev, dec_seq):
    t = y.shape[0]
    dec_b = t // dec_seq
    zeros = jnp.zeros((dec_b, dec_seq - 1, CONV_WIDTH), F32)
    e1 = jnp.concatenate([prev[:, 1:2], zeros], axis=1).reshape(t, CONV_WIDTH)
    e2 = jnp.concatenate([prev, zeros[:, 1:]], axis=1).reshape(t, CONV_WIDTH)
    kern = functools.partial(_conv_sample_kernel, dec_seq=dec_seq)
    wide = lambda c: pl.BlockSpec((t, CONV_WIDTH), lambda i: (0, c))
    full = pl.BlockSpec((t, CONV_WIDTH), lambda i: (0, 0))
    return pl.pallas_call(
        kern,
        grid=(1,),
        in_specs=[wide(3), wide(4), wide(5), full, full,
                  pl.BlockSpec((CONV_K, CONV_WIDTH), lambda i: (0, 0))],
        out_specs=[full, full],
        out_shape=[jax.ShapeDtypeStruct((t, CONV_WIDTH), BF16),
                   jax.ShapeDtypeStruct((t, CONV_WIDTH), F32)],
        compiler_params=_params("arbitrary"),
        name="conv_sample",
    )(y, y, y, e1, e2, conv_w)


def _outproj_kernel(x_ref, a_ref, c_ref, wa_ref, wc_ref, o_ref):
    o_ref[...] = (x_ref[...] + _dot(a_ref[...].astype(BF16), wa_ref[...])
                  + _dot(c_ref[...].astype(BF16), wc_ref[...]))


def _outproj(x, a, c, w, tm, tn, name):
    t, d = x.shape
    return pl.pallas_call(
        _outproj_kernel,
        grid=(t // tm, d // tn),
        in_specs=[pl.BlockSpec((tm, tn), lambda i, j: (i, j)),
                  pl.BlockSpec((tm, ATTN_WIDTH), lambda i, j: (i, 0)),
                  pl.BlockSpec((tm, CONV_WIDTH), lambda i, j: (i, 0)),
                  pl.BlockSpec((ATTN_WIDTH, tn), lambda i, j: (0, j)),
                  pl.BlockSpec((CONV_WIDTH, tn), lambda i, j: (1, j))],
        out_specs=pl.BlockSpec((tm, tn), lambda i, j: (i, j)),
        out_shape=jax.ShapeDtypeStruct((t, d), F32),
        compiler_params=_params("parallel", "arbitrary"),
        name=name,
    )(x, a, c, w, w)


def _top_values(s, k):
    out = []
    for _ in range(k):
        m = jnp.max(s, axis=0, keepdims=True)
        out.append(m)
        s = jnp.where(s == m, NEG_INF, s)
    return out


def _stack_rows(rows):
    n = len(rows)
    r = lax.broadcasted_iota(jnp.int32, (n, rows[0].shape[1]), 0)
    out = jnp.broadcast_to(rows[0], r.shape)
    for j in range(1, n):
        out = jnp.where(r == j, rows[j], out)
    return out


def _peer_select_kernel(x_ref, g_ref, hq_ref, sk_ref, xnt_ref, s1_ref, s2_ref, e1_ref, e2_ref, thr_ref):
    xn = _rms_rows(x_ref[...], g_ref[...])
    xnt_ref[...] = xn.T.astype(BF16)
    for h in range(PEER_HEADS):
        c0 = h * 2 * PEER_NKEYS
        q1 = hq_ref[:, c0:c0 + PEER_NKEYS].astype(BF16)
        q2 = hq_ref[:, c0 + PEER_NKEYS:c0 + 2 * PEER_NKEYS].astype(BF16)
        s1 = _dot_nt(sk_ref[h, 0], q1)
        s2 = _dot_nt(sk_ref[h, 1], q2)
        t1 = _top_values(s1, PEER_TOPK)
        t2 = _top_values(s2, PEER_TOPK)
        t2_all = _stack_rows(t2)
        cand = jnp.concatenate([t + t2_all for t in t1], axis=0)
        best = _top_values(cand, PEER_TOPK)
        z = jnp.ones_like(best[0])
        for c in best[1:]:
            z = z + jnp.exp(c - best[0])
        s1_ref[h] = s1
        s2_ref[h] = s2
        e1_ref[h] = jnp.exp(s1 - t1[0]) / z
        e2_ref[h] = jnp.exp(s2 - t2[0])
        thr_ref[h] = best[PEER_TOPK - 1]


def _peer_select(x, g, hq, subkeys, tm, name):
    t, d = x.shape
    nk = PEER_NKEYS
    head_arr = jax.ShapeDtypeStruct((PEER_HEADS, nk, t), F32)
    head_spec = pl.BlockSpec((PEER_HEADS, nk, tm), lambda i: (0, 0, i))
    return pl.pallas_call(
        _peer_select_kernel,
        grid=(t // tm,),
        in_specs=[pl.BlockSpec((tm, d), lambda i: (i, 0)),
                  pl.BlockSpec((1, d), lambda i: (0, 0)),
                  pl.BlockSpec((tm, PEER_HEADS * 2 * nk), lambda i: (i, 0)),
                  pl.BlockSpec((PEER_HEADS, 2, nk, nk), lambda i: (0, 0, 0, 0))],
        out_specs=[pl.BlockSpec((d, tm), lambda i: (0, i)),
                   head_spec, head_spec, head_spec, head_spec,
                   pl.BlockSpec((PEER_HEADS, 1, tm), lambda i: (0, 0, i))],
        out_shape=[jax.ShapeDtypeStruct((d, t), BF16),
                   head_arr, head_arr, head_arr, head_arr,
                   jax.ShapeDtypeStruct((PEER_HEADS, 1, t), F32)],
        compiler_params=_params("parallel"),
        name=name,
    )(x, g, hq, subkeys)


def _peer_dense_kernel(xnt_ref, s1_ref, s2_ref, e1_ref, e2_ref, thr_ref, u_ref, vt_ref, o_ref, acc_ref):
    e = pl.program_id(1)
    ec = u_ref.shape[0]
    n1_per_step = ec // PEER_NKEYS

    @pl.when(e == 0)
    def _():
        acc_ref[...] = jnp.zeros(acc_ref.shape, F32)

    a = _dot(u_ref[...], xnt_ref[...])
    act = 0.5 * a * (1.0 + lax.erf(a * SQRT_HALF))
    ws = []
    for r in range(n1_per_step):
        n1 = e * n1_per_step + r
        w = None
        for h in range(PEER_HEADS):
            s1_row = s1_ref[h, pl.ds(n1, 1), :]
            e1_row = e1_ref[h, pl.ds(n1, 1), :]
            keep = (s1_row + s2_ref[h]) >= thr_ref[h]
            wh = jnp.where(keep, e1_row * e2_ref[h], 0.0)
            w = wh if w is None else w + wh
        ws.append(w)
    gate = jnp.concatenate(ws, axis=0)
    acc_ref[...] += _dot(vt_ref[...], (gate * act).astype(BF16))

    @pl.when(e == pl.num_programs(1) - 1)
    def _():
        o_ref[...] = acc_ref[...]


def _peer_dense(xnt, s1, s2, e1, e2, thr, u, vt, tm, name):
    d, t = xnt.shape
    ec = PEER_EC
    head_spec = pl.BlockSpec((PEER_HEADS, PEER_NKEYS, tm), lambda i, e: (0, 0, i))
    return pl.pallas_call(
        _peer_dense_kernel,
        grid=(t // tm, PEER_N // ec),
        in_specs=[pl.BlockSpec((d, tm), lambda i, e: (0, i)),
                  head_spec, head_spec, head_spec, head_spec,
                  pl.BlockSpec((PEER_HEADS, 1, tm), lambda i, e: (0, 0, i)),
                  pl.BlockSpec((ec, d), lambda i, e: (e, 0)),
                  pl.BlockSpec((d, ec), lambda i, e: (0, e))],
        out_specs=pl.BlockSpec((d, tm), lambda i, e: (0, i)),
        out_shape=jax.ShapeDtypeStruct((d, t), F32),
        scratch_shapes=[pltpu.VMEM((d, tm), F32)],
        compiler_params=_params("parallel", "arbitrary"),
        name=name,
    )(xnt, s1, s2, e1, e2, thr, u, vt)


def _ple_kernel(x_ref, pt_ref, p_ref, g_ref, wg_ref, wp_ref, o_ref):
    x = x_ref[...] + pt_ref[...].T
    xn = _rms_rows(x, g_ref[...]).astype(BF16)
    z = _dot(xn, wg_ref[...])
    gate = 1.0 / (1.0 + jnp.exp(-z))
    o_ref[...] = x + gate * _dot(p_ref[...].astype(BF16), wp_ref[...])


def _ple(x, peer_t, p, g, wg, wp, tm, name):
    t, d = x.shape
    return pl.pallas_call(
        _ple_kernel,
        grid=(t // tm,),
        in_specs=[pl.BlockSpec((tm, d), lambda i: (i, 0)),
                  pl.BlockSpec((d, tm), lambda i: (0, i)),
                  pl.BlockSpec((tm, PLE_DIM), lambda i: (i, 0)),
                  pl.BlockSpec((1, d), lambda i: (0, 0)),
                  pl.BlockSpec((d, d), lambda i: (0, 0)),
                  pl.BlockSpec((PLE_DIM, d), lambda i: (0, 0))],
        out_specs=pl.BlockSpec((tm, d), lambda i: (i, 0)),
        out_shape=jax.ShapeDtypeStruct((t, d), F32),
        compiler_params=_params("parallel"),
        name=name,
    )(x, peer_t, p, g, wg, wp)


def _lambda_init(layer_idx):
    return 0.8 - 0.6 * math.exp(-0.3 * layer_idx)


def _token_tail(x, a, c, lw, p_l, tm, tag):
    x1 = _outproj(x, a, c, lw["w_out"], tm, D_MODEL // 2, "outproj_" + tag)
    hq = _norm_mm(x1, lw["ffn_norm_g"], lw["peer_wq"], tm, D_MODEL // 2, "peer_query_" + tag)
    xnt, s1, s2, e1, e2, thr = _peer_select(x1, lw["ffn_norm_g"], hq, lw["peer_subkeys"], tm,
                                            "peer_select_" + tag)
    peer_t = _peer_dense(xnt, s1, s2, e1, e2, thr, lw["peer_u"], lw["peer_vt"], tm, "peer_dense_" + tag)
    return _ple(x1, peer_t, p_l, lw["ple_norm_g"], lw["ple_gate_w"], lw["ple_proj_w"],
                min(tm, PLE_TM), "ple_" + tag)


def kernel(x_prompt, x_sample, cache_k, cache_v, state_conv, page_table, p_prompt, p_sample,
           attn_norm_g, w_in, q_norm_g, k_norm_g, lam_q1, lam_k1, lam_q2, lam_k2, subln_g,
           conv_w, w_out, ffn_norm_g, peer_wq, peer_subkeys, peer_u, peer_v,
           ple_norm_g, ple_gate_w, ple_proj_w):
    batch, seq, d = x_prompt.shape
    dec_b, dec_seq, _ = x_sample.shape
    depth = w_in.shape[0]
    n_pool = cache_k.shape[1]
    past_len = page_table.shape[1] * cache_k.shape[2]
    tp, ts = batch * seq, dec_b * dec_seq

    rope_p = _rope_tables(jnp.arange(seq))
    rope_s = _rope_tables(past_len + (jnp.arange(ts) % dec_seq))
    ck = cache_k.reshape(depth, n_pool, PAGE_SIZE, ATTN_WIDTH)
    cv = cache_v.reshape(depth, n_pool, PAGE_SIZE, ATTN_WIDTH)

    yp = x_prompt.reshape(tp, d)
    ys = x_sample.reshape(ts, d)
    outs = [[] for _ in range(6)]
    row = lambda v: v.reshape(1, -1)
    for l in range(depth):
        lam0 = _lambda_init(l)
        lw = {
            "w_out": w_out[l].astype(BF16),
            "ffn_norm_g": row(ffn_norm_g[l]),
            "peer_wq": peer_wq[l].astype(BF16),
            "peer_subkeys": peer_subkeys[l].astype(BF16),
            "peer_u": peer_u[l].astype(BF16),
            "peer_vt": peer_v[l].T.astype(BF16),
            "ple_norm_g": row(ple_norm_g[l]),
            "ple_gate_w": ple_gate_w[l].astype(BF16),
            "ple_proj_w": ple_proj_w[l].astype(BF16),
        }
        w_in_l = w_in[l].astype(BF16)
        g_in = row(attn_norm_g[l])
        qkg = jnp.stack([jnp.tile(q_norm_g[l], 2), jnp.tile(k_norm_g[l], 2)]).reshape(2, 1, LANES)
        lamv = jnp.pad(jnp.stack([lam_q1[l], lam_k1[l], lam_q2[l], lam_k2[l]]),
                       ((0, 0), (0, LANES - HEAD_DIM)))
        sg = row(subln_g[l])

        y = _inproj(yp, g_in, w_in_l, qkg, rope_p, PROMPT_TM, "inproj_prompt")
        a = _prompt_attention(y, lamv, sg, lam0, batch, seq)
        c, cstate = _conv_prompt(y, conv_w[l], batch, seq, PROMPT_TM)
        outs[0].append(y[:, ATTN_WIDTH:2 * ATTN_WIDTH].reshape(batch, seq, N_HEADS, 2, HEAD_DIM))
        outs[1].append(y[:, 2 * ATTN_WIDTH:3 * ATTN_WIDTH].reshape(batch, seq, N_HEADS, V_DIM))
        outs[2].append(cstate)
        yp = _token_tail(yp, a, c, lw, p_prompt[l].reshape(tp, PLE_DIM), PROMPT_TM, "prompt")

        y = _inproj(ys, g_in, w_in_l, qkg, rope_s, ts, "inproj_sample")
        a = _sample_attention(y.reshape(dec_b, dec_seq, IN_COLS), ck, cv, page_table, l, lamv, sg, lam0)
        c, u = _conv_sample(y, conv_w[l], state_conv[l], dec_seq)
        outs[3].append(y[:, ATTN_WIDTH:2 * ATTN_WIDTH].reshape(dec_b, dec_seq, N_HEADS, 2, HEAD_DIM))
        outs[4].append(y[:, 2 * ATTN_WIDTH:3 * ATTN_WIDTH].reshape(dec_b, dec_seq, N_HEADS, V_DIM))
        outs[5].append(u.reshape(dec_b, dec_seq, CONV_WIDTH)[:, dec_seq - (CONV_K - 1):])
        ys = _token_tail(ys, a.reshape(ts, ATTN_WIDTH), c, lw, p_sample[l].reshape(ts, PLE_DIM), ts, "sample")

    return (yp.reshape(batch, seq, d), ys.reshape(dec_b, dec_seq, d),
            jnp.stack(outs[0]), jnp.stack(outs[1]), jnp.stack(outs[2]),
            jnp.stack(outs[3]), jnp.stack(outs[4]), jnp.stack(outs[5]))
```

```python
import functools
import math

import jax
import jax.numpy as jnp
from jax import lax
from jax.experimental import pallas as pl
from jax.experimental.pallas import tpu as pltpu

F32 = jnp.float32
BF16 = jnp.bfloat16

D_MODEL = 2048
N_HEADS = 8
HEAD_DIM = 64
V_DIM = 128
ATTN_WIDTH = 1024
CONV_WIDTH = 1024
IN_COLS = 3 * ATTN_WIDTH + 3 * CONV_WIDTH
ROT_DIM = 16
ROPE_THETA = 500000.0
CONV_K = 3
PLE_DIM = 256
PEER_HEADS = 8
PEER_NKEYS = 128
PEER_N = PEER_NKEYS * PEER_NKEYS
PEER_TOPK = 16
EPS = 1e-6
PAGE_SIZE = 128

LANES = 128
VMEM_LIMIT_BYTES = 56 * 1024 * 1024
NEG_INF = float("-inf")
MASK_VALUE = float(jnp.finfo(jnp.float32).min)
SQRT_HALF = 0.7071067811865476

PROMPT_TM = 512
ATTN_TQ = 512
PAGES_PER_STEP = 8
PEER_EC = 1024
GATE_ROWS = 8
PLE_TM = 256


def _params(*sem):
    return pltpu.CompilerParams(dimension_semantics=sem, vmem_limit_bytes=VMEM_LIMIT_BYTES)


def _rms_rows(x, g):
    ms = jnp.mean(x * x, axis=-1, keepdims=True)
    return x * lax.rsqrt(ms + EPS) * g


def _dot(a, b):
    return jnp.dot(a, b, preferred_element_type=F32)


def _dot_nt(a, b):
    return lax.dot_general(a, b, (((1,), (1,)), ((), ())), preferred_element_type=F32)


def _norm_mm_kernel(x_ref, g_ref, w_ref, o_ref, hn_ref):
    @pl.when(pl.program_id(1) == 0)
    def _():
        hn_ref[...] = _rms_rows(x_ref[...], g_ref[...]).astype(BF16)

    o_ref[...] = _dot(hn_ref[...], w_ref[...])


def _inproj_kernel(x_ref, g_ref, w_ref, qkg_ref, ra_ref, rp_ref, rm_ref, o_ref, hn_ref):
    j = pl.program_id(1)

    @pl.when(j == 0)
    def _():
        hn_ref[...] = _rms_rows(x_ref[...], g_ref[...]).astype(BF16)

    y = _dot(hn_ref[...], w_ref[...])

    @pl.when(j < 2)
    def _():
        r = lax.broadcasted_iota(jnp.int32, (LANES, LANES), 0) // HEAD_DIM
        c = lax.broadcasted_iota(jnp.int32, (LANES, LANES), 1) // HEAD_DIM
        group_sum = jnp.where(r == c, 1.0, 0.0).astype(BF16)
        for h in range(N_HEADS):
            yh = y[:, h * LANES:(h + 1) * LANES]
            sq = yh * yh
            hi = sq.astype(BF16)
            lo = (sq - hi.astype(F32)).astype(BF16)
            ss = _dot(hi, group_sum) + _dot(lo, group_sum)
            yn = yh * lax.rsqrt(ss * (1.0 / HEAD_DIM) + EPS) * qkg_ref[...]
            o_ref[:, h * LANES:(h + 1) * LANES] = (
                yn * ra_ref[...]
                + pltpu.roll(yn, ROT_DIM // 2, 1) * rp_ref[...]
                + pltpu.roll(yn, LANES - ROT_DIM // 2, 1) * rm_ref[...])

    @pl.when(j >= 2)
    def _():
        o_ref[...] = y


def _norm_mm(x, g, w, tm, tn, name):
    t, d = x.shape
    n = w.shape[1]
    return pl.pallas_call(
        _norm_mm_kernel,
        grid=(t // tm, n // tn),
        in_specs=[pl.BlockSpec((tm, d), lambda i, j: (i, 0)),
                  pl.BlockSpec((1, d), lambda i, j: (0, 0)),
                  pl.BlockSpec((d, tn), lambda i, j: (0, j))],
        out_specs=pl.BlockSpec((tm, tn), lambda i, j: (i, j)),
        out_shape=jax.ShapeDtypeStruct((t, n), F32),
        scratch_shapes=[pltpu.VMEM((tm, d), BF16)],
        compiler_params=_params("parallel", "arbitrary"),
        name=name,
    )(x, g, w)


def _inproj(x, g, w, qkg, rope, tm, name):
    t, d = x.shape
    tn = ATTN_WIDTH
    ra, rp, rm = rope
    nrep = ra.shape[0] // tm
    rope_spec = pl.BlockSpec((tm, LANES), lambda i, j: (i % nrep, 0))
    return pl.pallas_call(
        _inproj_kernel,
        grid=(t // tm, IN_COLS // tn),
        in_specs=[pl.BlockSpec((tm, d), lambda i, j: (i, 0)),
                  pl.BlockSpec((1, d), lambda i, j: (0, 0)),
                  pl.BlockSpec((d, tn), lambda i, j: (0, j)),
                  pl.BlockSpec((None, 1, LANES), lambda i, j: (jnp.minimum(j, 1), 0, 0)),
                  rope_spec, rope_spec, rope_spec],
        out_specs=pl.BlockSpec((tm, tn), lambda i, j: (i, j)),
        out_shape=jax.ShapeDtypeStruct((t, IN_COLS), F32),
        scratch_shapes=[pltpu.VMEM((tm, d), BF16)],
        compiler_params=_params("parallel", "arbitrary"),
        name=name,
    )(x, g, w, qkg, ra, rp, rm)


def _rope_tables(pos):
    inv_freq = ROPE_THETA ** (-jnp.arange(0, ROT_DIM, 2, dtype=F32) / ROT_DIM)
    ang = pos.astype(F32)[:, None] * inv_freq[None, :]
    cos, sin = jnp.cos(ang), jnp.sin(ang)
    p = pos.shape[0]
    half = ROT_DIM // 2
    one = jnp.ones((p, HEAD_DIM - ROT_DIM), F32)
    zero = jnp.zeros((p, HEAD_DIM - ROT_DIM), F32)
    zh = jnp.zeros((p, half), F32)
    a = jnp.concatenate([cos, cos, one], axis=-1)
    bp = jnp.concatenate([zh, sin, zero], axis=-1)
    bm = jnp.concatenate([-sin, zh, zero], axis=-1)
    return tuple(jnp.concatenate([t, t], axis=-1) for t in (a, bp, bm))


def _lambda_full(lamv_ref, lam0):
    lv = lamv_ref[...]
    s1 = jnp.sum(lv[0:1] * lv[1:2], axis=-1, keepdims=True)
    s2 = jnp.sum(lv[2:3] * lv[3:4], axis=-1, keepdims=True)
    return jnp.exp(s1) - jnp.exp(s2) + lam0


def _pattn_kernel(q_ref, k_ref, v_ref, lamv_ref, sg_ref, o_ref, kb_ref, vb_ref, m_ref, acc_ref,
                  *, lam0, tq):
    qi = pl.program_id(2)
    seq = k_ref.shape[0]

    @pl.when(qi == 0)
    def _():
        kb_ref[...] = k_ref[...].astype(BF16)
        vb_ref[:, :V_DIM] = v_ref[...].astype(BF16)
        lane = lax.broadcasted_iota(jnp.int32, (seq, LANES), 1)
        vb_ref[:, V_DIM:] = jnp.where(lane == 0, 1.0, 0.0).astype(BF16)

    q = q_ref[...] * (HEAD_DIM ** -0.5)
    lane = lax.broadcasted_iota(jnp.int32, (tq, LANES), 1)
    q0 = jnp.where(lane < HEAD_DIM, q, 0.0).astype(BF16)
    q1 = jnp.where(lane >= HEAD_DIM, q, 0.0).astype(BF16)
    qq = jnp.concatenate([q0, q1], axis=0)

    m_ref[...] = jnp.full(m_ref.shape, NEG_INF, F32)
    acc_ref[...] = jnp.zeros(acc_ref.shape, F32)

    def chunk(kj, masked):
        off = pl.multiple_of(kj * tq, tq)
        s = _dot_nt(qq, kb_ref[pl.ds(off, tq), :])
        if masked:
            row = lax.broadcasted_iota(jnp.int32, s.shape, 0)
            col = lax.broadcasted_iota(jnp.int32, s.shape, 1)
            s = jnp.where(col <= (row & (tq - 1)), s, MASK_VALUE)
        m_old = m_ref[...]
        m_new = jnp.maximum(m_old, jnp.max(s, axis=-1, keepdims=True))
        alpha = jnp.exp(m_old - m_new)
        p = jnp.exp(s - m_new).astype(BF16)
        acc_ref[...] = alpha * acc_ref[...] + _dot(p, vb_ref[pl.ds(off, tq), :])
        m_ref[...] = m_new

    def body(kj, carry):
        chunk(kj, False)
        return carry

    lax.fori_loop(0, qi, body, 0)
    chunk(qi, True)

    acc = acc_ref[...]
    o = acc[:, :V_DIM] / acc[:, V_DIM:V_DIM + 1]
    lam = _lambda_full(lamv_ref, lam0)
    a = o[:tq] - lam * o[tq:]
    o_ref[...] = (_rms_rows(a, sg_ref[...]) * (1.0 - lam0)).astype(o_ref.dtype)


def _prompt_attention(y, lamv, sg, lam0, batch, seq):
    tq = ATTN_TQ
    nq = seq // tq
    kern = functools.partial(_pattn_kernel, lam0=lam0, tq=tq)
    return pl.pallas_call(
        kern,
        grid=(batch, N_HEADS, nq),
        in_specs=[pl.BlockSpec((tq, LANES), lambda b, h, i: (b * nq + i, h)),
                  pl.BlockSpec((seq, LANES), lambda b, h, i: (b, N_HEADS + h)),
                  pl.BlockSpec((seq, LANES), lambda b, h, i: (b, 2 * N_HEADS + h)),
                  pl.BlockSpec((4, LANES), lambda b, h, i: (0, 0)),
                  pl.BlockSpec((1, LANES), lambda b, h, i: (0, 0))],
        out_specs=pl.BlockSpec((tq, LANES), lambda b, h, i: (b * nq + i, h)),
        out_shape=jax.ShapeDtypeStruct((batch * seq, ATTN_WIDTH), BF16),
        scratch_shapes=[pltpu.VMEM((seq, LANES), BF16),
                        pltpu.VMEM((seq, 2 * LANES), BF16),
                        pltpu.VMEM((2 * tq, 1), F32),
                        pltpu.VMEM((2 * tq, 2 * LANES), F32)],
        compiler_params=_params("parallel", "parallel", "arbitrary"),
        name="prompt_attention",
    )(y, y, y, lamv, sg)


def _sattn_kernel(pt_ref, q_ref, kn_ref, vn_ref, lamv_ref, sg_ref, *rest, lam0, npg):
    k_refs = rest[:npg]
    v_refs = rest[npg:2 * npg]
    o_ref, qt_ref, knp_ref, vnp_ref, m_ref, l_ref, acc_ref = rest[2 * npg:]
    g = pl.program_id(1)
    dec_seq = q_ref.shape[0]
    rows = 2 * dec_seq
    nrow = N_HEADS * rows

    @pl.when(g == 0)
    def _():
        q = q_ref[...] * (HEAD_DIM ** -0.5)
        row = lax.broadcasted_iota(jnp.int32, (nrow, ATTN_WIDTH), 0)
        lane = lax.broadcasted_iota(jnp.int32, (nrow, ATTN_WIDTH), 1)
        qq = jnp.zeros((nrow, ATTN_WIDTH), F32)
        for i in range(dec_seq):
            qq = jnp.where((row & (dec_seq - 1)) == i, q[i:i + 1, :], qq)
        qt_ref[...] = jnp.where(lane // HEAD_DIM == row // dec_seq, qq, 0.0).astype(BF16)
        knp_ref[...] = jnp.zeros(knp_ref.shape, F32)
        vnp_ref[...] = jnp.zeros(vnp_ref.shape, F32)
        m_ref[...] = jnp.full(m_ref.shape, NEG_INF, F32)
        l_ref[...] = jnp.zeros(l_ref.shape, F32)
        acc_ref[...] = jnp.zeros(acc_ref.shape, F32)

    def attend(st, weighted_values):
        m_old = m_ref[...]
        m_new = jnp.maximum(m_old, jnp.max(st, axis=-1, keepdims=True))
        alpha = jnp.exp(m_old - m_new)
        p = jnp.exp(st - m_new)
        l_ref[...] = alpha * l_ref[...] + jnp.sum(p, axis=-1, keepdims=True)
        acc_ref[...] = alpha * acc_ref[...] + weighted_values(p)
        m_ref[...] = m_new

    kc = jnp.concatenate([r[...].astype(BF16) for r in k_refs], axis=1)
    st = _dot(qt_ref[...], kc)

    def page_values(p):
        outs = []
        for h in range(N_HEADS):
            vh = jnp.concatenate(
                [r[pl.ds(h, PAGE_SIZE, stride=N_HEADS), :].astype(BF16) for r in v_refs], axis=0)
            outs.append(_dot(p[h * rows:(h + 1) * rows, :].astype(BF16), vh))
        return jnp.concatenate(outs, axis=0)

    attend(st, page_values)

    @pl.when(g == pl.num_programs(1) - 1)
    def _():
        knp_ref[0:dec_seq, :] = kn_ref[...]
        vnp_ref[0:dec_seq, :] = vn_ref[...]
        sn = _dot_nt(qt_ref[...], knp_ref[...].astype(BF16))
        row = lax.broadcasted_iota(jnp.int32, sn.shape, 0)
        col = lax.broadcasted_iota(jnp.int32, sn.shape, 1)
        sn = jnp.where(col <= (row & (dec_seq - 1)), sn, MASK_VALUE)

        def new_values(p):
            o = _dot(p.astype(BF16), vnp_ref[...].astype(BF16))
            return jnp.concatenate(
                [o[h * rows:(h + 1) * rows, h * V_DIM:(h + 1) * V_DIM] for h in range(N_HEADS)], axis=0)

        attend(sn, new_values)

        o = acc_ref[...] / l_ref[...]
        lam = _lambda_full(lamv_ref, lam0)
        a = o - lam * pltpu.roll(o, nrow - dec_seq, 0)
        a = _rms_rows(a, sg_ref[...]) * (1.0 - lam0)
        for h in range(N_HEADS):
            o_ref[:, h * V_DIM:(h + 1) * V_DIM] = a[h * rows:h * rows + dec_seq, :]


def _sample_attention(y3, cache_kt, cache_vr, page_table, layer, lamv, sg, lam0):
    dec_b, dec_seq, _ = y3.shape
    n_pages = page_table.shape[1]
    npg = PAGES_PER_STEP
    assert n_pages % npg == 0 and dec_seq & (dec_seq - 1) == 0
    nrow = N_HEADS * 2 * dec_seq
    pt = page_table.reshape(-1)

    def page_spec(p):
        return pl.BlockSpec((None, None, ATTN_WIDTH, PAGE_SIZE),
                            lambda b, g, pt_ref: (layer, pt_ref[b * n_pages + g * npg + p], 0, 0))

    def col_spec(c):
        return pl.BlockSpec((None, dec_seq, ATTN_WIDTH), lambda b, g, pt_ref: (b, 0, c))

    grid_spec = pltpu.PrefetchScalarGridSpec(
        num_scalar_prefetch=1,
        grid=(dec_b, n_pages // npg),
        in_specs=[col_spec(0), col_spec(1), col_spec(2),
                  pl.BlockSpec((4, LANES), lambda b, g, pt_ref: (0, 0)),
                  pl.BlockSpec((1, LANES), lambda b, g, pt_ref: (0, 0))]
                 + [page_spec(p) for p in range(npg)] + [page_spec(p) for p in range(npg)],
        out_specs=pl.BlockSpec((None, dec_seq, ATTN_WIDTH), lambda b, g, pt_ref: (b, 0, 0)),
        scratch_shapes=[pltpu.VMEM((nrow, ATTN_WIDTH), BF16),
                        pltpu.VMEM((PAGE_SIZE, ATTN_WIDTH), F32),
                        pltpu.VMEM((PAGE_SIZE, ATTN_WIDTH), F32),
                        pltpu.VMEM((nrow, 1), F32),
                        pltpu.VMEM((nrow, 1), F32),
                        pltpu.VMEM((nrow, V_DIM), F32)],
    )
    kern = functools.partial(_sattn_kernel, lam0=lam0, npg=npg)
    return pl.pallas_call(
        kern,
        grid_spec=grid_spec,
        out_shape=jax.ShapeDtypeStruct((dec_b, dec_seq, ATTN_WIDTH), F32),
        compiler_params=_params("parallel", "arbitrary"),
        name="sample_attention",
    )(pt, y3, y3, y3, lamv, sg, *([cache_kt] * npg), *([cache_vr] * npg))


def _conv_prompt_kernel(gb_ref, gc_ref, xc_ref, hgc_ref, hxc_ref, w_ref, c_ref, st_ref, *, tiles_per_seq):
    i = pl.program_id(0)
    u = gc_ref[...] * xc_ref[...]
    tm = u.shape[0]
    uh = hgc_ref[...] * hxc_ref[...]
    uh = jnp.where(i % tiles_per_seq == 0, 0.0, uh)
    row = lax.broadcasted_iota(jnp.int32, u.shape, 0)
    u1 = jnp.where(row == 0, uh[7:8], pltpu.roll(u, 1, 0))
    u2 = jnp.where(row == 0, uh[6:7], jnp.where(row == 1, uh[7:8], pltpu.roll(u, 2, 0)))
    w = w_ref[...]
    conv = w[0:1] * u2 + w[1:2] * u1 + w[2:3] * u
    c_ref[...] = (gb_ref[...] * conv).astype(c_ref.dtype)
    st_ref[...] = u[tm - (CONV_K - 1):tm]


def _conv_prompt(y, conv_w, batch, seq, tm):
    t = y.shape[0]
    tps = seq // tm
    hb = tm // 8
    kern = functools.partial(_conv_prompt_kernel, tiles_per_seq=tps)
    wide = lambda c: pl.BlockSpec((tm, CONV_WIDTH), lambda i: (i, c))
    halo = lambda c: pl.BlockSpec((8, CONV_WIDTH), lambda i: (jnp.maximum(i * hb - 1, 0), c))
    return pl.pallas_call(
        kern,
        grid=(t // tm,),
        in_specs=[wide(3), wide(4), wide(5), halo(4), halo(5),
                  pl.BlockSpec((CONV_K, CONV_WIDTH), lambda i: (0, 0))],
        out_specs=[pl.BlockSpec((tm, CONV_WIDTH), lambda i: (i, 0)),
                   pl.BlockSpec((None, CONV_K - 1, CONV_WIDTH), lambda i: (i // tps, 0, 0))],
        out_shape=[jax.ShapeDtypeStruct((t, CONV_WIDTH), BF16),
                   jax.ShapeDtypeStruct((batch, CONV_K - 1, CONV_WIDTH), F32)],
        compiler_params=_params("arbitrary"),
        name="conv_prompt",
    )(y, y, y, y, y, conv_w)


def _conv_sample_kernel(gb_ref, gc_ref, xc_ref, e1_ref, e2_ref, w_ref, c_ref, u_ref, *, dec_seq):
    u = gc_ref[...] * xc_ref[...]
    pos = lax.broadcasted_iota(jnp.int32, u.shape, 0) & (dec_seq - 1)
    u1 = jnp.where(pos == 0, e1_ref[...], pltpu.roll(u, 1, 0))
    u2 = jnp.where(pos <= 1, e2_ref[...], pltpu.roll(u, 2, 0))
    w = w_ref[...]
    conv = w[0:1] * u2 + w[1:2] * u1 + w[2:3] * u
    c_ref[...] = (gb_ref[...] * conv).astype(c_ref.dtype)
    u_ref[...] = u


def _conv_sample(y, conv_w, prev, dec_seq):
    t = y.shape[0]
    dec_b = t // dec_seq
    zeros = jnp.zeros((dec_b, dec_seq - 1, CONV_WIDTH), F32)
    e1 = jnp.concatenate([prev[:, 1:2], zeros], axis=1).reshape(t, CONV_WIDTH)
    e2 = jnp.concatenate([prev, zeros[:, 1:]], axis=1).reshape(t, CONV_WIDTH)
    kern = functools.partial(_conv_sample_kernel, dec_seq=dec_seq)
    wide = lambda c: pl.BlockSpec((t, CONV_WIDTH), lambda i: (0, c))
    full = pl.BlockSpec((t, CONV_WIDTH), lambda i: (0, 0))
    return pl.pallas_call(
        kern,
        grid=(1,),
        in_specs=[wide(3), wide(4), wide(5), full, full,
                  pl.BlockSpec((CONV_K, CONV_WIDTH), lambda i: (0, 0))],
        out_specs=[full, full],
        out_shape=[jax.ShapeDtypeStruct((t, CONV_WIDTH), BF16),
                   jax.ShapeDtypeStruct((t, CONV_WIDTH), F32)],
        compiler_params=_params("arbitrary"),
        name="conv_sample",
    )(y, y, y, e1, e2, conv_w)


def _outproj_kernel(x_ref, a_ref, c_ref, wa_ref, wc_ref, o_ref):
    o_ref[...] = (x_ref[...] + _dot(a_ref[...].astype(BF16), wa_ref[...])
                  + _dot(c_ref[...].astype(BF16), wc_ref[...]))


def _outproj(x, a, c, w, tm, tn, name):
    t, d = x.shape
    return pl.pallas_call(
        _outproj_kernel,
        grid=(t // tm, d // tn),
        in_specs=[pl.BlockSpec((tm, tn), lambda i, j: (i, j)),
                  pl.BlockSpec((tm, ATTN_WIDTH), lambda i, j: (i, 0)),
                  pl.BlockSpec((tm, CONV_WIDTH), lambda i, j: (i, 0)),
                  pl.BlockSpec((ATTN_WIDTH, tn), lambda i, j: (0, j)),
                  pl.BlockSpec((CONV_WIDTH, tn), lambda i, j: (1, j))],
        out_specs=pl.BlockSpec((tm, tn), lambda i, j: (i, j)),
        out_shape=jax.ShapeDtypeStruct((t, d), F32),
        compiler_params=_params("parallel", "arbitrary"),
        name=name,
    )(x, a, c, w, w)


def _top_values(s, k):
    out = []
    for _ in range(k):
        m = jnp.max(s, axis=0, keepdims=True)
        out.append(m)
        s = jnp.where(s == m, NEG_INF, s)
    return out


def _stack_rows(rows):
    n = len(rows)
    r = lax.broadcasted_iota(jnp.int32, (n, rows[0].shape[1]), 0)
    out = jnp.broadcast_to(rows[0], r.shape)
    for j in range(1, n):
        out = jnp.where(r == j, rows[j], out)
    return out


def _peer_select_kernel(x_ref, g_ref, hq_ref, sk_ref, xnt_ref, s2_ref, e2_ref, tau_ref, e1_ref):
    xn = _rms_rows(x_ref[...], g_ref[...])
    xnt_ref[...] = xn.T.astype(BF16)
    for h in range(PEER_HEADS):
        c0 = h * 2 * PEER_NKEYS
        q1 = hq_ref[:, c0:c0 + PEER_NKEYS].astype(BF16)
        q2 = hq_ref[:, c0 + PEER_NKEYS:c0 + 2 * PEER_NKEYS].astype(BF16)
        s1 = _dot_nt(sk_ref[h, 0], q1)
        s2 = _dot_nt(sk_ref[h, 1], q2)
        t1 = _top_values(s1, PEER_TOPK)
        t2 = _top_values(s2, PEER_TOPK)
        t2_all = _stack_rows(t2)
        cand = jnp.concatenate([t + t2_all for t in t1], axis=0)
        best = _top_values(cand, PEER_TOPK)
        thr = best[PEER_TOPK - 1]
        z = jnp.ones_like(best[0])
        for c in best[1:]:
            z = z + jnp.exp(c - best[0])
        tau = jnp.full(s1.shape, jnp.inf, F32)
        for tj in t2:
            tau = jnp.where(s1 + tj >= thr, tj, tau)
        s2_ref[h] = s2
        e2_ref[h] = jnp.exp(s2 - t2[0])
        tau_ref[h] = tau
        e1_ref[h] = jnp.exp(s1 - t1[0]) / z


def _peer_select(x, g, hq, subkeys, tm, name):
    t, d = x.shape
    nk = PEER_NKEYS
    head_arr = jax.ShapeDtypeStruct((PEER_HEADS, nk, t), F32)
    head_spec = pl.BlockSpec((PEER_HEADS, nk, tm), lambda i: (0, 0, i))
    return pl.pallas_call(
        _peer_select_kernel,
        grid=(t // tm,),
        in_specs=[pl.BlockSpec((tm, d), lambda i: (i, 0)),
                  pl.BlockSpec((1, d), lambda i: (0, 0)),
                  pl.BlockSpec((tm, PEER_HEADS * 2 * nk), lambda i: (i, 0)),
                  pl.BlockSpec((PEER_HEADS, 2, nk, nk), lambda i: (0, 0, 0, 0))],
        out_specs=[pl.BlockSpec((d, tm), lambda i: (0, i)),
                   head_spec, head_spec, head_spec, head_spec],
        out_shape=[jax.ShapeDtypeStruct((d, t), BF16),
                   head_arr, head_arr, head_arr, head_arr],
        compiler_params=_params("parallel"),
        name=name,
    )(x, g, hq, subkeys)


def _peer_dense_kernel(xnt_ref, s2_ref, e2_ref, tau_ref, e1_ref, u_ref, vt_ref, o_ref, gate_ref):
    e = pl.program_id(1)
    tm = xnt_ref.shape[1]
    n1_per_step = tau_ref.shape[1]

    a = _dot(u_ref[...], xnt_ref[...])

    for lt in range(tm // LANES):
        cols = slice(lt * LANES, (lt + 1) * LANES)

        def tile(q, carry, cols=cols):
            r0 = pl.multiple_of(q * GATE_ROWS, GATE_ROWS)
            w = [None] * n1_per_step
            for h in range(PEER_HEADS):
                s2t = s2_ref[h, pl.ds(r0, GATE_ROWS), cols]
                e2t = e2_ref[h, pl.ds(r0, GATE_ROWS), cols]
                for r in range(n1_per_step):
                    wh = jnp.where(s2t >= tau_ref[h, r:r + 1, cols], e2t, 0.0) * e1_ref[h, r:r + 1, cols]
                    w[r] = wh if w[r] is None else w[r] + wh
            for r in range(n1_per_step):
                gate_ref[pl.ds(r * PEER_NKEYS + r0, GATE_ROWS), cols] = w[r]
            return carry

        lax.fori_loop(0, PEER_NKEYS // GATE_ROWS, tile, 0)

    act = 0.5 * a * (1.0 + lax.erf(a * SQRT_HALF))
    contrib = _dot(vt_ref[...], (gate_ref[...] * act).astype(BF16))

    @pl.when(e == 0)
    def _():
        o_ref[...] = contrib

    @pl.when(e > 0)
    def _():
        o_ref[...] += contrib


def _peer_dense(xnt, s2, e2, tau, e1, u, vt, tm, name):
    d, t = xnt.shape
    ec = PEER_EC
    n1s = ec // PEER_NKEYS
    head_spec = pl.BlockSpec((PEER_HEADS, PEER_NKEYS, tm), lambda i, e: (0, 0, i))
    row_spec = pl.BlockSpec((PEER_HEADS, n1s, tm), lambda i, e: (0, e, i))
    return pl.pallas_call(
        _peer_dense_kernel,
        grid=(t // tm, PEER_N // ec),
        in_specs=[pl.BlockSpec((d, tm), lambda i, e: (0, i)),
                  head_spec, head_spec, row_spec, row_spec,
                  pl.BlockSpec((ec, d), lambda i, e: (e, 0)),
                  pl.BlockSpec((d, ec), lambda i, e: (0, e))],
        out_specs=pl.BlockSpec((d, tm), lambda i, e: (0, i)),
        out_shape=jax.ShapeDtypeStruct((d, t), F32),
        scratch_shapes=[pltpu.VMEM((ec, tm), F32)],
        compiler_params=_params("parallel", "arbitrary"),
        name=name,
    )(xnt, s2, e2, tau, e1, u, vt)


def _ple_kernel(x_ref, pt_ref, p_ref, g_ref, wg_ref, wp_ref, o_ref):
    x = x_ref[...] + pt_ref[...].T
    xn = _rms_rows(x, g_ref[...]).astype(BF16)
    z = _dot(xn, wg_ref[...])
    gate = 1.0 / (1.0 + jnp.exp(-z))
    o_ref[...] = x + gate * _dot(p_ref[...].astype(BF16), wp_ref[...])


def _ple(x, peer_t, p, g, wg, wp, tm, name):
    t, d = x.shape
    return pl.pallas_call(
        _ple_kernel,
        grid=(t // tm,),
        in_specs=[pl.BlockSpec((tm, d), lambda i: (i, 0)),
                  pl.BlockSpec((d, tm), lambda i: (0, i)),
                  pl.BlockSpec((tm, PLE_DIM), lambda i: (i, 0)),
                  pl.BlockSpec((1, d), lambda i: (0, 0)),
                  pl.BlockSpec((d, d), lambda i: (0, 0)),
                  pl.BlockSpec((PLE_DIM, d), lambda i: (0, 0))],
        out_specs=pl.BlockSpec((tm, d), lambda i: (i, 0)),
        out_shape=jax.ShapeDtypeStruct((t, d), F32),
        compiler_params=_params("parallel"),
        name=name,
    )(x, peer_t, p, g, wg, wp)


def _lambda_init(layer_idx):
    return 0.8 - 0.6 * math.exp(-0.3 * layer_idx)


def _token_tail(x, a, c, lw, p_l, tm, tag):
    x1 = _outproj(x, a, c, lw["w_out"], tm, D_MODEL // 2, "outproj_" + tag)
    hq = _norm_mm(x1, lw["ffn_norm_g"], lw["peer_wq"], tm, D_MODEL // 2, "peer_query_" + tag)
    xnt, s2, e2, tau, e1 = _peer_select(x1, lw["ffn_norm_g"], hq, lw["peer_subkeys"], tm,
                                        "peer_select_" + tag)
    peer_t = _peer_dense(xnt, s2, e2, tau, e1, lw["peer_u"], lw["peer_vt"], tm, "peer_dense_" + tag)
    return _ple(x1, peer_t, p_l, lw["ple_norm_g"], lw["ple_gate_w"], lw["ple_proj_w"],
                min(tm, PLE_TM), "ple_" + tag)


def kernel(x_prompt, x_sample, cache_k, cache_v, state_conv, page_table, p_prompt, p_sample,
           attn_norm_g, w_in, q_norm_g, k_norm_g, lam_q1, lam_k1, lam_q2, lam_k2, subln_g,
           conv_w, w_out, ffn_norm_g, peer_wq, peer_subkeys, peer_u, peer_v,
           ple_norm_g, ple_gate_w, ple_proj_w):
    batch, seq, d = x_prompt.shape
    dec_b, dec_seq, _ = x_sample.shape
    depth = w_in.shape[0]
    n_pool = cache_k.shape[1]
    past_len = page_table.shape[1] * cache_k.shape[2]
    tp, ts = batch * seq, dec_b * dec_seq

    rope_p = _rope_tables(jnp.arange(seq))
    rope_s = _rope_tables(past_len + (jnp.arange(ts) % dec_seq))
    ck = cache_k.transpose(0, 1, 3, 4, 5, 2).reshape(depth, n_pool, ATTN_WIDTH, PAGE_SIZE)
    cv = cache_v.reshape(depth, n_pool, PAGE_SIZE * N_HEADS, V_DIM)

    yp = x_prompt.reshape(tp, d)
    ys = x_sample.reshape(ts, d)
    outs = [[] for _ in range(6)]
    row = lambda v: v.reshape(1, -1)
    for l in range(depth):
        lam0 = _lambda_init(l)
        lw = {
            "w_out": w_out[l].astype(BF16),
            "ffn_norm_g": row(ffn_norm_g[l]),
            "peer_wq": peer_wq[l].astype(BF16),
            "peer_subkeys": peer_subkeys[l].astype(BF16),
            "peer_u": peer_u[l].astype(BF16),
            "peer_vt": peer_v[l].T.astype(BF16),
            "ple_norm_g": row(ple_norm_g[l]),
            "ple_gate_w": ple_gate_w[l].astype(BF16),
            "ple_proj_w": ple_proj_w[l].astype(BF16),
        }
        w_in_l = w_in[l].astype(BF16)
        g_in = row(attn_norm_g[l])
        qkg = jnp.stack([jnp.tile(q_norm_g[l], 2), jnp.tile(k_norm_g[l], 2)]).reshape(2, 1, LANES)
        lamv = jnp.pad(jnp.stack([lam_q1[l], lam_k1[l], lam_q2[l], lam_k2[l]]),
                       ((0, 0), (0, LANES - HEAD_DIM)))
        sg = row(subln_g[l])

        y = _inproj(yp, g_in, w_in_l, qkg, rope_p, PROMPT_TM, "inproj_prompt")
        a = _prompt_attention(y, lamv, sg, lam0, batch, seq)
        c, cstate = _conv_prompt(y, conv_w[l], batch, seq, PROMPT_TM)
        outs[0].append(y[:, ATTN_WIDTH:2 * ATTN_WIDTH].reshape(batch, seq, N_HEADS, 2, HEAD_DIM))
        outs[1].append(y[:, 2 * ATTN_WIDTH:3 * ATTN_WIDTH].reshape(batch, seq, N_HEADS, V_DIM))
        outs[2].append(cstate)
        yp = _token_tail(yp, a, c, lw, p_prompt[l].reshape(tp, PLE_DIM), PROMPT_TM, "prompt")

        y = _inproj(ys, g_in, w_in_l, qkg, rope_s, ts, "inproj_sample")
        a = _sample_attention(y.reshape(dec_b, dec_seq, IN_COLS), ck, cv, page_table, l, lamv, sg, lam0)
        c, u = _conv_sample(y, conv_w[l], state_conv[l], dec_seq)
        outs[3].append(y[:, ATTN_WIDTH:2 * ATTN_WIDTH].reshape(dec_b, dec_seq, N_HEADS, 2, HEAD_DIM))
        outs[4].append(y[:, 2 * ATTN_WIDTH:3 * ATTN_WIDTH].reshape(dec_b, dec_seq, N_HEADS, V_DIM))
        outs[5].append(u.reshape(dec_b, dec_seq, CONV_WIDTH)[:, dec_seq - (CONV_K - 1):])
        ys = _token_tail(ys, a.reshape(ts, ATTN_WIDTH), c, lw, p_sample[l].reshape(ts, PLE_DIM), ts, "sample")

    return (yp.reshape(batch, seq, d), ys.reshape(dec_b, dec_seq, d),
            jnp.stack(outs[0]), jnp.stack(outs[1]), jnp.stack(outs[2]),
            jnp.stack(outs[3]), jnp.stack(outs[4]), jnp.stack(outs[5]))
```

```python
import functools
import math

import jax
import jax.numpy as jnp
from jax import lax
from jax.experimental import pallas as pl
from jax.experimental.pallas import tpu as pltpu

F32 = jnp.float32
BF16 = jnp.bfloat16

D_MODEL = 2048
N_HEADS = 8
HEAD_DIM = 64
V_DIM = 128
ATTN_WIDTH = 1024
CONV_WIDTH = 1024
IN_COLS = 3 * ATTN_WIDTH + 3 * CONV_WIDTH
ROT_DIM = 16
ROPE_THETA = 500000.0
CONV_K = 3
PLE_DIM = 256
PEER_HEADS = 8
PEER_NKEYS = 128
PEER_N = PEER_NKEYS * PEER_NKEYS
PEER_TOPK = 16
EPS = 1e-6
PAGE_SIZE = 128

LANES = 128
VMEM_LIMIT_BYTES = 56 * 1024 * 1024
NEG_INF = float("-inf")
MASK_VALUE = float(jnp.finfo(jnp.float32).min)
SQRT_HALF = 0.7071067811865476

PROMPT_TM = 512
ATTN_TQ = 512
ATTN_RB = 256
PAGES_PER_STEP = 8
PEER_EC = 1024
SELECT_TM = 256
GATE_GROUP = 16
GATE_PITCH = PEER_NKEYS + 8
PLE_TM = 256


def _params(*sem):
    return pltpu.CompilerParams(dimension_semantics=sem, vmem_limit_bytes=VMEM_LIMIT_BYTES)


def _rms_rows(x, g):
    ms = jnp.mean(x * x, axis=-1, keepdims=True)
    return x * lax.rsqrt(ms + EPS) * g


def _dot(a, b):
    return jnp.dot(a, b, preferred_element_type=F32)


def _dot_nt(a, b):
    return lax.dot_general(a, b, (((1,), (1,)), ((), ())), preferred_element_type=F32)


def _norm_mm_kernel(x_ref, g_ref, w_ref, o_ref, hn_ref):
    @pl.when(pl.program_id(1) == 0)
    def _():
        hn_ref[...] = _rms_rows(x_ref[...], g_ref[...]).astype(BF16)

    o_ref[...] = _dot(hn_ref[...], w_ref[...])


def _inproj_kernel(x_ref, g_ref, w_ref, qkg_ref, ra_ref, rp_ref, rm_ref, o_ref, hn_ref):
    j = pl.program_id(1)

    @pl.when(j == 0)
    def _():
        hn_ref[...] = _rms_rows(x_ref[...], g_ref[...]).astype(BF16)

    y = _dot(hn_ref[...], w_ref[...])

    @pl.when(j < 2)
    def _():
        r = lax.broadcasted_iota(jnp.int32, (LANES, LANES), 0) // HEAD_DIM
        c = lax.broadcasted_iota(jnp.int32, (LANES, LANES), 1) // HEAD_DIM
        group_sum = jnp.where(r == c, 1.0, 0.0).astype(BF16)
        for h in range(N_HEADS):
            yh = y[:, h * LANES:(h + 1) * LANES]
            sq = yh * yh
            hi = sq.astype(BF16)
            lo = (sq - hi.astype(F32)).astype(BF16)
            ss = _dot(hi, group_sum) + _dot(lo, group_sum)
            yn = yh * lax.rsqrt(ss * (1.0 / HEAD_DIM) + EPS) * qkg_ref[...]
            o_ref[:, h * LANES:(h + 1) * LANES] = (
                yn * ra_ref[...]
                + pltpu.roll(yn, ROT_DIM // 2, 1) * rp_ref[...]
                + pltpu.roll(yn, LANES - ROT_DIM // 2, 1) * rm_ref[...])

    @pl.when(j >= 2)
    def _():
        o_ref[...] = y


def _norm_mm(x, g, w, tm, tn, name):
    t, d = x.shape
    n = w.shape[1]
    return pl.pallas_call(
        _norm_mm_kernel,
        grid=(t // tm, n // tn),
        in_specs=[pl.BlockSpec((tm, d), lambda i, j: (i, 0)),
                  pl.BlockSpec((1, d), lambda i, j: (0, 0)),
                  pl.BlockSpec((d, tn), lambda i, j: (0, j))],
        out_specs=pl.BlockSpec((tm, tn), lambda i, j: (i, j)),
        out_shape=jax.ShapeDtypeStruct((t, n), F32),
        scratch_shapes=[pltpu.VMEM((tm, d), BF16)],
        compiler_params=_params("parallel", "arbitrary"),
        name=name,
    )(x, g, w)


def _inproj(x, g, w, qkg, rope, tm, name):
    t, d = x.shape
    tn = ATTN_WIDTH
    ra, rp, rm = rope
    nrep = ra.shape[0] // tm
    rope_spec = pl.BlockSpec((tm, LANES), lambda i, j: (i % nrep, 0))
    return pl.pallas_call(
        _inproj_kernel,
        grid=(t // tm, IN_COLS // tn),
        in_specs=[pl.BlockSpec((tm, d), lambda i, j: (i, 0)),
                  pl.BlockSpec((1, d), lambda i, j: (0, 0)),
                  pl.BlockSpec((d, tn), lambda i, j: (0, j)),
                  pl.BlockSpec((None, 1, LANES), lambda i, j: (jnp.minimum(j, 1), 0, 0)),
                  rope_spec, rope_spec, rope_spec],
        out_specs=pl.BlockSpec((tm, tn), lambda i, j: (i, j)),
        out_shape=jax.ShapeDtypeStruct((t, IN_COLS), F32),
        scratch_shapes=[pltpu.VMEM((tm, d), BF16)],
        compiler_params=_params("parallel", "arbitrary"),
        name=name,
    )(x, g, w, qkg, ra, rp, rm)


def _rope_tables(pos):
    inv_freq = ROPE_THETA ** (-jnp.arange(0, ROT_DIM, 2, dtype=F32) / ROT_DIM)
    ang = pos.astype(F32)[:, None] * inv_freq[None, :]
    cos, sin = jnp.cos(ang), jnp.sin(ang)
    p = pos.shape[0]
    half = ROT_DIM // 2
    one = jnp.ones((p, HEAD_DIM - ROT_DIM), F32)
    zero = jnp.zeros((p, HEAD_DIM - ROT_DIM), F32)
    zh = jnp.zeros((p, half), F32)
    a = jnp.concatenate([cos, cos, one], axis=-1)
    bp = jnp.concatenate([zh, sin, zero], axis=-1)
    bm = jnp.concatenate([-sin, zh, zero], axis=-1)
    return tuple(jnp.concatenate([t, t], axis=-1) for t in (a, bp, bm))


def _lambda_full(lamv_ref, lam0):
    lv = lamv_ref[...]
    s1 = jnp.sum(lv[0:1] * lv[1:2], axis=-1, keepdims=True)
    s2 = jnp.sum(lv[2:3] * lv[3:4], axis=-1, keepdims=True)
    return jnp.exp(s1) - jnp.exp(s2) + lam0


def _pattn_kernel(q_ref, k_ref, v_ref, lamv_ref, sg_ref, o_ref, kb_ref, vb_ref, m_ref, acc_ref,
                  *, lam0, tq):
    qi = pl.program_id(2)
    seq = k_ref.shape[0]

    @pl.when(qi == 0)
    def _():
        kb_ref[...] = k_ref[...].astype(BF16)
        vb_ref[:, :V_DIM] = v_ref[...].astype(BF16)
        lane = lax.broadcasted_iota(jnp.int32, (seq, LANES), 1)
        vb_ref[:, V_DIM:] = jnp.where(lane == 0, 1.0, 0.0).astype(BF16)

    q = q_ref[...] * (HEAD_DIM ** -0.5)
    lane = lax.broadcasted_iota(jnp.int32, (tq, LANES), 1)
    q0 = jnp.where(lane < HEAD_DIM, q, 0.0).astype(BF16)
    q1 = jnp.where(lane >= HEAD_DIM, q, 0.0).astype(BF16)
    qq = jnp.concatenate([q0, q1], axis=0)

    m_ref[...] = jnp.full(m_ref.shape, NEG_INF, F32)
    acc_ref[...] = jnp.zeros(acc_ref.shape, F32)

    def chunk(kj, masked):
        off = pl.multiple_of(kj * tq, tq)
        ks = kb_ref[pl.ds(off, tq), :]
        vs = vb_ref[pl.ds(off, tq), :]
        for rb in range(2 * tq // ATTN_RB):
            rs = slice(rb * ATTN_RB, (rb + 1) * ATTN_RB)
            s = _dot_nt(qq[rs], ks)
            if masked:
                row = lax.broadcasted_iota(jnp.int32, s.shape, 0) + rb * ATTN_RB
                col = lax.broadcasted_iota(jnp.int32, s.shape, 1)
                s = jnp.where(col <= (row & (tq - 1)), s, MASK_VALUE)
            m_old = m_ref[rs, :]
            m_new = jnp.maximum(m_old, jnp.max(s, axis=-1, keepdims=True))
            alpha = jnp.exp(m_old - m_new)
            p = jnp.exp(s - m_new).astype(BF16)
            acc_ref[rs, :] = alpha * acc_ref[rs, :] + _dot(p, vs)
            m_ref[rs, :] = m_new

    def body(kj, carry):
        chunk(kj, False)
        return carry

    lax.fori_loop(0, qi, body, 0)
    chunk(qi, True)

    acc = acc_ref[...]
    o = acc[:, :V_DIM] / acc[:, V_DIM:V_DIM + 1]
    lam = _lambda_full(lamv_ref, lam0)
    a = o[:tq] - lam * o[tq:]
    o_ref[...] = (_rms_rows(a, sg_ref[...]) * (1.0 - lam0)).astype(o_ref.dtype)


def _prompt_attention(y, lamv, sg, lam0, batch, seq):
    tq = ATTN_TQ
    nq = seq // tq
    kern = functools.partial(_pattn_kernel, lam0=lam0, tq=tq)
    return pl.pallas_call(
        kern,
        grid=(batch, N_HEADS, nq),
        in_specs=[pl.BlockSpec((tq, LANES), lambda b, h, i: (b * nq + i, h)),
                  pl.BlockSpec((seq, LANES), lambda b, h, i: (b, N_HEADS + h)),
                  pl.BlockSpec((seq, LANES), lambda b, h, i: (b, 2 * N_HEADS + h)),
                  pl.BlockSpec((4, LANES), lambda b, h, i: (0, 0)),
                  pl.BlockSpec((1, LANES), lambda b, h, i: (0, 0))],
        out_specs=pl.BlockSpec((tq, LANES), lambda b, h, i: (b * nq + i, h)),
        out_shape=jax.ShapeDtypeStruct((batch * seq, ATTN_WIDTH), BF16),
        scratch_shapes=[pltpu.VMEM((seq, LANES), BF16),
                        pltpu.VMEM((seq, 2 * LANES), BF16),
                        pltpu.VMEM((2 * tq, 1), F32),
                        pltpu.VMEM((2 * tq, 2 * LANES), F32)],
        compiler_params=_params("parallel", "parallel", "arbitrary"),
        name="prompt_attention",
    )(y, y, y, lamv, sg)


def _sattn_kernel(pt_ref, q_ref, kn_ref, vn_ref, lamv_ref, sg_ref, *rest, lam0, npg):
    k_refs = rest[:npg]
    v_refs = rest[npg:2 * npg]
    o_ref, qt_ref, knp_ref, vnp_ref, m_ref, l_ref, acc_ref = rest[2 * npg:]
    g = pl.program_id(1)
    dec_seq = q_ref.shape[0]
    rows = 2 * dec_seq
    nrow = N_HEADS * rows

    @pl.when(g == 0)
    def _():
        q = q_ref[...] * (HEAD_DIM ** -0.5)
        row = lax.broadcasted_iota(jnp.int32, (nrow, ATTN_WIDTH), 0)
        lane = lax.broadcasted_iota(jnp.int32, (nrow, ATTN_WIDTH), 1)
        qq = jnp.zeros((nrow, ATTN_WIDTH), F32)
        for i in range(dec_seq):
            qq = jnp.where((row & (dec_seq - 1)) == i, q[i:i + 1, :], qq)
        qt_ref[...] = jnp.where(lane // HEAD_DIM == row // dec_seq, qq, 0.0).astype(BF16)
        knp_ref[...] = jnp.zeros(knp_ref.shape, F32)
        vnp_ref[...] = jnp.zeros(vnp_ref.shape, F32)
        m_ref[...] = jnp.full(m_ref.shape, NEG_INF, F32)
        l_ref[...] = jnp.zeros(l_ref.shape, F32)
        acc_ref[...] = jnp.zeros(acc_ref.shape, F32)

    def attend(st, weighted_values):
        m_old = m_ref[...]
        m_new = jnp.maximum(m_old, jnp.max(st, axis=-1, keepdims=True))
        alpha = jnp.exp(m_old - m_new)
        p = jnp.exp(st - m_new)
        l_ref[...] = alpha * l_ref[...] + jnp.sum(p, axis=-1, keepdims=True)
        acc_ref[...] = alpha * acc_ref[...] + weighted_values(p)
        m_ref[...] = m_new

    kc = jnp.concatenate([r[...].astype(BF16) for r in k_refs], axis=1)
    st = _dot(qt_ref[...], kc)

    def page_values(p):
        outs = []
        for h in range(N_HEADS):
            vh = jnp.concatenate(
                [r[pl.ds(h, PAGE_SIZE, stride=N_HEADS), :].astype(BF16) for r in v_refs], axis=0)
            outs.append(_dot(p[h * rows:(h + 1) * rows, :].astype(BF16), vh))
        return jnp.concatenate(outs, axis=0)

    attend(st, page_values)

    @pl.when(g == pl.num_programs(1) - 1)
    def _():
        knp_ref[0:dec_seq, :] = kn_ref[...]
        vnp_ref[0:dec_seq, :] = vn_ref[...]
        sn = _dot_nt(qt_ref[...], knp_ref[...].astype(BF16))
        row = lax.broadcasted_iota(jnp.int32, sn.shape, 0)
        col = lax.broadcasted_iota(jnp.int32, sn.shape, 1)
        sn = jnp.where(col <= (row & (dec_seq - 1)), sn, MASK_VALUE)

        def new_values(p):
            o = _dot(p.astype(BF16), vnp_ref[...].astype(BF16))
            return jnp.concatenate(
                [o[h * rows:(h + 1) * rows, h * V_DIM:(h + 1) * V_DIM] for h in range(N_HEADS)], axis=0)

        attend(sn, new_values)

        o = acc_ref[...] / l_ref[...]
        lam = _lambda_full(lamv_ref, lam0)
        a = o - lam * pltpu.roll(o, nrow - dec_seq, 0)
        a = _rms_rows(a, sg_ref[...]) * (1.0 - lam0)
        for h in range(N_HEADS):
            o_ref[:, h * V_DIM:(h + 1) * V_DIM] = a[h * rows:h * rows + dec_seq, :]


def _sample_attention(y3, cache_kt, cache_vr, page_table, layer, lamv, sg, lam0):
    dec_b, dec_seq, _ = y3.shape
    n_pages = page_table.shape[1]
    npg = PAGES_PER_STEP
    assert n_pages % npg == 0 and dec_seq & (dec_seq - 1) == 0
    nrow = N_HEADS * 2 * dec_seq
    pt = page_table.reshape(-1)

    def page_spec(p):
        return pl.BlockSpec((None, None, ATTN_WIDTH, PAGE_SIZE),
                            lambda b, g, pt_ref: (layer, pt_ref[b * n_pages + g * npg + p], 0, 0))

    def col_spec(c):
        return pl.BlockSpec((None, dec_seq, ATTN_WIDTH), lambda b, g, pt_ref: (b, 0, c))

    grid_spec = pltpu.PrefetchScalarGridSpec(
        num_scalar_prefetch=1,
        grid=(dec_b, n_pages // npg),
        in_specs=[col_spec(0), col_spec(1), col_spec(2),
                  pl.BlockSpec((4, LANES), lambda b, g, pt_ref: (0, 0)),
                  pl.BlockSpec((1, LANES), lambda b, g, pt_ref: (0, 0))]
                 + [page_spec(p) for p in range(npg)] + [page_spec(p) for p in range(npg)],
        out_specs=pl.BlockSpec((None, dec_seq, ATTN_WIDTH), lambda b, g, pt_ref: (b, 0, 0)),
        scratch_shapes=[pltpu.VMEM((nrow, ATTN_WIDTH), BF16),
                        pltpu.VMEM((PAGE_SIZE, ATTN_WIDTH), F32),
                        pltpu.VMEM((PAGE_SIZE, ATTN_WIDTH), F32),
                        pltpu.VMEM((nrow, 1), F32),
                        pltpu.VMEM((nrow, 1), F32),
                        pltpu.VMEM((nrow, V_DIM), F32)],
    )
    kern = functools.partial(_sattn_kernel, lam0=lam0, npg=npg)
    return pl.pallas_call(
        kern,
        grid_spec=grid_spec,
        out_shape=jax.ShapeDtypeStruct((dec_b, dec_seq, ATTN_WIDTH), F32),
        compiler_params=_params("parallel", "arbitrary"),
        name="sample_attention",
    )(pt, y3, y3, y3, lamv, sg, *([cache_kt] * npg), *([cache_vr] * npg))


def _conv_prompt_kernel(gb_ref, gc_ref, xc_ref, hgc_ref, hxc_ref, w_ref, c_ref, st_ref, *, tiles_per_seq):
    i = pl.program_id(0)
    u = gc_ref[...] * xc_ref[...]
    tm = u.shape[0]
    uh = hgc_ref[...] * hxc_ref[...]
    uh = jnp.where(i % tiles_per_seq == 0, 0.0, uh)
    row = lax.broadcasted_iota(jnp.int32, u.shape, 0)
    u1 = jnp.where(row == 0, uh[7:8], pltpu.roll(u, 1, 0))
    u2 = jnp.where(row == 0, uh[6:7], jnp.where(row == 1, uh[7:8], pltpu.roll(u, 2, 0)))
    w = w_ref[...]
    conv = w[0:1] * u2 + w[1:2] * u1 + w[2:3] * u
    c_ref[...] = (gb_ref[...] * conv).astype(c_ref.dtype)
    st_ref[...] = u[tm - (CONV_K - 1):tm]


def _conv_prompt(y, conv_w, batch, seq, tm):
    t = y.shape[0]
    tps = seq // tm
    hb = tm // 8
    kern = functools.partial(_conv_prompt_kernel, tiles_per_seq=tps)
    wide = lambda c: pl.BlockSpec((tm, CONV_WIDTH), lambda i: (i, c))
    halo = lambda c: pl.BlockSpec((8, CONV_WIDTH), lambda i: (jnp.maximum(i * hb - 1, 0), c))
    return pl.pallas_call(
        kern,
        grid=(t // tm,),
        in_specs=[wide(3), wide(4), wide(5), halo(4), halo(5),
                  pl.BlockSpec((CONV_K, CONV_WIDTH), lambda i: (0, 0))],
        out_specs=[pl.BlockSpec((tm, CONV_WIDTH), lambda i: (i, 0)),
                   pl.BlockSpec((None, CONV_K - 1, CONV_WIDTH), lambda i: (i // tps, 0, 0))],
        out_shape=[jax.ShapeDtypeStruct((t, CONV_WIDTH), BF16),
                   jax.ShapeDtypeStruct((batch, CONV_K - 1, CONV_WIDTH), F32)],
        compiler_params=_params("arbitrary"),
        name="conv_prompt",
    )(y, y, y, y, y, conv_w)


def _conv_sample_kernel(gb_ref, gc_ref, xc_ref, e1_ref, e2_ref, w_ref, c_ref, u_ref, *, dec_seq):
    u = gc_ref[...] * xc_ref[...]
    pos = lax.broadcasted_iota(jnp.int32, u.shape, 0) & (dec_seq - 1)
    u1 = jnp.where(pos == 0, e1_ref[...], pltpu.roll(u, 1, 0))
    u2 = jnp.where(pos <= 1, e2_ref[...], pltpu.roll(u, 2, 0))
    w = w_ref[...]
    conv = w[0:1] * u2 + w[1:2] * u1 + w[2:3] * u
    c_ref[...] = (gb_ref[...] * conv).astype(c_ref.dtype)
    u_ref[...] = u


def _conv_sample(y, conv_w, prev, dec_seq):
    t = y.shape[0]
    dec_b = t // dec_seq
    zeros = jnp.zeros((dec_b, dec_seq - 1, CONV_WIDTH), F32)
    e1 = jnp.concatenate([prev[:, 1:2], zeros], axis=1).reshape(t, CONV_WIDTH)
    e2 = jnp.concatenate([prev, zeros[:, 1:]], axis=1).reshape(t, CONV_WIDTH)
    kern = functools.partial(_conv_sample_kernel, dec_seq=dec_seq)
    wide = lambda c: pl.BlockSpec((t, CONV_WIDTH), lambda i: (0, c))
    full = pl.BlockSpec((t, CONV_WIDTH), lambda i: (0, 0))
    return pl.pallas_call(
        kern,
        grid=(1,),
        in_specs=[wide(3), wide(4), wide(5), full, full,
                  pl.BlockSpec((CONV_K, CONV_WIDTH), lambda i: (0, 0))],
        out_specs=[full, full],
        out_shape=[jax.ShapeDtypeStruct((t, CONV_WIDTH), BF16),
                   jax.ShapeDtypeStruct((t, CONV_WIDTH), F32)],
        compiler_params=_params("arbitrary"),
        name="conv_sample",
    )(y, y, y, e1, e2, conv_w)


def _outproj_kernel(x_ref, a_ref, c_ref, wa_ref, wc_ref, o_ref):
    o_ref[...] = (x_ref[...] + _dot(a_ref[...].astype(BF16), wa_ref[...])
                  + _dot(c_ref[...].astype(BF16), wc_ref[...]))


def _outproj(x, a, c, w, tm, tn, name):
    t, d = x.shape
    return pl.pallas_call(
        _outproj_kernel,
        grid=(t // tm, d // tn),
        in_specs=[pl.BlockSpec((tm, tn), lambda i, j: (i, j)),
                  pl.BlockSpec((tm, ATTN_WIDTH), lambda i, j: (i, 0)),
                  pl.BlockSpec((tm, CONV_WIDTH), lambda i, j: (i, 0)),
                  pl.BlockSpec((ATTN_WIDTH, tn), lambda i, j: (0, j)),
                  pl.BlockSpec((CONV_WIDTH, tn), lambda i, j: (1, j))],
        out_specs=pl.BlockSpec((tm, tn), lambda i, j: (i, j)),
        out_shape=jax.ShapeDtypeStruct((t, d), F32),
        compiler_params=_params("parallel", "arbitrary"),
        name=name,
    )(x, a, c, w, w)


def _top_with_index(s, k):
    nrows = s.shape[0]
    riota = lax.broadcasted_iota(jnp.int32, s.shape, 0).astype(F32)
    vals, idxs = [], []
    for _ in range(k):
        m = jnp.max(s, axis=0, keepdims=True)
        i = jnp.min(jnp.where(s == m, riota, float(nrows)), axis=0, keepdims=True)
        vals.append(m)
        idxs.append(i)
        s = jnp.where(riota == i, NEG_INF, s)
    return vals, idxs


def _stack_rows(rows, nrows, fill):
    r = lax.broadcasted_iota(jnp.int32, (nrows, rows[0].shape[1]), 0)
    out = jnp.full(r.shape, fill, F32)
    for j, row in enumerate(rows):
        out = jnp.where(r == j, row, out)
    return out


_CAND_PAIRS = [(j1, j2) for j1 in range(PEER_TOPK) for j2 in range(PEER_TOPK) if (j1 + 1) * (j2 + 1) <= PEER_TOPK]
_CAND_ROWS = -(-len(_CAND_PAIRS) // 8) * 8


def _peer_select_kernel(x_ref, g_ref, hq_ref, sk_ref, xn_ref, gate_ref, at_ref, bt_ref, gt_ref,
                        atm_ref, btm_ref, gtm_ref, m_ref):
    tm = x_ref.shape[0]
    xn_ref[...] = _rms_rows(x_ref[...], g_ref[...]).astype(BF16)

    for h in range(PEER_HEADS):
        c0 = h * 2 * PEER_NKEYS
        q1 = hq_ref[:, c0:c0 + PEER_NKEYS].astype(BF16)
        q2 = hq_ref[:, c0 + PEER_NKEYS:c0 + 2 * PEER_NKEYS].astype(BF16)
        s1 = _dot_nt(sk_ref[h, 0], q1)
        s2 = _dot_nt(sk_ref[h, 1], q2)
        t1, i1 = _top_with_index(s1, PEER_TOPK)
        t2, i2 = _top_with_index(s2, PEER_TOPK)
        cand = _stack_rows([t1[j1] + t2[j2] for j1, j2 in _CAND_PAIRS], _CAND_ROWS, NEG_INF)
        cand_n1 = _stack_rows([i1[j1] for j1, _ in _CAND_PAIRS], _CAND_ROWS, 0.0)
        cand_n2 = _stack_rows([i2[j2] for _, j2 in _CAND_PAIRS], _CAND_ROWS, 0.0)
        sc, rows = _top_with_index(cand, PEER_TOPK)
        riota = lax.broadcasted_iota(jnp.int32, cand.shape, 0).astype(F32)
        ex = [jnp.exp(c - sc[0]) for c in sc]
        z = ex[0]
        for v in ex[1:]:
            z = z + v
        for j in range(PEER_TOPK):
            pick = riota == rows[j]
            slot = h * PEER_TOPK + j
            at_ref[slot:slot + 1, :] = jnp.sum(jnp.where(pick, cand_n1, 0.0), axis=0, keepdims=True)
            bt_ref[slot:slot + 1, :] = jnp.sum(jnp.where(pick, cand_n2, 0.0), axis=0, keepdims=True)
            gt_ref[slot:slot + 1, :] = ex[j] / z

    atm_ref[...] = at_ref[...].T
    btm_ref[...] = bt_ref[...].T
    gtm_ref[...] = gt_ref[...].T

    niota = lax.broadcasted_iota(jnp.int32, (PEER_NKEYS, LANES), 0).astype(F32)

    def group(gi, carry):
        t0 = pl.multiple_of(gi * GATE_GROUP, GATE_GROUP)
        for tt in range(GATE_GROUP):
            a_row = atm_ref[pl.ds(t0 + tt, 1), :]
            b_row = btm_ref[pl.ds(t0 + tt, 1), :]
            g_row = gtm_ref[pl.ds(t0 + tt, 1), :]
            lhs = jnp.where(niota == a_row, g_row, 0.0).astype(BF16)
            rhs = jnp.where(niota == b_row, 1.0, 0.0).astype(BF16)
            m_ref[tt * GATE_PITCH:tt * GATE_PITCH + PEER_NKEYS, :] = _dot_nt(lhs, rhs)
        for n1 in range(PEER_NKEYS):
            rows_n1 = m_ref[pl.ds(n1, GATE_GROUP, stride=GATE_PITCH), :]
            gate_ref[pl.ds(t0, GATE_GROUP), n1 * PEER_NKEYS:(n1 + 1) * PEER_NKEYS] = rows_n1.astype(BF16)
        return carry

    lax.fori_loop(0, tm // GATE_GROUP, group, 0)


def _peer_select(x, g, hq, subkeys, tm, name):
    t, d = x.shape
    nk = PEER_NKEYS
    slots = PEER_HEADS * PEER_TOPK
    return pl.pallas_call(
        _peer_select_kernel,
        grid=(t // tm,),
        in_specs=[pl.BlockSpec((tm, d), lambda i: (i, 0)),
                  pl.BlockSpec((1, d), lambda i: (0, 0)),
                  pl.BlockSpec((tm, PEER_HEADS * 2 * nk), lambda i: (i, 0)),
                  pl.BlockSpec((PEER_HEADS, 2, nk, nk), lambda i: (0, 0, 0, 0))],
        out_specs=[pl.BlockSpec((tm, d), lambda i: (i, 0)),
                   pl.BlockSpec((tm, PEER_N), lambda i: (i, 0))],
        out_shape=[jax.ShapeDtypeStruct((t, d), BF16),
                   jax.ShapeDtypeStruct((t, PEER_N), BF16)],
        scratch_shapes=[pltpu.VMEM((slots, tm), F32)] * 3 + [pltpu.VMEM((tm, slots), F32)] * 3
                       + [pltpu.VMEM((GATE_GROUP * GATE_PITCH, nk), F32)],
        compiler_params=_params("parallel"),
        name=name,
    )(x, g, hq, subkeys)


def _peer_dense_kernel(xn_ref, gate_ref, ut_ref, v_ref, o_ref):
    e = pl.program_id(1)
    a = _dot(xn_ref[...], ut_ref[...])
    act = 0.5 * a * (1.0 + lax.erf(a * SQRT_HALF))
    contrib = _dot((gate_ref[...].astype(F32) * act).astype(BF16), v_ref[...])

    @pl.when(e == 0)
    def _():
        o_ref[...] = contrib

    @pl.when(e > 0)
    def _():
        o_ref[...] += contrib


def _peer_dense(xn, gate, ut, v, tm, name):
    t, d = xn.shape
    ec = PEER_EC
    return pl.pallas_call(
        _peer_dense_kernel,
        grid=(t // tm, PEER_N // ec),
        in_specs=[pl.BlockSpec((tm, d), lambda i, e: (i, 0)),
                  pl.BlockSpec((tm, ec), lambda i, e: (i, e)),
                  pl.BlockSpec((d, ec), lambda i, e: (0, e)),
                  pl.BlockSpec((ec, d), lambda i, e: (e, 0))],
        out_specs=pl.BlockSpec((tm, d), lambda i, e: (i, 0)),
        out_shape=jax.ShapeDtypeStruct((t, d), F32),
        compiler_params=_params("parallel", "arbitrary"),
        name=name,
    )(xn, gate, ut, v)


def _ple_kernel(x_ref, peer_ref, p_ref, g_ref, wg_ref, wp_ref, o_ref):
    x = x_ref[...] + peer_ref[...]
    xn = _rms_rows(x, g_ref[...]).astype(BF16)
    z = _dot(xn, wg_ref[...])
    gate = 1.0 / (1.0 + jnp.exp(-z))
    o_ref[...] = x + gate * _dot(p_ref[...].astype(BF16), wp_ref[...])


def _ple(x, peer, p, g, wg, wp, tm, name):
    t, d = x.shape
    return pl.pallas_call(
        _ple_kernel,
        grid=(t // tm,),
        in_specs=[pl.BlockSpec((tm, d), lambda i: (i, 0)),
                  pl.BlockSpec((tm, d), lambda i: (i, 0)),
                  pl.BlockSpec((tm, PLE_DIM), lambda i: (i, 0)),
                  pl.BlockSpec((1, d), lambda i: (0, 0)),
                  pl.BlockSpec((d, d), lambda i: (0, 0)),
                  pl.BlockSpec((PLE_DIM, d), lambda i: (0, 0))],
        out_specs=pl.BlockSpec((tm, d), lambda i: (i, 0)),
        out_shape=jax.ShapeDtypeStruct((t, d), F32),
        compiler_params=_params("parallel"),
        name=name,
    )(x, peer, p, g, wg, wp)


def _lambda_init(layer_idx):
    return 0.8 - 0.6 * math.exp(-0.3 * layer_idx)


def _token_tail(x, a, c, lw, p_l, tm, tag):
    x1 = _outproj(x, a, c, lw["w_out"], tm, D_MODEL // 2, "outproj_" + tag)
    hq = _norm_mm(x1, lw["ffn_norm_g"], lw["peer_wq"], tm, D_MODEL // 2, "peer_query_" + tag)
    xn, gate = _peer_select(x1, lw["ffn_norm_g"], hq, lw["peer_subkeys"], min(tm, SELECT_TM),
                            "peer_select_" + tag)
    peer = _peer_dense(xn, gate, lw["peer_ut"], lw["peer_v"], tm, "peer_dense_" + tag)
    return _ple(x1, peer, p_l, lw["ple_norm_g"], lw["ple_gate_w"], lw["ple_proj_w"],
                min(tm, PLE_TM), "ple_" + tag)


def kernel(x_prompt, x_sample, cache_k, cache_v, state_conv, page_table, p_prompt, p_sample,
           attn_norm_g, w_in, q_norm_g, k_norm_g, lam_q1, lam_k1, lam_q2, lam_k2, subln_g,
           conv_w, w_out, ffn_norm_g, peer_wq, peer_subkeys, peer_u, peer_v,
           ple_norm_g, ple_gate_w, ple_proj_w):
    batch, seq, d = x_prompt.shape
    dec_b, dec_seq, _ = x_sample.shape
    depth = w_in.shape[0]
    n_pool = cache_k.shape[1]
    past_len = page_table.shape[1] * cache_k.shape[2]
    tp, ts = batch * seq, dec_b * dec_seq

    rope_p = _rope_tables(jnp.arange(seq))
    rope_s = _rope_tables(past_len + (jnp.arange(ts) % dec_seq))
    ck = cache_k.transpose(0, 1, 3, 4, 5, 2).reshape(depth, n_pool, ATTN_WIDTH, PAGE_SIZE)
    cv = cache_v.reshape(depth, n_pool, PAGE_SIZE * N_HEADS, V_DIM)

    yp = x_prompt.reshape(tp, d)
    ys = x_sample.reshape(ts, d)
    outs = [[] for _ in range(6)]
    row = lambda v: v.reshape(1, -1)
    for l in range(depth):
        lam0 = _lambda_init(l)
        lw = {
            "w_out": w_out[l].astype(BF16),
            "ffn_norm_g": row(ffn_norm_g[l]),
            "peer_wq": peer_wq[l].astype(BF16),
            "peer_subkeys": peer_subkeys[l].astype(BF16),
            "peer_ut": peer_u[l].T.astype(BF16),
            "peer_v": peer_v[l].astype(BF16),
            "ple_norm_g": row(ple_norm_g[l]),
            "ple_gate_w": ple_gate_w[l].astype(BF16),
            "ple_proj_w": ple_proj_w[l].astype(BF16),
        }
        w_in_l = w_in[l].astype(BF16)
        g_in = row(attn_norm_g[l])
        qkg = jnp.stack([jnp.tile(q_norm_g[l], 2), jnp.tile(k_norm_g[l], 2)]).reshape(2, 1, LANES)
        lamv = jnp.pad(jnp.stack([lam_q1[l], lam_k1[l], lam_q2[l], lam_k2[l]]),
                       ((0, 0), (0, LANES - HEAD_DIM)))
        sg = row(subln_g[l])

        y = _inproj(yp, g_in, w_in_l, qkg, rope_p, PROMPT_TM, "inproj_prompt")
        a = _prompt_attention(y, lamv, sg, lam0, batch, seq)
        c, cstate = _conv_prompt(y, conv_w[l], batch, seq, PROMPT_TM)
        outs[0].append(y[:, ATTN_WIDTH:2 * ATTN_WIDTH].reshape(batch, seq, N_HEADS, 2, HEAD_DIM))
        outs[1].append(y[:, 2 * ATTN_WIDTH:3 * ATTN_WIDTH].reshape(batch, seq, N_HEADS, V_DIM))
        outs[2].append(cstate)
        yp = _token_tail(yp, a, c, lw, p_prompt[l].reshape(tp, PLE_DIM), PROMPT_TM, "prompt")

        y = _inproj(ys, g_in, w_in_l, qkg, rope_s, ts, "inproj_sample")
        a = _sample_attention(y.reshape(dec_b, dec_seq, IN_COLS), ck, cv, page_table, l, lamv, sg, lam0)
        c, u = _conv_sample(y, conv_w[l], state_conv[l], dec_seq)
        outs[3].append(y[:, ATTN_WIDTH:2 * ATTN_WIDTH].reshape(dec_b, dec_seq, N_HEADS, 2, HEAD_DIM))
        outs[4].append(y[:, 2 * ATTN_WIDTH:3 * ATTN_WIDTH].reshape(dec_b, dec_seq, N_HEADS, V_DIM))
        outs[5].append(u.reshape(dec_b, dec_seq, CONV_WIDTH)[:, dec_seq - (CONV_K - 1):])
        ys = _token_tail(ys, a.reshape(ts, ATTN_WIDTH), c, lw, p_sample[l].reshape(ts, PLE_DIM), ts, "sample")

    return (yp.reshape(batch, seq, d), ys.reshape(dec_b, dec_seq, d),
            jnp.stack(outs[0]), jnp.stack(outs[1]), jnp.stack(outs[2]),
            jnp.stack(outs[3]), jnp.stack(outs[4]), jnp.stack(outs[5]))
```

```python
import functools
import math

import jax
import jax.numpy as jnp
from jax import lax
from jax.experimental import pallas as pl
from jax.experimental.pallas import tpu as pltpu

F32 = jnp.float32
BF16 = jnp.bfloat16

D_MODEL = 2048
N_HEADS = 8
HEAD_DIM = 64
V_DIM = 128
ATTN_WIDTH = 1024
CONV_WIDTH = 1024
IN_COLS = 3 * ATTN_WIDTH + 3 * CONV_WIDTH
ROT_DIM = 16
ROPE_THETA = 500000.0
CONV_K = 3
PLE_DIM = 256
PEER_HEADS = 8
PEER_NKEYS = 128
PEER_N = PEER_NKEYS * PEER_NKEYS
PEER_TOPK = 16
EPS = 1e-6
PAGE_SIZE = 128

LANES = 128
VMEM_LIMIT_BYTES = 56 * 1024 * 1024
NEG_INF = float("-inf")
MASK_VALUE = float(jnp.finfo(jnp.float32).min)
SQRT_HALF = 0.7071067811865476

PROMPT_TM = 512
ATTN_TQ = 512
ATTN_RB = 256
PAGES_PER_STEP = 16
PEER_EC = 1024
SELECT_TM = 256
GATE_GROUP = 16
GATE_PITCH = PEER_NKEYS + 8
PLE_TM = 256


def _params(*sem):
    return pltpu.CompilerParams(dimension_semantics=sem, vmem_limit_bytes=VMEM_LIMIT_BYTES)


def _rms_rows(x, g):
    ms = jnp.mean(x * x, axis=-1, keepdims=True)
    return x * lax.rsqrt(ms + EPS) * g


def _dot(a, b):
    return jnp.dot(a, b, preferred_element_type=F32)


def _dot_nt(a, b):
    return lax.dot_general(a, b, (((1,), (1,)), ((), ())), preferred_element_type=F32)


def _norm_mm_kernel(x_ref, g_ref, w_ref, o_ref, hn_ref):
    @pl.when(pl.program_id(1) == 0)
    def _():
        hn_ref[...] = _rms_rows(x_ref[...], g_ref[...]).astype(BF16)

    o_ref[...] = _dot(hn_ref[...], w_ref[...])


def _inproj_kernel(x_ref, g_ref, w_ref, qkg_ref, ra_ref, rp_ref, rm_ref, o_ref, *rest, emit_kv):
    hn_ref = rest[-1]
    j = pl.program_id(1)

    @pl.when(j == 0)
    def _():
        hn_ref[...] = _rms_rows(x_ref[...], g_ref[...]).astype(BF16)

    y = _dot(hn_ref[...], w_ref[...])

    def norm_rope(emit_t):
        r = lax.broadcasted_iota(jnp.int32, (LANES, LANES), 0) // HEAD_DIM
        c = lax.broadcasted_iota(jnp.int32, (LANES, LANES), 1) // HEAD_DIM
        group_sum = jnp.where(r == c, 1.0, 0.0).astype(BF16)
        for h in range(N_HEADS):
            yh = y[:, h * LANES:(h + 1) * LANES]
            sq = yh * yh
            hi = sq.astype(BF16)
            lo = (sq - hi.astype(F32)).astype(BF16)
            ss = _dot(hi, group_sum) + _dot(lo, group_sum)
            yn = yh * lax.rsqrt(ss * (1.0 / HEAD_DIM) + EPS) * qkg_ref[...]
            out = (yn * ra_ref[...]
                   + pltpu.roll(yn, ROT_DIM // 2, 1) * rp_ref[...]
                   + pltpu.roll(yn, LANES - ROT_DIM // 2, 1) * rm_ref[...])
            o_ref[:, h * LANES:(h + 1) * LANES] = out
            if emit_t:
                rest[0][h * LANES:(h + 1) * LANES, :] = out.T

    if emit_kv:
        pl.when(j == 0)(lambda: norm_rope(False))
        pl.when(j == 1)(lambda: norm_rope(True))
    else:
        pl.when(j < 2)(lambda: norm_rope(False))

    @pl.when(j >= 2)
    def _():
        o_ref[...] = y

    if emit_kv:
        @pl.when(j == 2)
        def _():
            heads = jnp.stack([y[:, h * V_DIM:(h + 1) * V_DIM] for h in range(N_HEADS)], axis=0)
            rest[1][...] = jnp.swapaxes(heads, 0, 1)


def _norm_mm(x, g, w, tm, tn, name):
    t, d = x.shape
    n = w.shape[1]
    return pl.pallas_call(
        _norm_mm_kernel,
        grid=(t // tm, n // tn),
        in_specs=[pl.BlockSpec((tm, d), lambda i, j: (i, 0)),
                  pl.BlockSpec((1, d), lambda i, j: (0, 0)),
                  pl.BlockSpec((d, tn), lambda i, j: (0, j))],
        out_specs=pl.BlockSpec((tm, tn), lambda i, j: (i, j)),
        out_shape=jax.ShapeDtypeStruct((t, n), F32),
        scratch_shapes=[pltpu.VMEM((tm, d), BF16)],
        compiler_params=_params("parallel", "arbitrary"),
        name=name,
    )(x, g, w)


def _inproj(x, g, w, qkg, rope, tm, name, kv_batch=None):
    t, d = x.shape
    tn = ATTN_WIDTH
    ra, rp, rm = rope
    nrep = ra.shape[0] // tm
    rope_spec = pl.BlockSpec((tm, LANES), lambda i, j: (i % nrep, 0))
    out_specs = [pl.BlockSpec((tm, tn), lambda i, j: (i, j))]
    out_shape = [jax.ShapeDtypeStruct((t, IN_COLS), F32)]
    if kv_batch is not None:
        batch, seq = kv_batch
        tps = seq // tm
        out_specs += [pl.BlockSpec((None, ATTN_WIDTH, tm), lambda i, j: (i // tps, 0, i % tps)),
                      pl.BlockSpec((tm, N_HEADS, V_DIM), lambda i, j: (i, 0, 0))]
        out_shape += [jax.ShapeDtypeStruct((batch, ATTN_WIDTH, seq), F32),
                      jax.ShapeDtypeStruct((t, N_HEADS, V_DIM), F32)]
    res = pl.pallas_call(
        functools.partial(_inproj_kernel, emit_kv=kv_batch is not None),
        grid=(t // tm, IN_COLS // tn),
        in_specs=[pl.BlockSpec((tm, d), lambda i, j: (i, 0)),
                  pl.BlockSpec((1, d), lambda i, j: (0, 0)),
                  pl.BlockSpec((d, tn), lambda i, j: (0, j)),
                  pl.BlockSpec((None, 1, LANES), lambda i, j: (jnp.minimum(j, 1), 0, 0)),
                  rope_spec, rope_spec, rope_spec],
        out_specs=out_specs,
        out_shape=out_shape,
        scratch_shapes=[pltpu.VMEM((tm, d), BF16)],
        compiler_params=_params("parallel", "arbitrary"),
        name=name,
    )(x, g, w, qkg, ra, rp, rm)
    return res if kv_batch is not None else res[0]


def _rope_tables(pos):
    inv_freq = ROPE_THETA ** (-jnp.arange(0, ROT_DIM, 2, dtype=F32) / ROT_DIM)
    ang = pos.astype(F32)[:, None] * inv_freq[None, :]
    cos, sin = jnp.cos(ang), jnp.sin(ang)
    p = pos.shape[0]
    half = ROT_DIM // 2
    one = jnp.ones((p, HEAD_DIM - ROT_DIM), F32)
    zero = jnp.zeros((p, HEAD_DIM - ROT_DIM), F32)
    zh = jnp.zeros((p, half), F32)
    a = jnp.concatenate([cos, cos, one], axis=-1)
    bp = jnp.concatenate([zh, sin, zero], axis=-1)
    bm = jnp.concatenate([-sin, zh, zero], axis=-1)
    return tuple(jnp.concatenate([t, t], axis=-1) for t in (a, bp, bm))


def _lambda_full(lamv_ref, lam0):
    lv = lamv_ref[...]
    s1 = jnp.sum(lv[0:1] * lv[1:2], axis=-1, keepdims=True)
    s2 = jnp.sum(lv[2:3] * lv[3:4], axis=-1, keepdims=True)
    return jnp.exp(s1) - jnp.exp(s2) + lam0


def _pattn_kernel(q_ref, k_ref, v_ref, lamv_ref, sg_ref, o_ref, kb_ref, vb_ref, m_ref, acc_ref,
                  *, lam0, tq):
    qi = pl.program_id(2)
    seq = k_ref.shape[0]

    @pl.when(qi == 0)
    def _():
        kb_ref[...] = k_ref[...].astype(BF16)
        vb_ref[:, :V_DIM] = v_ref[...].astype(BF16)
        lane = lax.broadcasted_iota(jnp.int32, (seq, LANES), 1)
        vb_ref[:, V_DIM:] = jnp.where(lane == 0, 1.0, 0.0).astype(BF16)

    q = q_ref[...] * (HEAD_DIM ** -0.5)
    lane = lax.broadcasted_iota(jnp.int32, (tq, LANES), 1)
    q0 = jnp.where(lane < HEAD_DIM, q, 0.0).astype(BF16)
    q1 = jnp.where(lane >= HEAD_DIM, q, 0.0).astype(BF16)
    qq = jnp.concatenate([q0, q1], axis=0)

    m_ref[...] = jnp.full(m_ref.shape, NEG_INF, F32)
    acc_ref[...] = jnp.zeros(acc_ref.shape, F32)

    def chunk(kj, masked):
        off = pl.multiple_of(kj * tq, tq)
        ks = kb_ref[pl.ds(off, tq), :]
        vs = vb_ref[pl.ds(off, tq), :]
        for rb in range(2 * tq // ATTN_RB):
            rs = slice(rb * ATTN_RB, (rb + 1) * ATTN_RB)
            s = _dot_nt(qq[rs], ks)
            if masked:
                row = lax.broadcasted_iota(jnp.int32, s.shape, 0) + rb * ATTN_RB
                col = lax.broadcasted_iota(jnp.int32, s.shape, 1)
                s = jnp.where(col <= (row & (tq - 1)), s, MASK_VALUE)
            m_old = m_ref[rs, :]
            m_new = jnp.maximum(m_old, jnp.max(s, axis=-1, keepdims=True))
            alpha = jnp.exp(m_old - m_new)
            p = jnp.exp(s - m_new).astype(BF16)
            acc_ref[rs, :] = alpha * acc_ref[rs, :] + _dot(p, vs)
            m_ref[rs, :] = m_new

    def body(kj, carry):
        chunk(kj, False)
        return carry

    lax.fori_loop(0, qi, body, 0)
    chunk(qi, True)

    acc = acc_ref[...]
    o = acc[:, :V_DIM] / acc[:, V_DIM:V_DIM + 1]
    lam = _lambda_full(lamv_ref, lam0)
    a = o[:tq] - lam * o[tq:]
    o_ref[...] = (_rms_rows(a, sg_ref[...]) * (1.0 - lam0)).astype(o_ref.dtype)


def _prompt_attention(y, lamv, sg, lam0, batch, seq):
    tq = ATTN_TQ
    nq = seq // tq
    kern = functools.partial(_pattn_kernel, lam0=lam0, tq=tq)
    return pl.pallas_call(
        kern,
        grid=(batch, N_HEADS, nq),
        in_specs=[pl.BlockSpec((tq, LANES), lambda b, h, i: (b * nq + i, h)),
                  pl.BlockSpec((seq, LANES), lambda b, h, i: (b, N_HEADS + h)),
                  pl.BlockSpec((seq, LANES), lambda b, h, i: (b, 2 * N_HEADS + h)),
                  pl.BlockSpec((4, LANES), lambda b, h, i: (0, 0)),
                  pl.BlockSpec((1, LANES), lambda b, h, i: (0, 0))],
        out_specs=pl.BlockSpec((tq, LANES), lambda b, h, i: (b * nq + i, h)),
        out_shape=jax.ShapeDtypeStruct((batch * seq, ATTN_WIDTH), BF16),
        scratch_shapes=[pltpu.VMEM((seq, LANES), BF16),
                        pltpu.VMEM((seq, 2 * LANES), BF16),
                        pltpu.VMEM((2 * tq, 1), F32),
                        pltpu.VMEM((2 * tq, 2 * LANES), F32)],
        compiler_params=_params("parallel", "parallel", "arbitrary"),
        name="prompt_attention",
    )(y, y, y, lamv, sg)


def _sattn_kernel(pt_ref, q_ref, kn_ref, vn_ref, lamv_ref, sg_ref, *rest, lam0, npg):
    k_refs = rest[:npg]
    v_refs = rest[npg:2 * npg]
    o_ref, qt_ref, knp_ref, vnp_ref, m_ref, l_ref, acc_ref = rest[2 * npg:]
    g = pl.program_id(1)
    dec_seq = q_ref.shape[0]
    rows = 2 * dec_seq
    nrow = N_HEADS * rows

    @pl.when(g == 0)
    def _():
        q = q_ref[...] * (HEAD_DIM ** -0.5)
        row = lax.broadcasted_iota(jnp.int32, (nrow, ATTN_WIDTH), 0)
        lane = lax.broadcasted_iota(jnp.int32, (nrow, ATTN_WIDTH), 1)
        qq = jnp.zeros((nrow, ATTN_WIDTH), F32)
        for i in range(dec_seq):
            qq = jnp.where((row & (dec_seq - 1)) == i, q[i:i + 1, :], qq)
        qt_ref[...] = jnp.where(lane // HEAD_DIM == row // dec_seq, qq, 0.0).astype(BF16)
        knp_ref[...] = jnp.zeros(knp_ref.shape, F32)
        vnp_ref[...] = jnp.zeros(vnp_ref.shape, F32)
        m_ref[...] = jnp.full(m_ref.shape, NEG_INF, F32)
        l_ref[...] = jnp.zeros(l_ref.shape, F32)
        acc_ref[...] = jnp.zeros(acc_ref.shape, F32)

    def attend(st, weighted_values):
        m_old = m_ref[...]
        m_new = jnp.maximum(m_old, jnp.max(st, axis=-1, keepdims=True))
        alpha = jnp.exp(m_old - m_new)
        p = jnp.exp(st - m_new)
        l_ref[...] = alpha * l_ref[...] + jnp.sum(p, axis=-1, keepdims=True)
        acc_ref[...] = alpha * acc_ref[...] + weighted_values(p)
        m_ref[...] = m_new

    kc = jnp.concatenate([r[...].astype(BF16) for r in k_refs], axis=1)
    st = _dot(qt_ref[...], kc)

    def page_values(p):
        outs = []
        for h in range(N_HEADS):
            vh = jnp.concatenate(
                [r[pl.ds(h, PAGE_SIZE, stride=N_HEADS), :].astype(BF16) for r in v_refs], axis=0)
            outs.append(_dot(p[h * rows:(h + 1) * rows, :].astype(BF16), vh))
        return jnp.concatenate(outs, axis=0)

    attend(st, page_values)

    @pl.when(g == pl.num_programs(1) - 1)
    def _():
        knp_ref[0:dec_seq, :] = kn_ref[...]
        vnp_ref[0:dec_seq, :] = vn_ref[...]
        sn = _dot_nt(qt_ref[...], knp_ref[...].astype(BF16))
        row = lax.broadcasted_iota(jnp.int32, sn.shape, 0)
        col = lax.broadcasted_iota(jnp.int32, sn.shape, 1)
        sn = jnp.where(col <= (row & (dec_seq - 1)), sn, MASK_VALUE)

        def new_values(p):
            o = _dot(p.astype(BF16), vnp_ref[...].astype(BF16))
            return jnp.concatenate(
                [o[h * rows:(h + 1) * rows, h * V_DIM:(h + 1) * V_DIM] for h in range(N_HEADS)], axis=0)

        attend(sn, new_values)

        o = acc_ref[...] / l_ref[...]
        lam = _lambda_full(lamv_ref, lam0)
        a = o - lam * pltpu.roll(o, nrow - dec_seq, 0)
        a = _rms_rows(a, sg_ref[...]) * (1.0 - lam0)
        for h in range(N_HEADS):
            o_ref[:, h * V_DIM:(h + 1) * V_DIM] = a[h * rows:h * rows + dec_seq, :]


def _sample_attention(y3, cache_kt, cache_vr, page_table, layer, lamv, sg, lam0):
    dec_b, dec_seq, _ = y3.shape
    n_pages = page_table.shape[1]
    npg = PAGES_PER_STEP
    assert n_pages % npg == 0 and dec_seq & (dec_seq - 1) == 0
    nrow = N_HEADS * 2 * dec_seq
    pt = page_table.reshape(-1)

    def page_spec(p):
        return pl.BlockSpec((None, None, ATTN_WIDTH, PAGE_SIZE),
                            lambda b, g, pt_ref: (layer, pt_ref[b * n_pages + g * npg + p], 0, 0))

    def col_spec(c):
        return pl.BlockSpec((None, dec_seq, ATTN_WIDTH), lambda b, g, pt_ref: (b, 0, c))

    grid_spec = pltpu.PrefetchScalarGridSpec(
        num_scalar_prefetch=1,
        grid=(dec_b, n_pages // npg),
        in_specs=[col_spec(0), col_spec(1), col_spec(2),
                  pl.BlockSpec((4, LANES), lambda b, g, pt_ref: (0, 0)),
                  pl.BlockSpec((1, LANES), lambda b, g, pt_ref: (0, 0))]
                 + [page_spec(p) for p in range(npg)] + [page_spec(p) for p in range(npg)],
        out_specs=pl.BlockSpec((None, dec_seq, ATTN_WIDTH), lambda b, g, pt_ref: (b, 0, 0)),
        scratch_shapes=[pltpu.VMEM((nrow, ATTN_WIDTH), BF16),
                        pltpu.VMEM((PAGE_SIZE, ATTN_WIDTH), F32),
                        pltpu.VMEM((PAGE_SIZE, ATTN_WIDTH), F32),
                        pltpu.VMEM((nrow, 1), F32),
                        pltpu.VMEM((nrow, 1), F32),
                        pltpu.VMEM((nrow, V_DIM), F32)],
    )
    kern = functools.partial(_sattn_kernel, lam0=lam0, npg=npg)
    return pl.pallas_call(
        kern,
        grid_spec=grid_spec,
        out_shape=jax.ShapeDtypeStruct((dec_b, dec_seq, ATTN_WIDTH), F32),
        compiler_params=_params("parallel", "arbitrary"),
        name="sample_attention",
    )(pt, y3, y3, y3, lamv, sg, *([cache_kt] * npg), *([cache_vr] * npg))


def _conv_prompt_kernel(gb_ref, gc_ref, xc_ref, hgc_ref, hxc_ref, w_ref, c_ref, st_ref, *, tiles_per_seq):
    i = pl.program_id(0)
    u = gc_ref[...] * xc_ref[...]
    tm = u.shape[0]
    uh = hgc_ref[...] * hxc_ref[...]
    uh = jnp.where(i % tiles_per_seq == 0, 0.0, uh)
    row = lax.broadcasted_iota(jnp.int32, u.shape, 0)
    u1 = jnp.where(row == 0, uh[7:8], pltpu.roll(u, 1, 0))
    u2 = jnp.where(row == 0, uh[6:7], jnp.where(row == 1, uh[7:8], pltpu.roll(u, 2, 0)))
    w = w_ref[...]
    conv = w[0:1] * u2 + w[1:2] * u1 + w[2:3] * u
    c_ref[...] = (gb_ref[...] * conv).astype(c_ref.dtype)
    st_ref[...] = u[tm - (CONV_K - 1):tm]


def _conv_prompt(y, conv_w, batch, seq, tm):
    t = y.shape[0]
    tps = seq // tm
    hb = tm // 8
    kern = functools.partial(_conv_prompt_kernel, tiles_per_seq=tps)
    wide = lambda c: pl.BlockSpec((tm, CONV_WIDTH), lambda i: (i, c))
    halo = lambda c: pl.BlockSpec((8, CONV_WIDTH), lambda i: (jnp.maximum(i * hb - 1, 0), c))
    return pl.pallas_call(
        kern,
        grid=(t // tm,),
        in_specs=[wide(3), wide(4), wide(5), halo(4), halo(5),
                  pl.BlockSpec((CONV_K, CONV_WIDTH), lambda i: (0, 0))],
        out_specs=[pl.BlockSpec((tm, CONV_WIDTH), lambda i: (i, 0)),
                   pl.BlockSpec((None, CONV_K - 1, CONV_WIDTH), lambda i: (i // tps, 0, 0))],
        out_shape=[jax.ShapeDtypeStruct((t, CONV_WIDTH), BF16),
                   jax.ShapeDtypeStruct((batch, CONV_K - 1, CONV_WIDTH), F32)],
        compiler_params=_params("arbitrary"),
        name="conv_prompt",
    )(y, y, y, y, y, conv_w)


def _conv_sample_kernel(gb_ref, gc_ref, xc_ref, e1_ref, e2_ref, w_ref, c_ref, u_ref, *, dec_seq):
    u = gc_ref[...] * xc_ref[...]
    pos = lax.broadcasted_iota(jnp.int32, u.shape, 0) & (dec_seq - 1)
    u1 = jnp.where(pos == 0, e1_ref[...], pltpu.roll(u, 1, 0))
    u2 = jnp.where(pos <= 1, e2_ref[...], pltpu.roll(u, 2, 0))
    w = w_ref[...]
    conv = w[0:1] * u2 + w[1:2] * u1 + w[2:3] * u
    c_ref[...] = (gb_ref[...] * conv).astype(c_ref.dtype)
    u_ref[...] = u


def _conv_sample(y, conv_w, prev, dec_seq):
    t = y.shape[0]
    dec_b = t // dec_seq
    zeros = jnp.zeros((dec_b, dec_seq - 1, CONV_WIDTH), F32)
    e1 = jnp.concatenate([prev[:, 1:2], zeros], axis=1).reshape(t, CONV_WIDTH)
    e2 = jnp.concatenate([prev, zeros[:, 1:]], axis=1).reshape(t, CONV_WIDTH)
    kern = functools.partial(_conv_sample_kernel, dec_seq=dec_seq)
    wide = lambda c: pl.BlockSpec((t, CONV_WIDTH), lambda i: (0, c))
    full = pl.BlockSpec((t, CONV_WIDTH), lambda i: (0, 0))
    return pl.pallas_call(
        kern,
        grid=(1,),
        in_specs=[wide(3), wide(4), wide(5), full, full,
                  pl.BlockSpec((CONV_K, CONV_WIDTH), lambda i: (0, 0))],
        out_specs=[full, full],
        out_shape=[jax.ShapeDtypeStruct((t, CONV_WIDTH), BF16),
                   jax.ShapeDtypeStruct((t, CONV_WIDTH), F32)],
        compiler_params=_params("arbitrary"),
        name="conv_sample",
    )(y, y, y, e1, e2, conv_w)


def _outproj_kernel(x_ref, a_ref, c_ref, wa_ref, wc_ref, o_ref):
    o_ref[...] = (x_ref[...] + _dot(a_ref[...].astype(BF16), wa_ref[...])
                  + _dot(c_ref[...].astype(BF16), wc_ref[...]))


def _outproj(x, a, c, w, tm, tn, name):
    t, d = x.shape
    return pl.pallas_call(
        _outproj_kernel,
        grid=(t // tm, d // tn),
        in_specs=[pl.BlockSpec((tm, tn), lambda i, j: (i, j)),
                  pl.BlockSpec((tm, ATTN_WIDTH), lambda i, j: (i, 0)),
                  pl.BlockSpec((tm, CONV_WIDTH), lambda i, j: (i, 0)),
                  pl.BlockSpec((ATTN_WIDTH, tn), lambda i, j: (0, j)),
                  pl.BlockSpec((CONV_WIDTH, tn), lambda i, j: (1, j))],
        out_specs=pl.BlockSpec((tm, tn), lambda i, j: (i, j)),
        out_shape=jax.ShapeDtypeStruct((t, d), F32),
        compiler_params=_params("parallel", "arbitrary"),
        name=name,
    )(x, a, c, w, w)


def _top_with_index(s, k):
    nrows = s.shape[0]
    riota = lax.broadcasted_iota(jnp.int32, s.shape, 0).astype(F32)
    vals, idxs = [], []
    for _ in range(k):
        m = jnp.max(s, axis=0, keepdims=True)
        i = jnp.min(jnp.where(s == m, riota, float(nrows)), axis=0, keepdims=True)
        vals.append(m)
        idxs.append(i)
        s = jnp.where(riota == i, NEG_INF, s)
    return vals, idxs


def _stack_rows(rows, nrows, fill):
    r = lax.broadcasted_iota(jnp.int32, (nrows, rows[0].shape[1]), 0)
    out = jnp.full(r.shape, fill, F32)
    for j, row in enumerate(rows):
        out = jnp.where(r == j, row, out)
    return out


def _candidates(t1, i1, t2, i2):
    k = PEER_TOPK
    t2_all = _stack_rows(t2, k, NEG_INF)
    i2_all = _stack_rows(i2, k, 0.0)
    row8 = lax.broadcasted_iota(jnp.int32, (8, t2_all.shape[1]), 0)
    vals, codes, tail_v, tail_c = [], [], [], []
    for j1 in range(k):
        cnt = k // (j1 + 1)
        base = i1[j1] * float(PEER_NKEYS)
        if cnt > 8:
            vals.append(t1[j1] + t2_all)
            codes.append(base + i2_all)
        elif cnt >= 4:
            v = t1[j1] + t2_all[:8]
            vals.append(v if cnt == 8 else jnp.where(row8 < cnt, v, NEG_INF))
            codes.append(base + i2_all[:8])
        else:
            tail_v += [t1[j1] + t2[j2] for j2 in range(cnt)]
            tail_c += [base + i2[j2] for j2 in range(cnt)]
    tail_rows = -(-len(tail_v) // 8) * 8
    vals.append(_stack_rows(tail_v, tail_rows, NEG_INF))
    codes.append(_stack_rows(tail_c, tail_rows, 0.0))
    return jnp.concatenate(vals, axis=0), jnp.concatenate(codes, axis=0)


def _peer_select_kernel(x_ref, g_ref, hq_ref, sk_ref, xn_ref, gate_ref, at_ref, bt_ref, gt_ref,
                        atm_ref, btm_ref, gtm_ref, m_ref):
    tm = x_ref.shape[0]
    xn_ref[...] = _rms_rows(x_ref[...], g_ref[...]).astype(BF16)

    for h in range(PEER_HEADS):
        c0 = h * 2 * PEER_NKEYS
        q1 = hq_ref[:, c0:c0 + PEER_NKEYS].astype(BF16)
        q2 = hq_ref[:, c0 + PEER_NKEYS:c0 + 2 * PEER_NKEYS].astype(BF16)
        s1 = _dot_nt(sk_ref[h, 0], q1)
        s2 = _dot_nt(sk_ref[h, 1], q2)
        t1, i1 = _top_with_index(s1, PEER_TOPK)
        t2, i2 = _top_with_index(s2, PEER_TOPK)
        cand, code = _candidates(t1, i1, t2, i2)
        sc, rows = _top_with_index(cand, PEER_TOPK)
        riota = lax.broadcasted_iota(jnp.int32, cand.shape, 0).astype(F32)
        ex = [jnp.exp(c - sc[0]) for c in sc]
        z = ex[0]
        for v in ex[1:]:
            z = z + v
        for j in range(PEER_TOPK):
            picked = jnp.sum(jnp.where(riota == rows[j], code, 0.0), axis=0, keepdims=True)
            n1 = jnp.floor(picked * (1.0 / PEER_NKEYS))
            slot = h * PEER_TOPK + j
            at_ref[slot:slot + 1, :] = n1
            bt_ref[slot:slot + 1, :] = picked - n1 * float(PEER_NKEYS)
            gt_ref[slot:slot + 1, :] = ex[j] / z

    atm_ref[...] = at_ref[...].T
    btm_ref[...] = bt_ref[...].T
    gtm_ref[...] = gt_ref[...].T

    niota = lax.broadcasted_iota(jnp.int32, (PEER_NKEYS, LANES), 0).astype(F32)

    def group(gi, carry):
        t0 = pl.multiple_of(gi * GATE_GROUP, GATE_GROUP)
        for tt in range(GATE_GROUP):
            a_row = atm_ref[pl.ds(t0 + tt, 1), :]
            b_row = btm_ref[pl.ds(t0 + tt, 1), :]
            g_row = gtm_ref[pl.ds(t0 + tt, 1), :]
            lhs = jnp.where(niota == a_row, g_row, 0.0).astype(BF16)
            rhs = jnp.where(niota == b_row, 1.0, 0.0).astype(BF16)
            m_ref[tt * GATE_PITCH:tt * GATE_PITCH + PEER_NKEYS, :] = _dot_nt(lhs, rhs)
        for n1 in range(PEER_NKEYS):
            rows_n1 = m_ref[pl.ds(n1, GATE_GROUP, stride=GATE_PITCH), :]
            gate_ref[pl.ds(t0, GATE_GROUP), n1 * PEER_NKEYS:(n1 + 1) * PEER_NKEYS] = rows_n1.astype(BF16)
        return carry

    lax.fori_loop(0, tm // GATE_GROUP, group, 0)


def _peer_select(x, g, hq, subkeys, tm, name):
    t, d = x.shape
    nk = PEER_NKEYS
    slots = PEER_HEADS * PEER_TOPK
    return pl.pallas_call(
        _peer_select_kernel,
        grid=(t // tm,),
        in_specs=[pl.BlockSpec((tm, d), lambda i: (i, 0)),
                  pl.BlockSpec((1, d), lambda i: (0, 0)),
                  pl.BlockSpec((tm, PEER_HEADS * 2 * nk), lambda i: (i, 0)),
                  pl.BlockSpec((PEER_HEADS, 2, nk, nk), lambda i: (0, 0, 0, 0))],
        out_specs=[pl.BlockSpec((tm, d), lambda i: (i, 0)),
                   pl.BlockSpec((tm, PEER_N), lambda i: (i, 0))],
        out_shape=[jax.ShapeDtypeStruct((t, d), BF16),
                   jax.ShapeDtypeStruct((t, PEER_N), BF16)],
        scratch_shapes=[pltpu.VMEM((slots, tm), F32)] * 3 + [pltpu.VMEM((tm, slots), F32)] * 3
                       + [pltpu.VMEM((GATE_GROUP * GATE_PITCH, nk), F32)],
        compiler_params=_params("parallel"),
        name=name,
    )(x, g, hq, subkeys)


def _peer_dense_kernel(xn_ref, gate_ref, ut_ref, v_ref, o_ref):
    e = pl.program_id(1)
    a = _dot(xn_ref[...], ut_ref[...])
    act = 0.5 * a * (1.0 + lax.erf(a * SQRT_HALF))
    weighted = (gate_ref[...].astype(F32) * act).astype(BF16)

    @pl.when(e == 0)
    def _():
        o_ref[...] = jnp.zeros(o_ref.shape, F32)

    o_ref[...] += _dot(weighted, v_ref[...])


def _peer_dense(xn, gate, ut, v, tm, name):
    t, d = xn.shape
    ec = PEER_EC
    return pl.pallas_call(
        _peer_dense_kernel,
        grid=(t // tm, PEER_N // ec),
        in_specs=[pl.BlockSpec((tm, d), lambda i, e: (i, 0)),
                  pl.BlockSpec((tm, ec), lambda i, e: (i, e)),
                  pl.BlockSpec((d, ec), lambda i, e: (0, e)),
                  pl.BlockSpec((ec, d), lambda i, e: (e, 0))],
        out_specs=pl.BlockSpec((tm, d), lambda i, e: (i, 0)),
        out_shape=jax.ShapeDtypeStruct((t, d), F32),
        compiler_params=_params("parallel", "arbitrary"),
        name=name,
    )(xn, gate, ut, v)


def _ple_kernel(x_ref, peer_ref, p_ref, g_ref, wg_ref, wp_ref, o_ref):
    x = x_ref[...] + peer_ref[...]
    xn = _rms_rows(x, g_ref[...]).astype(BF16)
    z = _dot(xn, wg_ref[...])
    gate = 1.0 / (1.0 + jnp.exp(-z))
    o_ref[...] = x + gate * _dot(p_ref[...].astype(BF16), wp_ref[...])


def _ple(x, peer, p, g, wg, wp, tm, name):
    t, d = x.shape
    return pl.pallas_call(
        _ple_kernel,
        grid=(t // tm,),
        in_specs=[pl.BlockSpec((tm, d), lambda i: (i, 0)),
                  pl.BlockSpec((tm, d), lambda i: (i, 0)),
                  pl.BlockSpec((tm, PLE_DIM), lambda i: (i, 0)),
                  pl.BlockSpec((1, d), lambda i: (0, 0)),
                  pl.BlockSpec((d, d), lambda i: (0, 0)),
                  pl.BlockSpec((PLE_DIM, d), lambda i: (0, 0))],
        out_specs=pl.BlockSpec((tm, d), lambda i: (i, 0)),
        out_shape=jax.ShapeDtypeStruct((t, d), F32),
        compiler_params=_params("parallel"),
        name=name,
    )(x, peer, p, g, wg, wp)


def _lambda_init(layer_idx):
    return 0.8 - 0.6 * math.exp(-0.3 * layer_idx)


def _token_tail(x, a, c, lw, p_l, tm, tag):
    x1 = _outproj(x, a, c, lw["w_out"], tm, D_MODEL // 2, "outproj_" + tag)
    hq = _norm_mm(x1, lw["ffn_norm_g"], lw["peer_wq"], tm, D_MODEL // 2, "peer_query_" + tag)
    xn, gate = _peer_select(x1, lw["ffn_norm_g"], hq, lw["peer_subkeys"], min(tm, SELECT_TM),
                            "peer_select_" + tag)
    peer = _peer_dense(xn, gate, lw["peer_ut"], lw["peer_v"], tm, "peer_dense_" + tag)
    return _ple(x1, peer, p_l, lw["ple_norm_g"], lw["ple_gate_w"], lw["ple_proj_w"],
                min(tm, PLE_TM), "ple_" + tag)


def kernel(x_prompt, x_sample, cache_k, cache_v, state_conv, page_table, p_prompt, p_sample,
           attn_norm_g, w_in, q_norm_g, k_norm_g, lam_q1, lam_k1, lam_q2, lam_k2, subln_g,
           conv_w, w_out, ffn_norm_g, peer_wq, peer_subkeys, peer_u, peer_v,
           ple_norm_g, ple_gate_w, ple_proj_w):
    batch, seq, d = x_prompt.shape
    dec_b, dec_seq, _ = x_sample.shape
    depth = w_in.shape[0]
    n_pool = cache_k.shape[1]
    past_len = page_table.shape[1] * cache_k.shape[2]
    tp, ts = batch * seq, dec_b * dec_seq

    rope_p = _rope_tables(jnp.arange(seq))
    rope_s = _rope_tables(past_len + (jnp.arange(ts) % dec_seq))
    ck = cache_k.transpose(0, 1, 3, 4, 5, 2).reshape(depth, n_pool, ATTN_WIDTH, PAGE_SIZE)
    cv = cache_v.reshape(depth, n_pool, PAGE_SIZE * N_HEADS, V_DIM)

    yp = x_prompt.reshape(tp, d)
    ys = x_sample.reshape(ts, d)
    outs = [[] for _ in range(6)]
    row = lambda v: v.reshape(1, -1)
    for l in range(depth):
        lam0 = _lambda_init(l)
        lw = {
            "w_out": w_out[l].astype(BF16),
            "ffn_norm_g": row(ffn_norm_g[l]),
            "peer_wq": peer_wq[l].astype(BF16),
            "peer_subkeys": peer_subkeys[l].astype(BF16),
            "peer_ut": peer_u[l].T.astype(BF16),
            "peer_v": peer_v[l].astype(BF16),
            "ple_norm_g": row(ple_norm_g[l]),
            "ple_gate_w": ple_gate_w[l].astype(BF16),
            "ple_proj_w": ple_proj_w[l].astype(BF16),
        }
        w_in_l = w_in[l].astype(BF16)
        g_in = row(attn_norm_g[l])
        qkg = jnp.stack([jnp.tile(q_norm_g[l], 2), jnp.tile(k_norm_g[l], 2)]).reshape(2, 1, LANES)
        lamv = jnp.pad(jnp.stack([lam_q1[l], lam_k1[l], lam_q2[l], lam_k2[l]]),
                       ((0, 0), (0, LANES - HEAD_DIM)))
        sg = row(subln_g[l])

        y, k_t, v_t = _inproj(yp, g_in, w_in_l, qkg, rope_p, PROMPT_TM, "inproj_prompt", kv_batch=(batch, seq))
        a = _prompt_attention(y, lamv, sg, lam0, batch, seq)
        c, cstate = _conv_prompt(y, conv_w[l], batch, seq, PROMPT_TM)
        outs[0].append(k_t.reshape(batch, N_HEADS, 2, HEAD_DIM, seq).transpose(0, 4, 1, 2, 3))
        outs[1].append(v_t.reshape(batch, seq, N_HEADS, V_DIM))
        outs[2].append(cstate)
        yp = _token_tail(yp, a, c, lw, p_prompt[l].reshape(tp, PLE_DIM), PROMPT_TM, "prompt")

        y = _inproj(ys, g_in, w_in_l, qkg, rope_s, ts, "inproj_sample")
        a = _sample_attention(y.reshape(dec_b, dec_seq, IN_COLS), ck, cv, page_table, l, lamv, sg, lam0)
        c, u = _conv_sample(y, conv_w[l], state_conv[l], dec_seq)
        outs[3].append(y[:, ATTN_WIDTH:2 * ATTN_WIDTH].reshape(dec_b, dec_seq, N_HEADS, 2, HEAD_DIM))
        outs[4].append(y[:, 2 * ATTN_WIDTH:3 * ATTN_WIDTH].reshape(dec_b, dec_seq, N_HEADS, V_DIM))
        outs[5].append(u.reshape(dec_b, dec_seq, CONV_WIDTH)[:, dec_seq - (CONV_K - 1):])
        ys = _token_tail(ys, a.reshape(ts, ATTN_WIDTH), c, lw, p_sample[l].reshape(ts, PLE_DIM), ts, "sample")

    return (yp.reshape(batch, seq, d), ys.reshape(dec_b, dec_seq, d),
            jnp.stack(outs[0]), jnp.stack(outs[1]), jnp.stack(outs[2]),
            jnp.stack(outs[3]), jnp.stack(outs[4]), jnp.stack(outs[5]))
```

```python
import functools
import math

import jax
import jax.numpy as jnp
from jax import lax
from jax.experimental import pallas as pl
from jax.experimental.pallas import tpu as pltpu

F32 = jnp.float32
BF16 = jnp.bfloat16

D_MODEL = 2048
N_HEADS = 8
HEAD_DIM = 64
V_DIM = 128
ATTN_WIDTH = 1024
CONV_WIDTH = 1024
IN_COLS = 3 * ATTN_WIDTH + 3 * CONV_WIDTH
ROT_DIM = 16
ROPE_THETA = 500000.0
CONV_K = 3
PLE_DIM = 256
PEER_HEADS = 8
PEER_NKEYS = 128
PEER_N = PEER_NKEYS * PEER_NKEYS
PEER_TOPK = 16
EPS = 1e-6
PAGE_SIZE = 128

LANES = 128
VMEM_LIMIT_BYTES = 56 * 1024 * 1024
NEG_INF = float("-inf")
MASK_VALUE = float(jnp.finfo(jnp.float32).min)
SQRT_HALF = 0.7071067811865476

PROMPT_TM = 512
ATTN_TQ = 512
ATTN_RB = 256
PAGES_PER_STEP = 16
PEER_EC = 1024
SELECT_TM = 256
GATE_GROUP = 16
GATE_PITCH = PEER_NKEYS + 8
PLE_TM = 256


def _params(*sem):
    return pltpu.CompilerParams(dimension_semantics=sem, vmem_limit_bytes=VMEM_LIMIT_BYTES)


def _rms_rows(x, g):
    ms = jnp.mean(x * x, axis=-1, keepdims=True)
    return x * lax.rsqrt(ms + EPS) * g


def _dot(a, b):
    return jnp.dot(a, b, preferred_element_type=F32)


def _dot_nt(a, b):
    return lax.dot_general(a, b, (((1,), (1,)), ((), ())), preferred_element_type=F32)


def _norm_mm_kernel(x_ref, g_ref, w_ref, o_ref, hn_ref):
    @pl.when(pl.program_id(1) == 0)
    def _():
        hn_ref[...] = _rms_rows(x_ref[...], g_ref[...]).astype(BF16)

    o_ref[...] = _dot(hn_ref[...], w_ref[...])


def _inproj_kernel(x_ref, g_ref, w_ref, qkg_ref, ra_ref, rp_ref, rm_ref, o_ref, *rest, emit_kv):
    hn_ref = rest[-1]
    j = pl.program_id(1)

    @pl.when(j == 0)
    def _():
        hn_ref[...] = _rms_rows(x_ref[...], g_ref[...]).astype(BF16)

    y = _dot(hn_ref[...], w_ref[...])

    def norm_rope(emit_t):
        r = lax.broadcasted_iota(jnp.int32, (LANES, LANES), 0) // HEAD_DIM
        c = lax.broadcasted_iota(jnp.int32, (LANES, LANES), 1) // HEAD_DIM
        group_sum = jnp.where(r == c, 1.0, 0.0).astype(BF16)
        for h in range(N_HEADS):
            yh = y[:, h * LANES:(h + 1) * LANES]
            sq = yh * yh
            hi = sq.astype(BF16)
            lo = (sq - hi.astype(F32)).astype(BF16)
            ss = _dot(hi, group_sum) + _dot(lo, group_sum)
            yn = yh * lax.rsqrt(ss * (1.0 / HEAD_DIM) + EPS) * qkg_ref[...]
            out = (yn * ra_ref[...]
                   + pltpu.roll(yn, ROT_DIM // 2, 1) * rp_ref[...]
                   + pltpu.roll(yn, LANES - ROT_DIM // 2, 1) * rm_ref[...])
            o_ref[:, h * LANES:(h + 1) * LANES] = out
            if emit_t:
                rest[0][h * LANES:(h + 1) * LANES, :] = out.T

    if emit_kv:
        pl.when(j == 0)(lambda: norm_rope(False))
        pl.when(j == 1)(lambda: norm_rope(True))
    else:
        pl.when(j < 2)(lambda: norm_rope(False))

    @pl.when(j >= 2)
    def _():
        o_ref[...] = y

    if emit_kv:
        @pl.when(j == 2)
        def _():
            heads = jnp.stack([y[:, h * V_DIM:(h + 1) * V_DIM] for h in range(N_HEADS)], axis=0)
            rest[1][...] = jnp.swapaxes(heads, 0, 1)


def _norm_mm(x, g, w, layer, tm, tn, name):
    t, d = x.shape
    n = w.shape[2]
    return pl.pallas_call(
        _norm_mm_kernel,
        grid=(t // tm, n // tn),
        in_specs=[pl.BlockSpec((tm, d), lambda i, j: (i, 0)),
                  pl.BlockSpec((1, d), lambda i, j: (0, 0)),
                  pl.BlockSpec((None, d, tn), lambda i, j: (layer, 0, j))],
        out_specs=pl.BlockSpec((tm, tn), lambda i, j: (i, j)),
        out_shape=jax.ShapeDtypeStruct((t, n), F32),
        scratch_shapes=[pltpu.VMEM((tm, d), BF16)],
        compiler_params=_params("parallel", "arbitrary"),
        name=name,
    )(x, g, w)


def _inproj(x, g, w, layer, qkg, rope, tm, name, kv_batch=None):
    t, d = x.shape
    tn = ATTN_WIDTH
    ra, rp, rm = rope
    nrep = ra.shape[0] // tm
    rope_spec = pl.BlockSpec((tm, LANES), lambda i, j: (i % nrep, 0))
    out_specs = [pl.BlockSpec((tm, tn), lambda i, j: (i, j))]
    out_shape = [jax.ShapeDtypeStruct((t, IN_COLS), F32)]
    if kv_batch is not None:
        batch, seq = kv_batch
        tps = seq // tm
        out_specs += [pl.BlockSpec((None, ATTN_WIDTH, tm), lambda i, j: (i // tps, 0, i % tps)),
                      pl.BlockSpec((tm, N_HEADS, V_DIM), lambda i, j: (i, 0, 0))]
        out_shape += [jax.ShapeDtypeStruct((batch, ATTN_WIDTH, seq), F32),
                      jax.ShapeDtypeStruct((t, N_HEADS, V_DIM), F32)]
    res = pl.pallas_call(
        functools.partial(_inproj_kernel, emit_kv=kv_batch is not None),
        grid=(t // tm, IN_COLS // tn),
        in_specs=[pl.BlockSpec((tm, d), lambda i, j: (i, 0)),
                  pl.BlockSpec((1, d), lambda i, j: (0, 0)),
                  pl.BlockSpec((None, d, tn), lambda i, j: (layer, 0, j)),
                  pl.BlockSpec((None, 1, LANES), lambda i, j: (jnp.minimum(j, 1), 0, 0)),
                  rope_spec, rope_spec, rope_spec],
        out_specs=out_specs,
        out_shape=out_shape,
        scratch_shapes=[pltpu.VMEM((tm, d), BF16)],
        compiler_params=_params("parallel", "arbitrary"),
        name=name,
    )(x, g, w, qkg, ra, rp, rm)
    return res if kv_batch is not None else res[0]


def _rope_tables(pos):
    inv_freq = ROPE_THETA ** (-jnp.arange(0, ROT_DIM, 2, dtype=F32) / ROT_DIM)
    ang = pos.astype(F32)[:, None] * inv_freq[None, :]
    cos, sin = jnp.cos(ang), jnp.sin(ang)
    p = pos.shape[0]
    half = ROT_DIM // 2
    one = jnp.ones((p, HEAD_DIM - ROT_DIM), F32)
    zero = jnp.zeros((p, HEAD_DIM - ROT_DIM), F32)
    zh = jnp.zeros((p, half), F32)
    a = jnp.concatenate([cos, cos, one], axis=-1)
    bp = jnp.concatenate([zh, sin, zero], axis=-1)
    bm = jnp.concatenate([-sin, zh, zero], axis=-1)
    return tuple(jnp.concatenate([t, t], axis=-1) for t in (a, bp, bm))


def _lambda_full(lamv_ref, lam0):
    lv = lamv_ref[...]
    s1 = jnp.sum(lv[0:1] * lv[1:2], axis=-1, keepdims=True)
    s2 = jnp.sum(lv[2:3] * lv[3:4], axis=-1, keepdims=True)
    return jnp.exp(s1) - jnp.exp(s2) + lam0


def _pattn_kernel(q_ref, k_ref, v_ref, lamv_ref, sg_ref, o_ref, kb_ref, vb_ref, m_ref, acc_ref,
                  *, lam0, tq):
    qi = pl.program_id(2)
    seq = k_ref.shape[0]

    @pl.when(qi == 0)
    def _():
        kb_ref[...] = k_ref[...].astype(BF16)
        vb_ref[:, :V_DIM] = v_ref[...].astype(BF16)
        lane = lax.broadcasted_iota(jnp.int32, (seq, LANES), 1)
        vb_ref[:, V_DIM:] = jnp.where(lane == 0, 1.0, 0.0).astype(BF16)

    q = q_ref[...] * (HEAD_DIM ** -0.5)
    lane = lax.broadcasted_iota(jnp.int32, (tq, LANES), 1)
    q0 = jnp.where(lane < HEAD_DIM, q, 0.0).astype(BF16)
    q1 = jnp.where(lane >= HEAD_DIM, q, 0.0).astype(BF16)
    qq = jnp.concatenate([q0, q1], axis=0)

    m_ref[...] = jnp.full(m_ref.shape, NEG_INF, F32)
    acc_ref[...] = jnp.zeros(acc_ref.shape, F32)

    def chunk(kj, masked):
        off = pl.multiple_of(kj * tq, tq)
        ks = kb_ref[pl.ds(off, tq), :]
        vs = vb_ref[pl.ds(off, tq), :]
        for rb in range(2 * tq // ATTN_RB):
            rs = slice(rb * ATTN_RB, (rb + 1) * ATTN_RB)
            s = _dot_nt(qq[rs], ks)
            if masked:
                row = lax.broadcasted_iota(jnp.int32, s.shape, 0) + rb * ATTN_RB
                col = lax.broadcasted_iota(jnp.int32, s.shape, 1)
                s = jnp.where(col <= (row & (tq - 1)), s, MASK_VALUE)
            m_old = m_ref[rs, :]
            m_new = jnp.maximum(m_old, jnp.max(s, axis=-1, keepdims=True))
            alpha = jnp.exp(m_old - m_new)
            p = jnp.exp(s - jnp.concatenate([m_new] * (tq // LANES), axis=1)).astype(BF16)
            acc_ref[rs, :] = jnp.concatenate([alpha, alpha], axis=1) * acc_ref[rs, :] + _dot(p, vs)
            m_ref[rs, :] = m_new

    def body(kj, carry):
        chunk(kj, False)
        return carry

    lax.fori_loop(0, qi, body, 0)
    chunk(qi, True)

    acc = acc_ref[...]
    o = acc[:, :V_DIM] / acc[:, V_DIM:V_DIM + 1]
    lam = _lambda_full(lamv_ref, lam0)
    a = o[:tq] - lam * o[tq:]
    o_ref[...] = (_rms_rows(a, sg_ref[...]) * (1.0 - lam0)).astype(o_ref.dtype)


def _prompt_attention(y, lamv, sg, lam0, batch, seq):
    tq = ATTN_TQ
    nq = seq // tq
    kern = functools.partial(_pattn_kernel, lam0=lam0, tq=tq)
    return pl.pallas_call(
        kern,
        grid=(batch, N_HEADS, nq),
        in_specs=[pl.BlockSpec((tq, LANES), lambda b, h, i: (b * nq + i, h)),
                  pl.BlockSpec((seq, LANES), lambda b, h, i: (b, N_HEADS + h)),
                  pl.BlockSpec((seq, LANES), lambda b, h, i: (b, 2 * N_HEADS + h)),
                  pl.BlockSpec((4, LANES), lambda b, h, i: (0, 0)),
                  pl.BlockSpec((1, LANES), lambda b, h, i: (0, 0))],
        out_specs=pl.BlockSpec((tq, LANES), lambda b, h, i: (b * nq + i, h)),
        out_shape=jax.ShapeDtypeStruct((batch * seq, ATTN_WIDTH), BF16),
        scratch_shapes=[pltpu.VMEM((seq, LANES), BF16),
                        pltpu.VMEM((seq, 2 * LANES), BF16),
                        pltpu.VMEM((2 * tq, LANES), F32),
                        pltpu.VMEM((2 * tq, 2 * LANES), F32)],
        compiler_params=_params("parallel", "parallel", "arbitrary"),
        name="prompt_attention",
    )(y, y, y, lamv, sg)


def _sattn_kernel(pt_ref, q_ref, kn_ref, vn_ref, lamv_ref, sg_ref, *rest, lam0, npg):
    k_refs = rest[:npg]
    v_refs = rest[npg:2 * npg]
    o_ref, qt_ref, knp_ref, vnp_ref, m_ref, l_ref, acc_ref = rest[2 * npg:]
    g = pl.program_id(1)
    dec_seq = q_ref.shape[0]
    rows = 2 * dec_seq
    nrow = N_HEADS * rows

    @pl.when(g == 0)
    def _():
        q = q_ref[...] * (HEAD_DIM ** -0.5)
        row = lax.broadcasted_iota(jnp.int32, (nrow, ATTN_WIDTH), 0)
        lane = lax.broadcasted_iota(jnp.int32, (nrow, ATTN_WIDTH), 1)
        qq = jnp.zeros((nrow, ATTN_WIDTH), F32)
        for i in range(dec_seq):
            qq = jnp.where((row & (dec_seq - 1)) == i, q[i:i + 1, :], qq)
        qt_ref[...] = jnp.where(lane // HEAD_DIM == row // dec_seq, qq, 0.0).astype(BF16)
        knp_ref[...] = jnp.zeros(knp_ref.shape, F32)
        vnp_ref[...] = jnp.zeros(vnp_ref.shape, F32)
        m_ref[...] = jnp.full(m_ref.shape, NEG_INF, F32)
        l_ref[...] = jnp.zeros(l_ref.shape, F32)
        acc_ref[...] = jnp.zeros(acc_ref.shape, F32)

    def attend(st, weighted_values):
        m_old = m_ref[...]
        m_new = jnp.maximum(m_old, jnp.max(st, axis=-1, keepdims=True))
        alpha = jnp.exp(m_old - m_new)
        p = jnp.exp(st - m_new)
        l_ref[...] = alpha * l_ref[...] + jnp.sum(p, axis=-1, keepdims=True)
        acc_ref[...] = alpha * acc_ref[...] + weighted_values(p)
        m_ref[...] = m_new

    kc = jnp.concatenate([r[...].astype(BF16) for r in k_refs], axis=1)
    st = _dot(qt_ref[...], kc)

    def page_values(p):
        outs = []
        for h in range(N_HEADS):
            vh = jnp.concatenate(
                [r[pl.ds(h, PAGE_SIZE, stride=N_HEADS), :].astype(BF16) for r in v_refs], axis=0)
            outs.append(_dot(p[h * rows:(h + 1) * rows, :].astype(BF16), vh))
        return jnp.concatenate(outs, axis=0)

    attend(st, page_values)

    @pl.when(g == pl.num_programs(1) - 1)
    def _():
        knp_ref[0:dec_seq, :] = kn_ref[...]
        vnp_ref[0:dec_seq, :] = vn_ref[...]
        sn = _dot_nt(qt_ref[...], knp_ref[...].astype(BF16))
        row = lax.broadcasted_iota(jnp.int32, sn.shape, 0)
        col = lax.broadcasted_iota(jnp.int32, sn.shape, 1)
        sn = jnp.where(col <= (row & (dec_seq - 1)), sn, MASK_VALUE)

        def new_values(p):
            o = _dot(p.astype(BF16), vnp_ref[...].astype(BF16))
            return jnp.concatenate(
                [o[h * rows:(h + 1) * rows, h * V_DIM:(h + 1) * V_DIM] for h in range(N_HEADS)], axis=0)

        attend(sn, new_values)

        o = acc_ref[...] / l_ref[...]
        lam = _lambda_full(lamv_ref, lam0)
        a = o - lam * pltpu.roll(o, nrow - dec_seq, 0)
        a = _rms_rows(a, sg_ref[...]) * (1.0 - lam0)
        for h in range(N_HEADS):
            o_ref[:, h * V_DIM:(h + 1) * V_DIM] = a[h * rows:h * rows + dec_seq, :]


def _sample_attention(y3, cache_kt, cache_vr, page_table, layer, lamv, sg, lam0):
    dec_b, dec_seq, _ = y3.shape
    n_pages = page_table.shape[1]
    npg = PAGES_PER_STEP
    assert n_pages % npg == 0 and dec_seq & (dec_seq - 1) == 0
    nrow = N_HEADS * 2 * dec_seq
    pt = page_table.reshape(-1)

    def page_spec(p):
        return pl.BlockSpec((None, None, ATTN_WIDTH, PAGE_SIZE),
                            lambda b, g, pt_ref: (layer, pt_ref[b * n_pages + g * npg + p], 0, 0))

    def col_spec(c):
        return pl.BlockSpec((None, dec_seq, ATTN_WIDTH), lambda b, g, pt_ref: (b, 0, c))

    grid_spec = pltpu.PrefetchScalarGridSpec(
        num_scalar_prefetch=1,
        grid=(dec_b, n_pages // npg),
        in_specs=[col_spec(0), col_spec(1), col_spec(2),
                  pl.BlockSpec((4, LANES), lambda b, g, pt_ref: (0, 0)),
                  pl.BlockSpec((1, LANES), lambda b, g, pt_ref: (0, 0))]
                 + [page_spec(p) for p in range(npg)] + [page_spec(p) for p in range(npg)],
        out_specs=pl.BlockSpec((None, dec_seq, ATTN_WIDTH), lambda b, g, pt_ref: (b, 0, 0)),
        scratch_shapes=[pltpu.VMEM((nrow, ATTN_WIDTH), BF16),
                        pltpu.VMEM((PAGE_SIZE, ATTN_WIDTH), F32),
                        pltpu.VMEM((PAGE_SIZE, ATTN_WIDTH), F32),
                        pltpu.VMEM((nrow, 1), F32),
                        pltpu.VMEM((nrow, 1), F32),
                        pltpu.VMEM((nrow, V_DIM), F32)],
    )
    kern = functools.partial(_sattn_kernel, lam0=lam0, npg=npg)
    return pl.pallas_call(
        kern,
        grid_spec=grid_spec,
        out_shape=jax.ShapeDtypeStruct((dec_b, dec_seq, ATTN_WIDTH), F32),
        compiler_params=_params("parallel", "arbitrary"),
        name="sample_attention",
    )(pt, y3, y3, y3, lamv, sg, *([cache_kt] * npg), *([cache_vr] * npg))


def _conv_prompt_kernel(gb_ref, gc_ref, xc_ref, hgc_ref, hxc_ref, w_ref, c_ref, st_ref, *, tiles_per_seq):
    i = pl.program_id(0)
    u = gc_ref[...] * xc_ref[...]
    tm = u.shape[0]
    uh = hgc_ref[...] * hxc_ref[...]
    uh = jnp.where(i % tiles_per_seq == 0, 0.0, uh)
    row = lax.broadcasted_iota(jnp.int32, u.shape, 0)
    u1 = jnp.where(row == 0, uh[7:8], pltpu.roll(u, 1, 0))
    u2 = jnp.where(row == 0, uh[6:7], jnp.where(row == 1, uh[7:8], pltpu.roll(u, 2, 0)))
    w = w_ref[...]
    conv = w[0:1] * u2 + w[1:2] * u1 + w[2:3] * u
    c_ref[...] = (gb_ref[...] * conv).astype(c_ref.dtype)
    st_ref[...] = u[tm - (CONV_K - 1):tm]


def _conv_prompt(y, conv_w, batch, seq, tm):
    t = y.shape[0]
    tps = seq // tm
    hb = tm // 8
    kern = functools.partial(_conv_prompt_kernel, tiles_per_seq=tps)
    wide = lambda c: pl.BlockSpec((tm, CONV_WIDTH), lambda i: (i, c))
    halo = lambda c: pl.BlockSpec((8, CONV_WIDTH), lambda i: (jnp.maximum(i * hb - 1, 0), c))
    return pl.pallas_call(
        kern,
        grid=(t // tm,),
        in_specs=[wide(3), wide(4), wide(5), halo(4), halo(5),
                  pl.BlockSpec((CONV_K, CONV_WIDTH), lambda i: (0, 0))],
        out_specs=[pl.BlockSpec((tm, CONV_WIDTH), lambda i: (i, 0)),
                   pl.BlockSpec((None, CONV_K - 1, CONV_WIDTH), lambda i: (i // tps, 0, 0))],
        out_shape=[jax.ShapeDtypeStruct((t, CONV_WIDTH), BF16),
                   jax.ShapeDtypeStruct((batch, CONV_K - 1, CONV_WIDTH), F32)],
        compiler_params=_params("arbitrary"),
        name="conv_prompt",
    )(y, y, y, y, y, conv_w)


def _conv_sample_kernel(gb_ref, gc_ref, xc_ref, e1_ref, e2_ref, w_ref, c_ref, u_ref, *, dec_seq):
    u = gc_ref[...] * xc_ref[...]
    pos = lax.broadcasted_iota(jnp.int32, u.shape, 0) & (dec_seq - 1)
    u1 = jnp.where(pos == 0, e1_ref[...], pltpu.roll(u, 1, 0))
    u2 = jnp.where(pos <= 1, e2_ref[...], pltpu.roll(u, 2, 0))
    w = w_ref[...]
    conv = w[0:1] * u2 + w[1:2] * u1 + w[2:3] * u
    c_ref[...] = (gb_ref[...] * conv).astype(c_ref.dtype)
    u_ref[...] = u


def _conv_sample(y, conv_w, prev, dec_seq):
    t = y.shape[0]
    dec_b = t // dec_seq
    zeros = jnp.zeros((dec_b, dec_seq - 1, CONV_WIDTH), F32)
    e1 = jnp.concatenate([prev[:, 1:2], zeros], axis=1).reshape(t, CONV_WIDTH)
    e2 = jnp.concatenate([prev, zeros[:, 1:]], axis=1).reshape(t, CONV_WIDTH)
    kern = functools.partial(_conv_sample_kernel, dec_seq=dec_seq)
    wide = lambda c: pl.BlockSpec((t, CONV_WIDTH), lambda i: (0, c))
    full = pl.BlockSpec((t, CONV_WIDTH), lambda i: (0, 0))
    return pl.pallas_call(
        kern,
        grid=(1,),
        in_specs=[wide(3), wide(4), wide(5), full, full,
                  pl.BlockSpec((CONV_K, CONV_WIDTH), lambda i: (0, 0))],
        out_specs=[full, full],
        out_shape=[jax.ShapeDtypeStruct((t, CONV_WIDTH), BF16),
                   jax.ShapeDtypeStruct((t, CONV_WIDTH), F32)],
        compiler_params=_params("arbitrary"),
        name="conv_sample",
    )(y, y, y, e1, e2, conv_w)


def _outproj_kernel(x_ref, a_ref, c_ref, wa_ref, wc_ref, o_ref):
    o_ref[...] = (x_ref[...] + _dot(a_ref[...].astype(BF16), wa_ref[...])
                  + _dot(c_ref[...].astype(BF16), wc_ref[...]))


def _outproj(x, a, c, w, layer, tm, tn, name):
    t, d = x.shape
    return pl.pallas_call(
        _outproj_kernel,
        grid=(t // tm, d // tn),
        in_specs=[pl.BlockSpec((tm, tn), lambda i, j: (i, j)),
                  pl.BlockSpec((tm, ATTN_WIDTH), lambda i, j: (i, 0)),
                  pl.BlockSpec((tm, CONV_WIDTH), lambda i, j: (i, 0)),
                  pl.BlockSpec((None, ATTN_WIDTH, tn), lambda i, j: (layer, 0, j)),
                  pl.BlockSpec((None, CONV_WIDTH, tn), lambda i, j: (layer, 1, j))],
        out_specs=pl.BlockSpec((tm, tn), lambda i, j: (i, j)),
        out_shape=jax.ShapeDtypeStruct((t, d), F32),
        compiler_params=_params("parallel", "arbitrary"),
        name=name,
    )(x, a, c, w, w)


def _top_with_index(s, k):
    nrows = s.shape[0]
    riota = lax.broadcasted_iota(jnp.int32, s.shape, 0).astype(F32)
    vals, idxs = [], []
    for _ in range(k):
        m = jnp.max(s, axis=0, keepdims=True)
        i = jnp.min(jnp.where(s == m, riota, float(nrows)), axis=0, keepdims=True)
        vals.append(m)
        idxs.append(i)
        s = jnp.where(riota == i, NEG_INF, s)
    return vals, idxs


def _stack_rows(rows, nrows, fill):
    r = lax.broadcasted_iota(jnp.int32, (nrows, rows[0].shape[1]), 0)
    out = jnp.full(r.shape, fill, F32)
    for j, row in enumerate(rows):
        out = jnp.where(r == j, row, out)
    return out


def _candidates(t1, i1, t2, i2):
    k = PEER_TOPK
    t2_all = _stack_rows(t2, k, NEG_INF)
    i2_all = _stack_rows(i2, k, 0.0)
    row8 = lax.broadcasted_iota(jnp.int32, (8, t2_all.shape[1]), 0)
    vals, codes, tail_v, tail_c = [], [], [], []
    for j1 in range(k):
        cnt = k // (j1 + 1)
        base = i1[j1] * float(PEER_NKEYS)
        if cnt > 8:
            vals.append(t1[j1] + t2_all)
            codes.append(base + i2_all)
        elif cnt >= 4:
            v = t1[j1] + t2_all[:8]
            vals.append(v if cnt == 8 else jnp.where(row8 < cnt, v, NEG_INF))
            codes.append(base + i2_all[:8])
        else:
            tail_v += [t1[j1] + t2[j2] for j2 in range(cnt)]
            tail_c += [base + i2[j2] for j2 in range(cnt)]
    tail_rows = -(-len(tail_v) // 8) * 8
    vals.append(_stack_rows(tail_v, tail_rows, NEG_INF))
    codes.append(_stack_rows(tail_c, tail_rows, 0.0))
    return jnp.concatenate(vals, axis=0), jnp.concatenate(codes, axis=0)


def _peer_select_kernel(x_ref, g_ref, hq_ref, sk_ref, xn_ref, gate_ref, at_ref, bt_ref, gt_ref,
                        atm_ref, btm_ref, gtm_ref, m_ref):
    tm = x_ref.shape[0]
    xn_ref[...] = _rms_rows(x_ref[...], g_ref[...]).astype(BF16)

    for h in range(PEER_HEADS):
        c0 = h * 2 * PEER_NKEYS
        q1 = hq_ref[:, c0:c0 + PEER_NKEYS].astype(BF16)
        q2 = hq_ref[:, c0 + PEER_NKEYS:c0 + 2 * PEER_NKEYS].astype(BF16)
        s1 = _dot_nt(sk_ref[h, 0], q1)
        s2 = _dot_nt(sk_ref[h, 1], q2)
        t1, i1 = _top_with_index(s1, PEER_TOPK)
        t2, i2 = _top_with_index(s2, PEER_TOPK)
        cand, code = _candidates(t1, i1, t2, i2)
        sc, rows = _top_with_index(cand, PEER_TOPK)
        riota = lax.broadcasted_iota(jnp.int32, cand.shape, 0).astype(F32)
        ex = [jnp.exp(c - sc[0]) for c in sc]
        z = ex[0]
        for v in ex[1:]:
            z = z + v
        for j in range(PEER_TOPK):
            picked = jnp.sum(jnp.where(riota == rows[j], code, 0.0), axis=0, keepdims=True)
            n1 = jnp.floor(picked * (1.0 / PEER_NKEYS))
            slot = h * PEER_TOPK + j
            at_ref[slot:slot + 1, :] = n1
            bt_ref[slot:slot + 1, :] = picked - n1 * float(PEER_NKEYS)
            gt_ref[slot:slot + 1, :] = ex[j] / z

    atm_ref[...] = at_ref[...].T
    btm_ref[...] = bt_ref[...].T
    gtm_ref[...] = gt_ref[...].T

    niota = lax.broadcasted_iota(jnp.int32, (PEER_NKEYS, LANES), 0).astype(F32)

    def group(gi, carry):
        t0 = pl.multiple_of(gi * GATE_GROUP, GATE_GROUP)
        for tt in range(GATE_GROUP):
            a_row = atm_ref[pl.ds(t0 + tt, 1), :]
            b_row = btm_ref[pl.ds(t0 + tt, 1), :]
            g_row = gtm_ref[pl.ds(t0 + tt, 1), :]
            lhs = jnp.where(niota == a_row, g_row, 0.0).astype(BF16)
            rhs = jnp.where(niota == b_row, 1.0, 0.0).astype(BF16)
            m_ref[tt * GATE_PITCH:tt * GATE_PITCH + PEER_NKEYS, :] = _dot_nt(lhs, rhs)
        for n1 in range(PEER_NKEYS):
            rows_n1 = m_ref[pl.ds(n1, GATE_GROUP, stride=GATE_PITCH), :]
            gate_ref[pl.ds(t0, GATE_GROUP), n1 * PEER_NKEYS:(n1 + 1) * PEER_NKEYS] = rows_n1.astype(BF16)
        return carry

    lax.fori_loop(0, tm // GATE_GROUP, group, 0)


def _peer_select(x, g, hq, subkeys, layer, tm, name):
    t, d = x.shape
    nk = PEER_NKEYS
    slots = PEER_HEADS * PEER_TOPK
    return pl.pallas_call(
        _peer_select_kernel,
        grid=(t // tm,),
        in_specs=[pl.BlockSpec((tm, d), lambda i: (i, 0)),
                  pl.BlockSpec((1, d), lambda i: (0, 0)),
                  pl.BlockSpec((tm, PEER_HEADS * 2 * nk), lambda i: (i, 0)),
                  pl.BlockSpec((None, PEER_HEADS, 2, nk, nk), lambda i: (layer, 0, 0, 0, 0))],
        out_specs=[pl.BlockSpec((tm, d), lambda i: (i, 0)),
                   pl.BlockSpec((tm, PEER_N), lambda i: (i, 0))],
        out_shape=[jax.ShapeDtypeStruct((t, d), BF16),
                   jax.ShapeDtypeStruct((t, PEER_N), BF16)],
        scratch_shapes=[pltpu.VMEM((slots, tm), F32)] * 3 + [pltpu.VMEM((tm, slots), F32)] * 3
                       + [pltpu.VMEM((GATE_GROUP * GATE_PITCH, nk), F32)],
        compiler_params=_params("parallel"),
        name=name,
    )(x, g, hq, subkeys)


def _peer_dense_kernel(xn_ref, gate_ref, ut_ref, v_ref, o_ref):
    e = pl.program_id(1)
    a = _dot(xn_ref[...], ut_ref[...])
    act = 0.5 * a * (1.0 + lax.erf(a * SQRT_HALF))
    weighted = (gate_ref[...].astype(F32) * act).astype(BF16)

    @pl.when(e == 0)
    def _():
        o_ref[...] = jnp.zeros(o_ref.shape, F32)

    o_ref[...] += _dot(weighted, v_ref[...])


def _peer_dense(xn, gate, ut, v, layer, tm, name):
    t, d = xn.shape
    ec = PEER_EC
    return pl.pallas_call(
        _peer_dense_kernel,
        grid=(t // tm, PEER_N // ec),
        in_specs=[pl.BlockSpec((tm, d), lambda i, e: (i, 0)),
                  pl.BlockSpec((tm, ec), lambda i, e: (i, e)),
                  pl.BlockSpec((None, d, ec), lambda i, e: (layer, 0, e)),
                  pl.BlockSpec((None, ec, d), lambda i, e: (layer, e, 0))],
        out_specs=pl.BlockSpec((tm, d), lambda i, e: (i, 0)),
        out_shape=jax.ShapeDtypeStruct((t, d), F32),
        compiler_params=_params("parallel", "arbitrary"),
        name=name,
    )(xn, gate, ut, v)


def _ple_kernel(x_ref, peer_ref, p_ref, g_ref, wg_ref, wp_ref, o_ref):
    x = x_ref[...] + peer_ref[...]
    xn = _rms_rows(x, g_ref[...]).astype(BF16)
    z = _dot(xn, wg_ref[...])
    gate = 1.0 / (1.0 + jnp.exp(-z))
    o_ref[...] = x + gate * _dot(p_ref[...].astype(BF16), wp_ref[...])


def _ple(x, peer, p, g, wg, wp, layer, tm, name):
    t, d = x.shape
    return pl.pallas_call(
        _ple_kernel,
        grid=(t // tm,),
        in_specs=[pl.BlockSpec((tm, d), lambda i: (i, 0)),
                  pl.BlockSpec((tm, d), lambda i: (i, 0)),
                  pl.BlockSpec((tm, PLE_DIM), lambda i: (i, 0)),
                  pl.BlockSpec((1, d), lambda i: (0, 0)),
                  pl.BlockSpec((None, d, d), lambda i: (layer, 0, 0)),
                  pl.BlockSpec((None, PLE_DIM, d), lambda i: (layer, 0, 0))],
        out_specs=pl.BlockSpec((tm, d), lambda i: (i, 0)),
        out_shape=jax.ShapeDtypeStruct((t, d), F32),
        compiler_params=_params("parallel"),
        name=name,
    )(x, peer, p, g, wg, wp)


def _lambda_init(layer_idx):
    return 0.8 - 0.6 * math.exp(-0.3 * layer_idx)


def _token_tail(x, a, c, wts, lw, layer, p_l, tm, tag):
    x1 = _outproj(x, a, c, wts["w_out"], layer, tm, D_MODEL // 2, "outproj_" + tag)
    hq = _norm_mm(x1, lw["ffn_norm_g"], wts["peer_wq"], layer, tm, D_MODEL // 2, "peer_query_" + tag)
    xn, gate = _peer_select(x1, lw["ffn_norm_g"], hq, wts["peer_subkeys"], layer, min(tm, SELECT_TM),
                            "peer_select_" + tag)
    peer = _peer_dense(xn, gate, wts["peer_ut"], wts["peer_v"], layer, tm, "peer_dense_" + tag)
    return _ple(x1, peer, p_l, lw["ple_norm_g"], wts["ple_gate_w"], wts["ple_proj_w"], layer,
                min(tm, PLE_TM), "ple_" + tag)


def kernel(x_prompt, x_sample, cache_k, cache_v, state_conv, page_table, p_prompt, p_sample,
           attn_norm_g, w_in, q_norm_g, k_norm_g, lam_q1, lam_k1, lam_q2, lam_k2, subln_g,
           conv_w, w_out, ffn_norm_g, peer_wq, peer_subkeys, peer_u, peer_v,
           ple_norm_g, ple_gate_w, ple_proj_w):
    batch, seq, d = x_prompt.shape
    dec_b, dec_seq, _ = x_sample.shape
    depth = w_in.shape[0]
    n_pool = cache_k.shape[1]
    past_len = page_table.shape[1] * cache_k.shape[2]
    tp, ts = batch * seq, dec_b * dec_seq

    rope_p = _rope_tables(jnp.arange(seq))
    rope_s = _rope_tables(past_len + (jnp.arange(ts) % dec_seq))
    ck = cache_k.transpose(0, 1, 3, 4, 5, 2).reshape(depth, n_pool, ATTN_WIDTH, PAGE_SIZE)
    cv = cache_v.reshape(depth, n_pool, PAGE_SIZE * N_HEADS, V_DIM)

    yp = x_prompt.reshape(tp, d)
    ys = x_sample.reshape(ts, d)
    outs = [[] for _ in range(6)]
    row = lambda v: v.reshape(1, -1)
    wts = {
        "w_in": w_in.astype(BF16),
        "w_out": w_out.astype(BF16),
        "peer_wq": peer_wq.astype(BF16),
        "peer_subkeys": peer_subkeys.astype(BF16),
        "peer_ut": jnp.swapaxes(peer_u, 1, 2).astype(BF16),
        "peer_v": peer_v.astype(BF16),
        "ple_gate_w": ple_gate_w.astype(BF16),
        "ple_proj_w": ple_proj_w.astype(BF16),
    }
    for l in range(depth):
        lam0 = _lambda_init(l)
        lw = {"ffn_norm_g": row(ffn_norm_g[l]), "ple_norm_g": row(ple_norm_g[l])}
        g_in = row(attn_norm_g[l])
        qkg = jnp.stack([jnp.tile(q_norm_g[l], 2), jnp.tile(k_norm_g[l], 2)]).reshape(2, 1, LANES)
        lamv = jnp.pad(jnp.stack([lam_q1[l], lam_k1[l], lam_q2[l], lam_k2[l]]),
                       ((0, 0), (0, LANES - HEAD_DIM)))
        sg = row(subln_g[l])

        y, k_t, v_t = _inproj(yp, g_in, wts["w_in"], l, qkg, rope_p, PROMPT_TM, "inproj_prompt",
                                kv_batch=(batch, seq))
        a = _prompt_attention(y, lamv, sg, lam0, batch, seq)
        c, cstate = _conv_prompt(y, conv_w[l], batch, seq, PROMPT_TM)
        outs[0].append(k_t.reshape(batch, N_HEADS, 2, HEAD_DIM, seq).transpose(0, 4, 1, 2, 3))
        outs[1].append(v_t.reshape(batch, seq, N_HEADS, V_DIM))
        outs[2].append(cstate)
        yp = _token_tail(yp, a, c, wts, lw, l, p_prompt[l].reshape(tp, PLE_DIM), PROMPT_TM, "prompt")

        y = _inproj(ys, g_in, wts["w_in"], l, qkg, rope_s, ts, "inproj_sample")
        a = _sample_attention(y.reshape(dec_b, dec_seq, IN_COLS), ck, cv, page_table, l, lamv, sg, lam0)
        c, u = _conv_sample(y, conv_w[l], state_conv[l], dec_seq)
        outs[3].append(y[:, ATTN_WIDTH:2 * ATTN_WIDTH].reshape(dec_b, dec_seq, N_HEADS, 2, HEAD_DIM))
        outs[4].append(y[:, 2 * ATTN_WIDTH:3 * ATTN_WIDTH].reshape(dec_b, dec_seq, N_HEADS, V_DIM))
        outs[5].append(u.reshape(dec_b, dec_seq, CONV_WIDTH)[:, dec_seq - (CONV_K - 1):])
        ys = _token_tail(ys, a.reshape(ts, ATTN_WIDTH), c, wts, lw, l, p_sample[l].reshape(ts, PLE_DIM), ts,
                         "sample")

    return (yp.reshape(batch, seq, d), ys.reshape(dec_b, dec_seq, d),
            jnp.stack(outs[0]), jnp.stack(outs[1]), jnp.stack(outs[2]),
            jnp.stack(outs[3]), jnp.stack(outs[4]), jnp.stack(outs[5]))
```

```python
import functools
import math

import jax
import jax.numpy as jnp
from jax import lax
from jax.experimental import pallas as pl
from jax.experimental.pallas import tpu as pltpu

F32 = jnp.float32
BF16 = jnp.bfloat16

D_MODEL = 2048
N_HEADS = 8
HEAD_DIM = 64
V_DIM = 128
ATTN_WIDTH = 1024
CONV_WIDTH = 1024
IN_COLS = 3 * ATTN_WIDTH + 3 * CONV_WIDTH
ROT_DIM = 16
ROPE_THETA = 500000.0
CONV_K = 3
PLE_DIM = 256
PEER_HEADS = 8
PEER_NKEYS = 128
PEER_N = PEER_NKEYS * PEER_NKEYS
PEER_TOPK = 16
EPS = 1e-6
PAGE_SIZE = 128

LANES = 128
VMEM_LIMIT_BYTES = 56 * 1024 * 1024
NEG_INF = float("-inf")
MASK_VALUE = float(jnp.finfo(jnp.float32).min)
SQRT_HALF = 0.7071067811865476

PROMPT_TM = 512
ATTN_TQ = 512
ATTN_RB = 256
PAGES_PER_STEP = 16
PEER_EC = 1024
SELECT_TM = 256
GATE_GROUP = 16
GATE_PITCH = PEER_NKEYS + 8
PLE_TM = 256


def _params(*sem):
    return pltpu.CompilerParams(dimension_semantics=sem, vmem_limit_bytes=VMEM_LIMIT_BYTES)


def _rms_rows(x, g):
    ms = jnp.mean(x * x, axis=-1, keepdims=True)
    return x * lax.rsqrt(ms + EPS) * g


def _dot(a, b):
    return jnp.dot(a, b, preferred_element_type=F32)


def _dot_nt(a, b):
    return lax.dot_general(a, b, (((1,), (1,)), ((), ())), preferred_element_type=F32)


def _norm_mm_kernel(x_ref, g_ref, w_ref, o_ref, hn_ref):
    @pl.when(pl.program_id(1) == 0)
    def _():
        hn_ref[...] = _rms_rows(x_ref[...], g_ref[...]).astype(BF16)

    o_ref[...] = _dot(hn_ref[...], w_ref[...])


def _inproj_kernel(x_ref, g_ref, w_ref, qkg_ref, ra_ref, rp_ref, rm_ref, o_ref, *rest, emit_kv):
    hn_ref = rest[-1]
    j = pl.program_id(1)

    @pl.when(j == 0)
    def _():
        hn_ref[...] = _rms_rows(x_ref[...], g_ref[...]).astype(BF16)

    y = _dot(hn_ref[...], w_ref[...])

    def norm_rope(emit_t):
        r = lax.broadcasted_iota(jnp.int32, (LANES, LANES), 0) // HEAD_DIM
        c = lax.broadcasted_iota(jnp.int32, (LANES, LANES), 1) // HEAD_DIM
        group_sum = jnp.where(r == c, 1.0, 0.0).astype(BF16)
        for h in range(N_HEADS):
            yh = y[:, h * LANES:(h + 1) * LANES]
            sq = yh * yh
            hi = sq.astype(BF16)
            lo = (sq - hi.astype(F32)).astype(BF16)
            ss = _dot(hi, group_sum) + _dot(lo, group_sum)
            yn = yh * lax.rsqrt(ss * (1.0 / HEAD_DIM) + EPS) * qkg_ref[...]
            out = (yn * ra_ref[...]
                   + pltpu.roll(yn, ROT_DIM // 2, 1) * rp_ref[...]
                   + pltpu.roll(yn, LANES - ROT_DIM // 2, 1) * rm_ref[...])
            o_ref[:, h * LANES:(h + 1) * LANES] = out
            if emit_t:
                rest[0][h * LANES:(h + 1) * LANES, :] = out.T

    if emit_kv:
        pl.when(j == 0)(lambda: norm_rope(False))
        pl.when(j == 1)(lambda: norm_rope(True))
    else:
        pl.when(j < 2)(lambda: norm_rope(False))

    @pl.when(j >= 2)
    def _():
        o_ref[...] = y

    if emit_kv:
        @pl.when(j == 2)
        def _():
            heads = jnp.stack([y[:, h * V_DIM:(h + 1) * V_DIM] for h in range(N_HEADS)], axis=0)
            rest[1][...] = jnp.swapaxes(heads, 0, 1)


def _norm_mm(x, g, w, layer, tm, tn, name):
    t, d = x.shape
    n = w.shape[2]
    return pl.pallas_call(
        _norm_mm_kernel,
        grid=(t // tm, n // tn),
        in_specs=[pl.BlockSpec((tm, d), lambda i, j: (i, 0)),
                  pl.BlockSpec((1, d), lambda i, j: (0, 0)),
                  pl.BlockSpec((None, d, tn), lambda i, j: (layer, 0, j))],
        out_specs=pl.BlockSpec((tm, tn), lambda i, j: (i, j)),
        out_shape=jax.ShapeDtypeStruct((t, n), F32),
        scratch_shapes=[pltpu.VMEM((tm, d), BF16)],
        compiler_params=_params("parallel", "arbitrary"),
        name=name,
    )(x, g, w)


def _inproj(x, g, w, layer, qkg, rope, tm, name, kv_batch=None):
    t, d = x.shape
    tn = ATTN_WIDTH
    ra, rp, rm = rope
    nrep = ra.shape[0] // tm
    rope_spec = pl.BlockSpec((tm, LANES), lambda i, j: (i % nrep, 0))
    out_specs = [pl.BlockSpec((tm, tn), lambda i, j: (i, j))]
    out_shape = [jax.ShapeDtypeStruct((t, IN_COLS), F32)]
    if kv_batch is not None:
        batch, seq = kv_batch
        tps = seq // tm
        out_specs += [pl.BlockSpec((None, ATTN_WIDTH, tm), lambda i, j: (i // tps, 0, i % tps)),
                      pl.BlockSpec((tm, N_HEADS, V_DIM), lambda i, j: (i, 0, 0))]
        out_shape += [jax.ShapeDtypeStruct((batch, ATTN_WIDTH, seq), F32),
                      jax.ShapeDtypeStruct((t, N_HEADS, V_DIM), F32)]
    res = pl.pallas_call(
        functools.partial(_inproj_kernel, emit_kv=kv_batch is not None),
        grid=(t // tm, IN_COLS // tn),
        in_specs=[pl.BlockSpec((tm, d), lambda i, j: (i, 0)),
                  pl.BlockSpec((1, d), lambda i, j: (0, 0)),
                  pl.BlockSpec((None, d, tn), lambda i, j: (layer, 0, j)),
                  pl.BlockSpec((None, 1, LANES), lambda i, j: (jnp.minimum(j, 1), 0, 0)),
                  rope_spec, rope_spec, rope_spec],
        out_specs=out_specs,
        out_shape=out_shape,
        scratch_shapes=[pltpu.VMEM((tm, d), BF16)],
        compiler_params=_params("parallel", "arbitrary"),
        name=name,
    )(x, g, w, qkg, ra, rp, rm)
    return res if kv_batch is not None else res[0]


def _rope_tables(pos):
    inv_freq = ROPE_THETA ** (-jnp.arange(0, ROT_DIM, 2, dtype=F32) / ROT_DIM)
    ang = pos.astype(F32)[:, None] * inv_freq[None, :]
    cos, sin = jnp.cos(ang), jnp.sin(ang)
    p = pos.shape[0]
    half = ROT_DIM // 2
    one = jnp.ones((p, HEAD_DIM - ROT_DIM), F32)
    zero = jnp.zeros((p, HEAD_DIM - ROT_DIM), F32)
    zh = jnp.zeros((p, half), F32)
    a = jnp.concatenate([cos, cos, one], axis=-1)
    bp = jnp.concatenate([zh, sin, zero], axis=-1)
    bm = jnp.concatenate([-sin, zh, zero], axis=-1)
    return tuple(jnp.concatenate([t, t], axis=-1) for t in (a, bp, bm))


def _lambda_full(lamv_ref, lam0):
    lv = lamv_ref[...]
    s1 = jnp.sum(lv[0:1] * lv[1:2], axis=-1, keepdims=True)
    s2 = jnp.sum(lv[2:3] * lv[3:4], axis=-1, keepdims=True)
    return jnp.exp(s1) - jnp.exp(s2) + lam0


def _pattn_kernel(q_ref, k_ref, v_ref, lamv_ref, sg_ref, o_ref, kb_ref, vb_ref, m_ref, acc_ref,
                  *, lam0, tq):
    qi = pl.program_id(2)
    seq = k_ref.shape[0]

    @pl.when(qi == 0)
    def _():
        kb_ref[...] = k_ref[...].astype(BF16)
        vb_ref[:, :V_DIM] = v_ref[...].astype(BF16)
        lane = lax.broadcasted_iota(jnp.int32, (seq, LANES), 1)
        vb_ref[:, V_DIM:] = jnp.where(lane == 0, 1.0, 0.0).astype(BF16)

    q = q_ref[...] * (HEAD_DIM ** -0.5)
    lane = lax.broadcasted_iota(jnp.int32, (tq, LANES), 1)
    q0 = jnp.where(lane < HEAD_DIM, q, 0.0).astype(BF16)
    q1 = jnp.where(lane >= HEAD_DIM, q, 0.0).astype(BF16)
    qq = jnp.concatenate([q0, q1], axis=0)

    m_ref[...] = jnp.full(m_ref.shape, NEG_INF, F32)
    acc_ref[...] = jnp.zeros(acc_ref.shape, F32)

    def chunk(kj, masked):
        off = pl.multiple_of(kj * tq, tq)
        ks = kb_ref[pl.ds(off, tq), :]
        vs = vb_ref[pl.ds(off, tq), :]
        for rb in range(2 * tq // ATTN_RB):
            rs = slice(rb * ATTN_RB, (rb + 1) * ATTN_RB)
            s = _dot_nt(qq[rs], ks)
            if masked:
                row = lax.broadcasted_iota(jnp.int32, s.shape, 0) + rb * ATTN_RB
                col = lax.broadcasted_iota(jnp.int32, s.shape, 1)
                s = jnp.where(col <= (row & (tq - 1)), s, MASK_VALUE)
            m_old = m_ref[rs, :]
            m_new = jnp.maximum(m_old, jnp.max(s, axis=-1, keepdims=True))
            alpha = jnp.exp(m_old - m_new)
            p = jnp.exp(s - jnp.concatenate([m_new] * (tq // LANES), axis=1)).astype(BF16)
            acc_ref[rs, :] = jnp.concatenate([alpha, alpha], axis=1) * acc_ref[rs, :] + _dot(p, vs)
            m_ref[rs, :] = m_new

    def body(kj, carry):
        chunk(kj, False)
        return carry

    lax.fori_loop(0, qi, body, 0)
    chunk(qi, True)

    acc = acc_ref[...]
    o = acc[:, :V_DIM] / acc[:, V_DIM:V_DIM + 1]
    lam = _lambda_full(lamv_ref, lam0)
    a = o[:tq] - lam * o[tq:]
    o_ref[...] = (_rms_rows(a, sg_ref[...]) * (1.0 - lam0)).astype(o_ref.dtype)


def _prompt_attention(y, lamv, sg, lam0, batch, seq):
    tq = ATTN_TQ
    nq = seq // tq
    kern = functools.partial(_pattn_kernel, lam0=lam0, tq=tq)
    return pl.pallas_call(
        kern,
        grid=(batch, N_HEADS, nq),
        in_specs=[pl.BlockSpec((tq, LANES), lambda b, h, i: (b * nq + i, h)),
                  pl.BlockSpec((seq, LANES), lambda b, h, i: (b, N_HEADS + h)),
                  pl.BlockSpec((seq, LANES), lambda b, h, i: (b, 2 * N_HEADS + h)),
                  pl.BlockSpec((4, LANES), lambda b, h, i: (0, 0)),
                  pl.BlockSpec((1, LANES), lambda b, h, i: (0, 0))],
        out_specs=pl.BlockSpec((tq, LANES), lambda b, h, i: (b * nq + i, h)),
        out_shape=jax.ShapeDtypeStruct((batch * seq, ATTN_WIDTH), BF16),
        scratch_shapes=[pltpu.VMEM((seq, LANES), BF16),
                        pltpu.VMEM((seq, 2 * LANES), BF16),
                        pltpu.VMEM((2 * tq, LANES), F32),
                        pltpu.VMEM((2 * tq, 2 * LANES), F32)],
        compiler_params=_params("parallel", "parallel", "arbitrary"),
        name="prompt_attention",
    )(y, y, y, lamv, sg)


def _sattn_kernel(pt_ref, q_ref, kn_ref, vn_ref, lamv_ref, sg_ref, *rest, lam0, npg):
    k_refs = rest[:npg]
    v_refs = rest[npg:2 * npg]
    o_ref, qt_ref, knp_ref, vnp_ref, m_ref, l_ref, acc_ref = rest[2 * npg:]
    g = pl.program_id(1)
    dec_seq = q_ref.shape[0]
    rows = 2 * dec_seq
    nrow = N_HEADS * rows

    @pl.when(g == 0)
    def _():
        q = q_ref[...] * (HEAD_DIM ** -0.5)
        row = lax.broadcasted_iota(jnp.int32, (nrow, ATTN_WIDTH), 0)
        lane = lax.broadcasted_iota(jnp.int32, (nrow, ATTN_WIDTH), 1)
        qq = jnp.zeros((nrow, ATTN_WIDTH), F32)
        for i in range(dec_seq):
            qq = jnp.where((row & (dec_seq - 1)) == i, q[i:i + 1, :], qq)
        qt_ref[...] = jnp.where(lane // HEAD_DIM == row // dec_seq, qq, 0.0).astype(BF16)
        knp_ref[...] = jnp.zeros(knp_ref.shape, F32)
        vnp_ref[...] = jnp.zeros(vnp_ref.shape, F32)
        m_ref[...] = jnp.full(m_ref.shape, NEG_INF, F32)
        l_ref[...] = jnp.zeros(l_ref.shape, F32)
        acc_ref[...] = jnp.zeros(acc_ref.shape, F32)

    def attend(st, weighted_values):
        m_old = m_ref[...]
        m_new = jnp.maximum(m_old, jnp.max(st, axis=-1, keepdims=True))
        alpha = jnp.exp(m_old - m_new)
        p = jnp.exp(st - m_new)
        l_ref[...] = alpha * l_ref[...] + jnp.sum(p, axis=-1, keepdims=True)
        acc_ref[...] = alpha * acc_ref[...] + weighted_values(p)
        m_ref[...] = m_new

    kc = jnp.concatenate([r[...].astype(BF16) for r in k_refs], axis=1)
    st = _dot(qt_ref[...], kc)

    def page_values(p):
        outs = []
        for h in range(N_HEADS):
            vh = jnp.concatenate(
                [r[pl.ds(h, PAGE_SIZE, stride=N_HEADS), :].astype(BF16) for r in v_refs], axis=0)
            outs.append(_dot(p[h * rows:(h + 1) * rows, :].astype(BF16), vh))
        return jnp.concatenate(outs, axis=0)

    attend(st, page_values)

    @pl.when(g == pl.num_programs(1) - 1)
    def _():
        knp_ref[0:dec_seq, :] = kn_ref[...]
        vnp_ref[0:dec_seq, :] = vn_ref[...]
        sn = _dot_nt(qt_ref[...], knp_ref[...].astype(BF16))
        row = lax.broadcasted_iota(jnp.int32, sn.shape, 0)
        col = lax.broadcasted_iota(jnp.int32, sn.shape, 1)
        sn = jnp.where(col <= (row & (dec_seq - 1)), sn, MASK_VALUE)

        def new_values(p):
            o = _dot(p.astype(BF16), vnp_ref[...].astype(BF16))
            return jnp.concatenate(
                [o[h * rows:(h + 1) * rows, h * V_DIM:(h + 1) * V_DIM] for h in range(N_HEADS)], axis=0)

        attend(sn, new_values)

        o = acc_ref[...] / l_ref[...]
        lam = _lambda_full(lamv_ref, lam0)
        a = o - lam * pltpu.roll(o, nrow - dec_seq, 0)
        a = _rms_rows(a, sg_ref[...]) * (1.0 - lam0)
        for h in range(N_HEADS):
            o_ref[:, h * V_DIM:(h + 1) * V_DIM] = a[h * rows:h * rows + dec_seq, :]


def _sample_attention(y3, cache_kt, cache_vr, page_table, layer, lamv, sg, lam0):
    dec_b, dec_seq, _ = y3.shape
    n_pages = page_table.shape[1]
    npg = PAGES_PER_STEP
    assert n_pages % npg == 0 and dec_seq & (dec_seq - 1) == 0
    nrow = N_HEADS * 2 * dec_seq
    pt = page_table.reshape(-1)

    def page_spec(p):
        return pl.BlockSpec((None, None, ATTN_WIDTH, PAGE_SIZE),
                            lambda b, g, pt_ref: (layer, pt_ref[b * n_pages + g * npg + p], 0, 0))

    def col_spec(c):
        return pl.BlockSpec((None, dec_seq, ATTN_WIDTH), lambda b, g, pt_ref: (b, 0, c))

    grid_spec = pltpu.PrefetchScalarGridSpec(
        num_scalar_prefetch=1,
        grid=(dec_b, n_pages // npg),
        in_specs=[col_spec(0), col_spec(1), col_spec(2),
                  pl.BlockSpec((4, LANES), lambda b, g, pt_ref: (0, 0)),
                  pl.BlockSpec((1, LANES), lambda b, g, pt_ref: (0, 0))]
                 + [page_spec(p) for p in range(npg)] + [page_spec(p) for p in range(npg)],
        out_specs=pl.BlockSpec((None, dec_seq, ATTN_WIDTH), lambda b, g, pt_ref: (b, 0, 0)),
        scratch_shapes=[pltpu.VMEM((nrow, ATTN_WIDTH), BF16),
                        pltpu.VMEM((PAGE_SIZE, ATTN_WIDTH), F32),
                        pltpu.VMEM((PAGE_SIZE, ATTN_WIDTH), F32),
                        pltpu.VMEM((nrow, 1), F32),
                        pltpu.VMEM((nrow, 1), F32),
                        pltpu.VMEM((nrow, V_DIM), F32)],
    )
    kern = functools.partial(_sattn_kernel, lam0=lam0, npg=npg)
    return pl.pallas_call(
        kern,
        grid_spec=grid_spec,
        out_shape=jax.ShapeDtypeStruct((dec_b, dec_seq, ATTN_WIDTH), F32),
        compiler_params=_params("parallel", "arbitrary"),
        name="sample_attention",
    )(pt, y3, y3, y3, lamv, sg, *([cache_kt] * npg), *([cache_vr] * npg))


def _conv_prompt_kernel(gb_ref, gc_ref, xc_ref, hgc_ref, hxc_ref, w_ref, c_ref, st_ref, *, tiles_per_seq):
    i = pl.program_id(0)
    u = gc_ref[...] * xc_ref[...]
    tm = u.shape[0]
    uh = hgc_ref[...] * hxc_ref[...]
    uh = jnp.where(i % tiles_per_seq == 0, 0.0, uh)
    row = lax.broadcasted_iota(jnp.int32, u.shape, 0)
    u1 = jnp.where(row == 0, uh[7:8], pltpu.roll(u, 1, 0))
    u2 = jnp.where(row == 0, uh[6:7], jnp.where(row == 1, uh[7:8], pltpu.roll(u, 2, 0)))
    w = w_ref[...]
    conv = w[0:1] * u2 + w[1:2] * u1 + w[2:3] * u
    c_ref[...] = (gb_ref[...] * conv).astype(c_ref.dtype)
    st_ref[...] = u[tm - (CONV_K - 1):tm]


def _conv_prompt(y, conv_w, batch, seq, tm):
    t = y.shape[0]
    tps = seq // tm
    hb = tm // 8
    kern = functools.partial(_conv_prompt_kernel, tiles_per_seq=tps)
    wide = lambda c: pl.BlockSpec((tm, CONV_WIDTH), lambda i: (i, c))
    halo = lambda c: pl.BlockSpec((8, CONV_WIDTH), lambda i: (jnp.maximum(i * hb - 1, 0), c))
    return pl.pallas_call(
        kern,
        grid=(t // tm,),
        in_specs=[wide(3), wide(4), wide(5), halo(4), halo(5),
                  pl.BlockSpec((CONV_K, CONV_WIDTH), lambda i: (0, 0))],
        out_specs=[pl.BlockSpec((tm, CONV_WIDTH), lambda i: (i, 0)),
                   pl.BlockSpec((None, CONV_K - 1, CONV_WIDTH), lambda i: (i // tps, 0, 0))],
        out_shape=[jax.ShapeDtypeStruct((t, CONV_WIDTH), BF16),
                   jax.ShapeDtypeStruct((batch, CONV_K - 1, CONV_WIDTH), F32)],
        compiler_params=_params("arbitrary"),
        name="conv_prompt",
    )(y, y, y, y, y, conv_w)


def _conv_sample_kernel(gb_ref, gc_ref, xc_ref, e1_ref, e2_ref, w_ref, c_ref, u_ref, *, dec_seq):
    u = gc_ref[...] * xc_ref[...]
    pos = lax.broadcasted_iota(jnp.int32, u.shape, 0) & (dec_seq - 1)
    u1 = jnp.where(pos == 0, e1_ref[...], pltpu.roll(u, 1, 0))
    u2 = jnp.where(pos <= 1, e2_ref[...], pltpu.roll(u, 2, 0))
    w = w_ref[...]
    conv = w[0:1] * u2 + w[1:2] * u1 + w[2:3] * u
    c_ref[...] = (gb_ref[...] * conv).astype(c_ref.dtype)
    u_ref[...] = u


def _conv_sample(y, conv_w, prev, dec_seq):
    t = y.shape[0]
    dec_b = t // dec_seq
    zeros = jnp.zeros((dec_b, dec_seq - 1, CONV_WIDTH), F32)
    e1 = jnp.concatenate([prev[:, 1:2], zeros], axis=1).reshape(t, CONV_WIDTH)
    e2 = jnp.concatenate([prev, zeros[:, 1:]], axis=1).reshape(t, CONV_WIDTH)
    kern = functools.partial(_conv_sample_kernel, dec_seq=dec_seq)
    wide = lambda c: pl.BlockSpec((t, CONV_WIDTH), lambda i: (0, c))
    full = pl.BlockSpec((t, CONV_WIDTH), lambda i: (0, 0))
    return pl.pallas_call(
        kern,
        grid=(1,),
        in_specs=[wide(3), wide(4), wide(5), full, full,
                  pl.BlockSpec((CONV_K, CONV_WIDTH), lambda i: (0, 0))],
        out_specs=[full, full],
        out_shape=[jax.ShapeDtypeStruct((t, CONV_WIDTH), BF16),
                   jax.ShapeDtypeStruct((t, CONV_WIDTH), F32)],
        compiler_params=_params("arbitrary"),
        name="conv_sample",
    )(y, y, y, e1, e2, conv_w)


def _outproj_kernel(x_ref, a_ref, c_ref, wa_ref, wc_ref, o_ref):
    o_ref[...] = (x_ref[...] + _dot(a_ref[...].astype(BF16), wa_ref[...])
                  + _dot(c_ref[...].astype(BF16), wc_ref[...]))


def _outproj(x, a, c, w, layer, tm, tn, name):
    t, d = x.shape
    return pl.pallas_call(
        _outproj_kernel,
        grid=(t // tm, d // tn),
        in_specs=[pl.BlockSpec((tm, tn), lambda i, j: (i, j)),
                  pl.BlockSpec((tm, ATTN_WIDTH), lambda i, j: (i, 0)),
                  pl.BlockSpec((tm, CONV_WIDTH), lambda i, j: (i, 0)),
                  pl.BlockSpec((None, ATTN_WIDTH, tn), lambda i, j: (layer, 0, j)),
                  pl.BlockSpec((None, CONV_WIDTH, tn), lambda i, j: (layer, 1, j))],
        out_specs=pl.BlockSpec((tm, tn), lambda i, j: (i, j)),
        out_shape=jax.ShapeDtypeStruct((t, d), F32),
        compiler_params=_params("parallel", "arbitrary"),
        name=name,
    )(x, a, c, w, w)


def _top_with_index(s, k):
    nrows = s.shape[0]
    riota = lax.broadcasted_iota(jnp.int32, s.shape, 0).astype(F32)
    vals, idxs = [], []
    for _ in range(k):
        m = jnp.max(s, axis=0, keepdims=True)
        i = jnp.min(jnp.where(s == m, riota, float(nrows)), axis=0, keepdims=True)
        vals.append(m)
        idxs.append(i)
        s = jnp.where(riota == i, NEG_INF, s)
    return vals, idxs


def _stack_rows(rows, nrows, fill):
    r = lax.broadcasted_iota(jnp.int32, (nrows, rows[0].shape[1]), 0)
    out = jnp.full(r.shape, fill, F32)
    for j, row in enumerate(rows):
        out = jnp.where(r == j, row, out)
    return out


def _candidates(t1, i1, t2, i2):
    k = PEER_TOPK
    t2_all = _stack_rows(t2, k, NEG_INF)
    i2_all = _stack_rows(i2, k, 0.0)
    row8 = lax.broadcasted_iota(jnp.int32, (8, t2_all.shape[1]), 0)
    vals, codes, tail_v, tail_c = [], [], [], []
    for j1 in range(k):
        cnt = k // (j1 + 1)
        base = i1[j1] * float(PEER_NKEYS)
        if cnt > 8:
            vals.append(t1[j1] + t2_all)
            codes.append(base + i2_all)
        elif cnt >= 4:
            v = t1[j1] + t2_all[:8]
            vals.append(v if cnt == 8 else jnp.where(row8 < cnt, v, NEG_INF))
            codes.append(base + i2_all[:8])
        else:
            tail_v += [t1[j1] + t2[j2] for j2 in range(cnt)]
            tail_c += [base + i2[j2] for j2 in range(cnt)]
    tail_rows = -(-len(tail_v) // 8) * 8
    vals.append(_stack_rows(tail_v, tail_rows, NEG_INF))
    codes.append(_stack_rows(tail_c, tail_rows, 0.0))
    return jnp.concatenate(vals, axis=0), jnp.concatenate(codes, axis=0)


def _peer_select_kernel(x_ref, g_ref, hq_ref, sk_ref, xn_ref, gate_ref, at_ref, bt_ref, gt_ref,
                        atm_ref, btm_ref, gtm_ref, m_ref):
    tm = x_ref.shape[0]
    xn_ref[...] = _rms_rows(x_ref[...], g_ref[...]).astype(BF16)

    for h in range(PEER_HEADS):
        c0 = h * 2 * PEER_NKEYS
        q1 = hq_ref[:, c0:c0 + PEER_NKEYS].astype(BF16)
        q2 = hq_ref[:, c0 + PEER_NKEYS:c0 + 2 * PEER_NKEYS].astype(BF16)
        s1 = _dot_nt(sk_ref[h, 0], q1)
        s2 = _dot_nt(sk_ref[h, 1], q2)
        t1, i1 = _top_with_index(s1, PEER_TOPK)
        t2, i2 = _top_with_index(s2, PEER_TOPK)
        cand, code = _candidates(t1, i1, t2, i2)
        sc, rows = _top_with_index(cand, PEER_TOPK)
        riota = lax.broadcasted_iota(jnp.int32, cand.shape, 0).astype(F32)
        ex = [jnp.exp(c - sc[0]) for c in sc]
        z = ex[0]
        for v in ex[1:]:
            z = z + v
        for j in range(PEER_TOPK):
            picked = jnp.sum(jnp.where(riota == rows[j], code, 0.0), axis=0, keepdims=True)
            n1 = jnp.floor(picked * (1.0 / PEER_NKEYS))
            slot = h * PEER_TOPK + j
            at_ref[slot:slot + 1, :] = n1
            bt_ref[slot:slot + 1, :] = picked - n1 * float(PEER_NKEYS)
            gt_ref[slot:slot + 1, :] = ex[j] / z

    atm_ref[...] = at_ref[...].T
    btm_ref[...] = bt_ref[...].T
    gtm_ref[...] = gt_ref[...].T

    niota = lax.broadcasted_iota(jnp.int32, (PEER_NKEYS, LANES), 0).astype(F32)

    def group(gi, carry):
        t0 = pl.multiple_of(gi * GATE_GROUP, GATE_GROUP)
        for tt in range(GATE_GROUP):
            a_row = atm_ref[pl.ds(t0 + tt, 1), :]
            b_row = btm_ref[pl.ds(t0 + tt, 1), :]
            g_row = gtm_ref[pl.ds(t0 + tt, 1), :]
            lhs = jnp.where(niota == a_row, g_row, 0.0).astype(BF16)
            rhs = jnp.where(niota == b_row, 1.0, 0.0).astype(BF16)
            m_ref[tt * GATE_PITCH:tt * GATE_PITCH + PEER_NKEYS, :] = _dot_nt(lhs, rhs)
        for n1 in range(PEER_NKEYS):
            rows_n1 = m_ref[pl.ds(n1, GATE_GROUP, stride=GATE_PITCH), :]
            gate_ref[pl.ds(t0, GATE_GROUP), n1 * PEER_NKEYS:(n1 + 1) * PEER_NKEYS] = rows_n1.astype(BF16)
        return carry

    lax.fori_loop(0, tm // GATE_GROUP, group, 0)


def _peer_select(x, g, hq, subkeys, layer, tm, name):
    t, d = x.shape
    nk = PEER_NKEYS
    slots = PEER_HEADS * PEER_TOPK
    return pl.pallas_call(
        _peer_select_kernel,
        grid=(t // tm,),
        in_specs=[pl.BlockSpec((tm, d), lambda i: (i, 0)),
                  pl.BlockSpec((1, d), lambda i: (0, 0)),
                  pl.BlockSpec((tm, PEER_HEADS * 2 * nk), lambda i: (i, 0)),
                  pl.BlockSpec((None, PEER_HEADS, 2, nk, nk), lambda i: (layer, 0, 0, 0, 0))],
        out_specs=[pl.BlockSpec((tm, d), lambda i: (i, 0)),
                   pl.BlockSpec((tm, PEER_N), lambda i: (i, 0))],
        out_shape=[jax.ShapeDtypeStruct((t, d), BF16),
                   jax.ShapeDtypeStruct((t, PEER_N), BF16)],
        scratch_shapes=[pltpu.VMEM((slots, tm), F32)] * 3 + [pltpu.VMEM((tm, slots), F32)] * 3
                       + [pltpu.VMEM((GATE_GROUP * GATE_PITCH, nk), F32)],
        compiler_params=_params("parallel"),
        name=name,
    )(x, g, hq, subkeys)


def _peer_dense_kernel(xn_ref, gate_ref, u_ref, v_ref, o_ref):
    e = pl.program_id(1)
    a = _dot_nt(xn_ref[...], u_ref[...])
    twice_gelu = a * (1.0 + lax.erf(a * SQRT_HALF))
    weighted = twice_gelu.astype(BF16) * (gate_ref[...] * jnp.asarray(0.5, BF16))

    @pl.when(e == 0)
    def _():
        o_ref[...] = jnp.zeros(o_ref.shape, F32)

    o_ref[...] += _dot(weighted, v_ref[...])


def _peer_dense(xn, gate, u, v, layer, tm, name):
    t, d = xn.shape
    ec = PEER_EC
    return pl.pallas_call(
        _peer_dense_kernel,
        grid=(t // tm, PEER_N // ec),
        in_specs=[pl.BlockSpec((tm, d), lambda i, e: (i, 0)),
                  pl.BlockSpec((tm, ec), lambda i, e: (i, e)),
                  pl.BlockSpec((None, ec, d), lambda i, e: (layer, e, 0)),
                  pl.BlockSpec((None, ec, d), lambda i, e: (layer, e, 0))],
        out_specs=pl.BlockSpec((tm, d), lambda i, e: (i, 0)),
        out_shape=jax.ShapeDtypeStruct((t, d), F32),
        compiler_params=_params("parallel", "arbitrary"),
        name=name,
    )(xn, gate, u, v)


def _ple_kernel(x_ref, peer_ref, p_ref, g_ref, wg_ref, wp_ref, o_ref):
    x = x_ref[...] + peer_ref[...]
    xn = _rms_rows(x, g_ref[...]).astype(BF16)
    z = _dot(xn, wg_ref[...])
    gate = 1.0 / (1.0 + jnp.exp(-z))
    o_ref[...] = x + gate * _dot(p_ref[...].astype(BF16), wp_ref[...])


def _ple(x, peer, p, g, wg, wp, layer, tm, name):
    t, d = x.shape
    return pl.pallas_call(
        _ple_kernel,
        grid=(t // tm,),
        in_specs=[pl.BlockSpec((tm, d), lambda i: (i, 0)),
                  pl.BlockSpec((tm, d), lambda i: (i, 0)),
                  pl.BlockSpec((tm, PLE_DIM), lambda i: (i, 0)),
                  pl.BlockSpec((1, d), lambda i: (0, 0)),
                  pl.BlockSpec((None, d, d), lambda i: (layer, 0, 0)),
                  pl.BlockSpec((None, PLE_DIM, d), lambda i: (layer, 0, 0))],
        out_specs=pl.BlockSpec((tm, d), lambda i: (i, 0)),
        out_shape=jax.ShapeDtypeStruct((t, d), F32),
        compiler_params=_params("parallel"),
        name=name,
    )(x, peer, p, g, wg, wp)


def _lambda_init(layer_idx):
    return 0.8 - 0.6 * math.exp(-0.3 * layer_idx)


def _token_tail(x, a, c, wts, lw, layer, p_l, tm, tag):
    x1 = _outproj(x, a, c, wts["w_out"], layer, tm, D_MODEL // 2, "outproj_" + tag)
    hq = _norm_mm(x1, lw["ffn_norm_g"], wts["peer_wq"], layer, tm, D_MODEL // 2, "peer_query_" + tag)
    xn, gate = _peer_select(x1, lw["ffn_norm_g"], hq, wts["peer_subkeys"], layer, min(tm, SELECT_TM),
                            "peer_select_" + tag)
    peer = _peer_dense(xn, gate, wts["peer_u"], wts["peer_v"], layer, tm, "peer_dense_" + tag)
    return _ple(x1, peer, p_l, lw["ple_norm_g"], wts["ple_gate_w"], wts["ple_proj_w"], layer,
                min(tm, PLE_TM), "ple_" + tag)


def kernel(x_prompt, x_sample, cache_k, cache_v, state_conv, page_table, p_prompt, p_sample,
           attn_norm_g, w_in, q_norm_g, k_norm_g, lam_q1, lam_k1, lam_q2, lam_k2, subln_g,
           conv_w, w_out, ffn_norm_g, peer_wq, peer_subkeys, peer_u, peer_v,
           ple_norm_g, ple_gate_w, ple_proj_w):
    batch, seq, d = x_prompt.shape
    dec_b, dec_seq, _ = x_sample.shape
    depth = w_in.shape[0]
    n_pool = cache_k.shape[1]
    past_len = page_table.shape[1] * cache_k.shape[2]
    tp, ts = batch * seq, dec_b * dec_seq

    rope_p = _rope_tables(jnp.arange(seq))
    rope_s = _rope_tables(past_len + (jnp.arange(ts) % dec_seq))
    ck = cache_k.transpose(0, 1, 3, 4, 5, 2).reshape(depth, n_pool, ATTN_WIDTH, PAGE_SIZE)
    cv = cache_v.reshape(depth, n_pool, PAGE_SIZE * N_HEADS, V_DIM)

    yp = x_prompt.reshape(tp, d)
    ys = x_sample.reshape(ts, d)
    outs = [[] for _ in range(6)]
    row = lambda v: v.reshape(1, -1)
    wts = {
        "w_in": w_in.astype(BF16),
        "w_out": w_out.astype(BF16),
        "peer_wq": peer_wq.astype(BF16),
        "peer_subkeys": peer_subkeys.astype(BF16),
        "peer_u": peer_u.astype(BF16),
        "peer_v": peer_v.astype(BF16),
        "ple_gate_w": ple_gate_w.astype(BF16),
        "ple_proj_w": ple_proj_w.astype(BF16),
    }
    for l in range(depth):
        lam0 = _lambda_init(l)
        lw = {"ffn_norm_g": row(ffn_norm_g[l]), "ple_norm_g": row(ple_norm_g[l])}
        g_in = row(attn_norm_g[l])
        qkg = jnp.stack([jnp.tile(q_norm_g[l], 2), jnp.tile(k_norm_g[l], 2)]).reshape(2, 1, LANES)
        lamv = jnp.pad(jnp.stack([lam_q1[l], lam_k1[l], lam_q2[l], lam_k2[l]]),
                       ((0, 0), (0, LANES - HEAD_DIM)))
        sg = row(subln_g[l])

        y, k_t, v_t = _inproj(yp, g_in, wts["w_in"], l, qkg, rope_p, PROMPT_TM, "inproj_prompt",
                                kv_batch=(batch, seq))
        a = _prompt_attention(y, lamv, sg, lam0, batch, seq)
        c, cstate = _conv_prompt(y, conv_w[l], batch, seq, PROMPT_TM)
        outs[0].append(k_t.reshape(batch, N_HEADS, 2, HEAD_DIM, seq).transpose(0, 4, 1, 2, 3))
        outs[1].append(v_t.reshape(batch, seq, N_HEADS, V_DIM))
        outs[2].append(cstate)
        yp = _token_tail(yp, a, c, wts, lw, l, p_prompt[l].reshape(tp, PLE_DIM), PROMPT_TM, "prompt")

        y = _inproj(ys, g_in, wts["w_in"], l, qkg, rope_s, ts, "inproj_sample")
        a = _sample_attention(y.reshape(dec_b, dec_seq, IN_COLS), ck, cv, page_table, l, lamv, sg, lam0)
        c, u = _conv_sample(y, conv_w[l], state_conv[l], dec_seq)
        outs[3].append(y[:, ATTN_WIDTH:2 * ATTN_WIDTH].reshape(dec_b, dec_seq, N_HEADS, 2, HEAD_DIM))
        outs[4].append(y[:, 2 * ATTN_WIDTH:3 * ATTN_WIDTH].reshape(dec_b, dec_seq, N_HEADS, V_DIM))
        outs[5].append(u.reshape(dec_b, dec_seq, CONV_WIDTH)[:, dec_seq - (CONV_K - 1):])
        ys = _token_tail(ys, a.reshape(ts, ATTN_WIDTH), c, wts, lw, l, p_sample[l].reshape(ts, PLE_DIM), ts,
                         "sample")

    return (yp.reshape(batch, seq, d), ys.reshape(dec_b, dec_seq, d),
            jnp.stack(outs[0]), jnp.stack(outs[1]), jnp.stack(outs[2]),
            jnp.stack(outs[3]), jnp.stack(outs[4]), jnp.stack(outs[5]))
```

```python
import functools
import math

import jax
import jax.numpy as jnp
from jax import lax
from jax.experimental import pallas as pl
from jax.experimental.pallas import tpu as pltpu

F32 = jnp.float32
BF16 = jnp.bfloat16

D_MODEL = 2048
N_HEADS = 8
HEAD_DIM = 64
V_DIM = 128
ATTN_WIDTH = 1024
CONV_WIDTH = 1024
IN_COLS = 3 * ATTN_WIDTH + 3 * CONV_WIDTH
ROT_DIM = 16
ROPE_THETA = 500000.0
CONV_K = 3
PLE_DIM = 256
PEER_HEADS = 8
PEER_NKEYS = 128
PEER_N = PEER_NKEYS * PEER_NKEYS
PEER_TOPK = 16
EPS = 1e-6
PAGE_SIZE = 128

LANES = 128
SUBLANES = 8
VMEM_LIMIT_BYTES = 56 * 1024 * 1024
NEG_INF = float("-inf")
MASK_VALUE = float(jnp.finfo(jnp.float32).min)
SQRT_HALF = 0.7071067811865476

PROMPT_TM = 512
ATTN_TQ = 512
ATTN_RB = 256
PAGES_PER_STEP = 16
PEER_EC = 1024
SELECT_TM = 256
GATE_GROUP = 16
GATE_PITCH = PEER_NKEYS + SUBLANES
PLE_TM = 256


def _params(*sem):
    return pltpu.CompilerParams(dimension_semantics=sem, vmem_limit_bytes=VMEM_LIMIT_BYTES)


def _rms_rows(x, g):
    ms = jnp.mean(x * x, axis=-1, keepdims=True)
    return x * lax.rsqrt(ms + EPS) * g


def _dot(a, b):
    return jnp.dot(a, b, preferred_element_type=F32)


def _dot_nt(a, b):
    return lax.dot_general(a, b, (((1,), (1,)), ((), ())), preferred_element_type=F32)


def _norm_mm_kernel(x_ref, g_ref, w_ref, o_ref, hn_ref):
    @pl.when(pl.program_id(1) == 0)
    def _():
        hn_ref[...] = _rms_rows(x_ref[...], g_ref[...]).astype(BF16)

    o_ref[...] = _dot(hn_ref[...], w_ref[...])


def _inproj_kernel(x_ref, g_ref, w_ref, qkg_ref, ra_ref, rp_ref, rm_ref, o_ref, *rest, emit_kv):
    hn_ref = rest[-1]
    j = pl.program_id(1)

    @pl.when(j == 0)
    def _():
        hn_ref[...] = _rms_rows(x_ref[...], g_ref[...]).astype(BF16)

    y = _dot(hn_ref[...], w_ref[...])

    @pl.when(j == 0)
    def _():
        r = lax.broadcasted_iota(jnp.int32, (LANES, LANES), 0) // HEAD_DIM
        c = lax.broadcasted_iota(jnp.int32, (LANES, LANES), 1) // HEAD_DIM
        group_sum = jnp.where(r == c, 1.0, 0.0).astype(BF16)
        for h in range(2 * N_HEADS):
            is_key = h >= N_HEADS
            yh = y[:, h * LANES:(h + 1) * LANES]
            sq = yh * yh
            hi = sq.astype(BF16)
            lo = (sq - hi.astype(F32)).astype(BF16)
            ss = _dot(hi, group_sum) + _dot(lo, group_sum)
            yn = yh * lax.rsqrt(ss * (1.0 / HEAD_DIM) + EPS) * qkg_ref[1 if is_key else 0]
            out = (yn * ra_ref[...]
                   + pltpu.roll(yn, ROT_DIM // 2, 1) * rp_ref[...]
                   + pltpu.roll(yn, LANES - ROT_DIM // 2, 1) * rm_ref[...])
            o_ref[:, h * LANES:(h + 1) * LANES] = out
            if emit_kv and is_key:
                hk = h - N_HEADS
                rest[0][hk * LANES:(hk + 1) * LANES, :] = out.T

    @pl.when(j > 0)
    def _():
        o_ref[...] = y

    if emit_kv:
        @pl.when(j == 1)
        def _():
            heads = jnp.stack([y[:, h * V_DIM:(h + 1) * V_DIM] for h in range(N_HEADS)], axis=0)
            rest[1][...] = jnp.swapaxes(heads, 0, 1)


def _norm_mm(x, g, w, layer, tm, tn, name):
    t, d = x.shape
    n = w.shape[2]
    return pl.pallas_call(
        _norm_mm_kernel,
        grid=(t // tm, n // tn),
        in_specs=[pl.BlockSpec((tm, d), lambda i, j: (i, 0)),
                  pl.BlockSpec((1, d), lambda i, j: (0, 0)),
                  pl.BlockSpec((None, d, tn), lambda i, j: (layer, 0, j))],
        out_specs=pl.BlockSpec((tm, tn), lambda i, j: (i, j)),
        out_shape=jax.ShapeDtypeStruct((t, n), F32),
        scratch_shapes=[pltpu.VMEM((tm, d), BF16)],
        compiler_params=_params("parallel", "arbitrary"),
        name=name,
    )(x, g, w)


def _inproj(x, g, w, layer, qkg, rope, tm, name, kv_batch=None):
    t, d = x.shape
    tn = 2 * ATTN_WIDTH
    ra, rp, rm = rope
    nrep = ra.shape[0] // tm
    rope_spec = pl.BlockSpec((tm, LANES), lambda i, j: (i % nrep, 0))
    out_specs = [pl.BlockSpec((tm, tn), lambda i, j: (i, j))]
    out_shape = [jax.ShapeDtypeStruct((t, IN_COLS), F32)]
    if kv_batch is not None:
        batch, seq = kv_batch
        tps = seq // tm
        out_specs += [pl.BlockSpec((None, ATTN_WIDTH, tm), lambda i, j: (i // tps, 0, i % tps)),
                      pl.BlockSpec((tm, N_HEADS, V_DIM), lambda i, j: (i, 0, 0))]
        out_shape += [jax.ShapeDtypeStruct((batch, ATTN_WIDTH, seq), F32),
                      jax.ShapeDtypeStruct((t, N_HEADS, V_DIM), F32)]
    res = pl.pallas_call(
        functools.partial(_inproj_kernel, emit_kv=kv_batch is not None),
        grid=(t // tm, IN_COLS // tn),
        in_specs=[pl.BlockSpec((tm, d), lambda i, j: (i, 0)),
                  pl.BlockSpec((1, d), lambda i, j: (0, 0)),
                  pl.BlockSpec((None, d, tn), lambda i, j: (layer, 0, j)),
                  pl.BlockSpec((2, 1, LANES), lambda i, j: (0, 0, 0)),
                  rope_spec, rope_spec, rope_spec],
        out_specs=out_specs,
        out_shape=out_shape,
        scratch_shapes=[pltpu.VMEM((tm, d), BF16)],
        compiler_params=_params("parallel", "arbitrary"),
        name=name,
    )(x, g, w, qkg, ra, rp, rm)
    return res if kv_batch is not None else res[0]


def _rope_tables(pos):
    inv_freq = ROPE_THETA ** (-jnp.arange(0, ROT_DIM, 2, dtype=F32) / ROT_DIM)
    ang = pos.astype(F32)[:, None] * inv_freq[None, :]
    cos, sin = jnp.cos(ang), jnp.sin(ang)
    p = pos.shape[0]
    half = ROT_DIM // 2
    one = jnp.ones((p, HEAD_DIM - ROT_DIM), F32)
    zero = jnp.zeros((p, HEAD_DIM - ROT_DIM), F32)
    zh = jnp.zeros((p, half), F32)
    a = jnp.concatenate([cos, cos, one], axis=-1)
    bp = jnp.concatenate([zh, sin, zero], axis=-1)
    bm = jnp.concatenate([-sin, zh, zero], axis=-1)
    return tuple(jnp.concatenate([t, t], axis=-1) for t in (a, bp, bm))


def _lambda_full(lamv_ref, lam0):
    lv = lamv_ref[...]
    s1 = jnp.sum(lv[0:1] * lv[1:2], axis=-1, keepdims=True)
    s2 = jnp.sum(lv[2:3] * lv[3:4], axis=-1, keepdims=True)
    return jnp.exp(s1) - jnp.exp(s2) + lam0


def _pattn_kernel(q_ref, k_ref, v_ref, lamv_ref, sg_ref, o_ref, kb_ref, vb_ref, m_ref, acc_ref,
                  *, lam0, tq):
    qi = pl.program_id(2)
    seq = k_ref.shape[0]

    @pl.when(qi == 0)
    def _():
        kb_ref[...] = k_ref[...].astype(BF16)
        vb_ref[:, :V_DIM] = v_ref[...].astype(BF16)
        lane = lax.broadcasted_iota(jnp.int32, (seq, LANES), 1)
        vb_ref[:, V_DIM:] = jnp.where(lane == 0, 1.0, 0.0).astype(BF16)

    q = q_ref[...] * (HEAD_DIM ** -0.5)
    lane = lax.broadcasted_iota(jnp.int32, (tq, LANES), 1)
    q0 = jnp.where(lane < HEAD_DIM, q, 0.0).astype(BF16)
    q1 = jnp.where(lane >= HEAD_DIM, q, 0.0).astype(BF16)
    qq = jnp.concatenate([q0, q1], axis=0)

    m_ref[...] = jnp.full(m_ref.shape, NEG_INF, F32)
    acc_ref[...] = jnp.zeros(acc_ref.shape, F32)

    def chunk(kj, masked):
        off = pl.multiple_of(kj * tq, tq)
        ks = kb_ref[pl.ds(off, tq), :]
        vs = vb_ref[pl.ds(off, tq), :]
        for rb in range(2 * tq // ATTN_RB):
            rs = slice(rb * ATTN_RB, (rb + 1) * ATTN_RB)
            s = _dot_nt(qq[rs], ks)
            if masked:
                row = lax.broadcasted_iota(jnp.int32, s.shape, 0) + rb * ATTN_RB
                col = lax.broadcasted_iota(jnp.int32, s.shape, 1)
                s = jnp.where(col <= (row & (tq - 1)), s, MASK_VALUE)
            m_old = m_ref[rs, :]
            m_new = jnp.maximum(m_old, jnp.max(s, axis=-1, keepdims=True))
            alpha = jnp.exp(m_old - m_new)
            p = jnp.exp(s - jnp.concatenate([m_new] * (tq // LANES), axis=1)).astype(BF16)
            acc_ref[rs, :] = jnp.concatenate([alpha, alpha], axis=1) * acc_ref[rs, :] + _dot(p, vs)
            m_ref[rs, :] = m_new

    def body(kj, carry):
        chunk(kj, False)
        return carry

    lax.fori_loop(0, qi, body, 0)
    chunk(qi, True)

    acc = acc_ref[...]
    o = acc[:, :V_DIM] / acc[:, V_DIM:V_DIM + 1]
    lam = _lambda_full(lamv_ref, lam0)
    a = o[:tq] - lam * o[tq:]
    o_ref[...] = (_rms_rows(a, sg_ref[...]) * (1.0 - lam0)).astype(o_ref.dtype)


def _prompt_attention(y, lamv, sg, lam0, batch, seq):
    tq = ATTN_TQ
    nq = seq // tq
    kern = functools.partial(_pattn_kernel, lam0=lam0, tq=tq)
    return pl.pallas_call(
        kern,
        grid=(batch, N_HEADS, nq),
        in_specs=[pl.BlockSpec((tq, LANES), lambda b, h, i: (b * nq + i, h)),
                  pl.BlockSpec((seq, LANES), lambda b, h, i: (b, N_HEADS + h)),
                  pl.BlockSpec((seq, LANES), lambda b, h, i: (b, 2 * N_HEADS + h)),
                  pl.BlockSpec((4, LANES), lambda b, h, i: (0, 0)),
                  pl.BlockSpec((1, LANES), lambda b, h, i: (0, 0))],
        out_specs=pl.BlockSpec((tq, LANES), lambda b, h, i: (b * nq + i, h)),
        out_shape=jax.ShapeDtypeStruct((batch * seq, ATTN_WIDTH), BF16),
        scratch_shapes=[pltpu.VMEM((seq, LANES), BF16),
                        pltpu.VMEM((seq, 2 * LANES), BF16),
                        pltpu.VMEM((2 * tq, LANES), F32),
                        pltpu.VMEM((2 * tq, 2 * LANES), F32)],
        compiler_params=_params("parallel", "parallel", "arbitrary"),
        name="prompt_attention",
    )(y, y, y, lamv, sg)


def _sattn_kernel(pt_ref, q_ref, kn_ref, vn_ref, lamv_ref, sg_ref, *rest, lam0, npg):
    k_refs = rest[:npg]
    v_refs = rest[npg:2 * npg]
    o_ref, qt_ref, knp_ref, vnp_ref, m_ref, l_ref, acc_ref = rest[2 * npg:]
    g = pl.program_id(1)
    dec_seq = q_ref.shape[0]
    rows = 2 * dec_seq
    nrow = N_HEADS * rows

    @pl.when(g == 0)
    def _():
        q = q_ref[...] * (HEAD_DIM ** -0.5)
        row = lax.broadcasted_iota(jnp.int32, (nrow, ATTN_WIDTH), 0)
        lane = lax.broadcasted_iota(jnp.int32, (nrow, ATTN_WIDTH), 1)
        qq = jnp.zeros((nrow, ATTN_WIDTH), F32)
        for i in range(dec_seq):
            qq = jnp.where((row & (dec_seq - 1)) == i, q[i:i + 1, :], qq)
        qt_ref[...] = jnp.where(lane // HEAD_DIM == row // dec_seq, qq, 0.0).astype(BF16)
        knp_ref[...] = jnp.zeros(knp_ref.shape, F32)
        vnp_ref[...] = jnp.zeros(vnp_ref.shape, F32)
        m_ref[...] = jnp.full(m_ref.shape, NEG_INF, F32)
        l_ref[...] = jnp.zeros(l_ref.shape, F32)
        acc_ref[...] = jnp.zeros(acc_ref.shape, F32)

    def attend(st, weighted_values):
        m_old = m_ref[...]
        m_new = jnp.maximum(m_old, jnp.max(st, axis=-1, keepdims=True))
        alpha = jnp.exp(m_old - m_new)
        p = jnp.exp(st - m_new)
        l_ref[...] = alpha * l_ref[...] + jnp.sum(p, axis=-1, keepdims=True)
        acc_ref[...] = alpha * acc_ref[...] + weighted_values(p)
        m_ref[...] = m_new

    kc = jnp.concatenate([r[...].astype(BF16) for r in k_refs], axis=1)
    st = _dot(qt_ref[...], kc)

    def page_values(p):
        outs = []
        for h in range(N_HEADS):
            vh = jnp.concatenate(
                [r[pl.ds(h, PAGE_SIZE, stride=N_HEADS), :].astype(BF16) for r in v_refs], axis=0)
            outs.append(_dot(p[h * rows:(h + 1) * rows, :].astype(BF16), vh))
        return jnp.concatenate(outs, axis=0)

    attend(st, page_values)

    @pl.when(g == pl.num_programs(1) - 1)
    def _():
        knp_ref[0:dec_seq, :] = kn_ref[...]
        vnp_ref[0:dec_seq, :] = vn_ref[...]
        sn = _dot_nt(qt_ref[...], knp_ref[...].astype(BF16))
        row = lax.broadcasted_iota(jnp.int32, sn.shape, 0)
        col = lax.broadcasted_iota(jnp.int32, sn.shape, 1)
        sn = jnp.where(col <= (row & (dec_seq - 1)), sn, MASK_VALUE)

        def new_values(p):
            o = _dot(p.astype(BF16), vnp_ref[...].astype(BF16))
            return jnp.concatenate(
                [o[h * rows:(h + 1) * rows, h * V_DIM:(h + 1) * V_DIM] for h in range(N_HEADS)], axis=0)

        attend(sn, new_values)

        o = acc_ref[...] / l_ref[...]
        lam = _lambda_full(lamv_ref, lam0)
        a = o - lam * pltpu.roll(o, nrow - dec_seq, 0)
        a = _rms_rows(a, sg_ref[...]) * (1.0 - lam0)
        for h in range(N_HEADS):
            o_ref[:, h * V_DIM:(h + 1) * V_DIM] = a[h * rows:h * rows + dec_seq, :]


def _sample_attention(y3, cache_kt, cache_vr, page_table, layer, lamv, sg, lam0):
    dec_b, dec_seq, _ = y3.shape
    n_pages = page_table.shape[1]
    npg = PAGES_PER_STEP
    assert n_pages % npg == 0 and dec_seq & (dec_seq - 1) == 0
    nrow = N_HEADS * 2 * dec_seq
    pt = page_table.reshape(-1)

    def page_spec(p):
        return pl.BlockSpec((None, None, ATTN_WIDTH, PAGE_SIZE),
                            lambda b, g, pt_ref: (layer, pt_ref[b * n_pages + g * npg + p], 0, 0))

    def col_spec(c):
        return pl.BlockSpec((None, dec_seq, ATTN_WIDTH), lambda b, g, pt_ref: (b, 0, c))

    grid_spec = pltpu.PrefetchScalarGridSpec(
        num_scalar_prefetch=1,
        grid=(dec_b, n_pages // npg),
        in_specs=[col_spec(0), col_spec(1), col_spec(2),
                  pl.BlockSpec((4, LANES), lambda b, g, pt_ref: (0, 0)),
                  pl.BlockSpec((1, LANES), lambda b, g, pt_ref: (0, 0))]
                 + [page_spec(p) for p in range(npg)] + [page_spec(p) for p in range(npg)],
        out_specs=pl.BlockSpec((None, dec_seq, ATTN_WIDTH), lambda b, g, pt_ref: (b, 0, 0)),
        scratch_shapes=[pltpu.VMEM((nrow, ATTN_WIDTH), BF16),
                        pltpu.VMEM((PAGE_SIZE, ATTN_WIDTH), F32),
                        pltpu.VMEM((PAGE_SIZE, ATTN_WIDTH), F32),
                        pltpu.VMEM((nrow, 1), F32),
                        pltpu.VMEM((nrow, 1), F32),
                        pltpu.VMEM((nrow, V_DIM), F32)],
    )
    kern = functools.partial(_sattn_kernel, lam0=lam0, npg=npg)
    return pl.pallas_call(
        kern,
        grid_spec=grid_spec,
        out_shape=jax.ShapeDtypeStruct((dec_b, dec_seq, ATTN_WIDTH), F32),
        compiler_params=_params("parallel", "arbitrary"),
        name="sample_attention",
    )(pt, y3, y3, y3, lamv, sg, *([cache_kt] * npg), *([cache_vr] * npg))


def _conv_prompt_kernel(gb_ref, gc_ref, xc_ref, hgc_ref, hxc_ref, w_ref, c_ref, st_ref, *, tiles_per_seq):
    i = pl.program_id(0)
    u = gc_ref[...] * xc_ref[...]
    tm = u.shape[0]
    uh = hgc_ref[...] * hxc_ref[...]
    uh = jnp.where(i % tiles_per_seq == 0, 0.0, uh)
    row = lax.broadcasted_iota(jnp.int32, u.shape, 0)
    prev1, prev2 = uh[SUBLANES - 1:SUBLANES], uh[SUBLANES - 2:SUBLANES - 1]
    u1 = jnp.where(row == 0, prev1, pltpu.roll(u, 1, 0))
    u2 = jnp.where(row == 0, prev2, jnp.where(row == 1, prev1, pltpu.roll(u, 2, 0)))
    w = w_ref[...]
    conv = w[0:1] * u2 + w[1:2] * u1 + w[2:3] * u
    c_ref[...] = (gb_ref[...] * conv).astype(c_ref.dtype)
    st_ref[...] = u[tm - (CONV_K - 1):tm]


def _conv_prompt(y, conv_w, batch, seq, tm):
    t = y.shape[0]
    tps = seq // tm
    hb = tm // SUBLANES
    kern = functools.partial(_conv_prompt_kernel, tiles_per_seq=tps)
    wide = lambda c: pl.BlockSpec((tm, CONV_WIDTH), lambda i: (i, c))
    halo = lambda c: pl.BlockSpec((SUBLANES, CONV_WIDTH), lambda i: (jnp.maximum(i * hb - 1, 0), c))
    return pl.pallas_call(
        kern,
        grid=(t // tm,),
        in_specs=[wide(3), wide(4), wide(5), halo(4), halo(5),
                  pl.BlockSpec((CONV_K, CONV_WIDTH), lambda i: (0, 0))],
        out_specs=[pl.BlockSpec((tm, CONV_WIDTH), lambda i: (i, 0)),
                   pl.BlockSpec((None, CONV_K - 1, CONV_WIDTH), lambda i: (i // tps, 0, 0))],
        out_shape=[jax.ShapeDtypeStruct((t, CONV_WIDTH), BF16),
                   jax.ShapeDtypeStruct((batch, CONV_K - 1, CONV_WIDTH), F32)],
        compiler_params=_params("arbitrary"),
        name="conv_prompt",
    )(y, y, y, y, y, conv_w)


def _conv_sample_kernel(gb_ref, gc_ref, xc_ref, e1_ref, e2_ref, w_ref, c_ref, u_ref, *, dec_seq):
    u = gc_ref[...] * xc_ref[...]
    pos = lax.broadcasted_iota(jnp.int32, u.shape, 0) & (dec_seq - 1)
    u1 = jnp.where(pos == 0, e1_ref[...], pltpu.roll(u, 1, 0))
    u2 = jnp.where(pos <= 1, e2_ref[...], pltpu.roll(u, 2, 0))
    w = w_ref[...]
    conv = w[0:1] * u2 + w[1:2] * u1 + w[2:3] * u
    c_ref[...] = (gb_ref[...] * conv).astype(c_ref.dtype)
    u_ref[...] = u


def _conv_sample(y, conv_w, prev, dec_seq):
    t = y.shape[0]
    dec_b = t // dec_seq
    zeros = jnp.zeros((dec_b, dec_seq - 1, CONV_WIDTH), F32)
    e1 = jnp.concatenate([prev[:, 1:2], zeros], axis=1).reshape(t, CONV_WIDTH)
    e2 = jnp.concatenate([prev, zeros[:, 1:]], axis=1).reshape(t, CONV_WIDTH)
    kern = functools.partial(_conv_sample_kernel, dec_seq=dec_seq)
    wide = lambda c: pl.BlockSpec((t, CONV_WIDTH), lambda i: (0, c))
    full = pl.BlockSpec((t, CONV_WIDTH), lambda i: (0, 0))
    return pl.pallas_call(
        kern,
        grid=(1,),
        in_specs=[wide(3), wide(4), wide(5), full, full,
                  pl.BlockSpec((CONV_K, CONV_WIDTH), lambda i: (0, 0))],
        out_specs=[full, full],
        out_shape=[jax.ShapeDtypeStruct((t, CONV_WIDTH), BF16),
                   jax.ShapeDtypeStruct((t, CONV_WIDTH), F32)],
        compiler_params=_params("arbitrary"),
        name="conv_sample",
    )(y, y, y, e1, e2, conv_w)


def _outproj_kernel(x_ref, a_ref, c_ref, wa_ref, wc_ref, o_ref):
    o_ref[...] = (x_ref[...] + _dot(a_ref[...].astype(BF16), wa_ref[...])
                  + _dot(c_ref[...].astype(BF16), wc_ref[...]))


def _outproj(x, a, c, w, layer, tm, tn, name):
    t, d = x.shape
    return pl.pallas_call(
        _outproj_kernel,
        grid=(t // tm, d // tn),
        in_specs=[pl.BlockSpec((tm, tn), lambda i, j: (i, j)),
                  pl.BlockSpec((tm, ATTN_WIDTH), lambda i, j: (i, 0)),
                  pl.BlockSpec((tm, CONV_WIDTH), lambda i, j: (i, 0)),
                  pl.BlockSpec((None, ATTN_WIDTH, tn), lambda i, j: (layer, 0, j)),
                  pl.BlockSpec((None, CONV_WIDTH, tn), lambda i, j: (layer, 1, j))],
        out_specs=pl.BlockSpec((tm, tn), lambda i, j: (i, j)),
        out_shape=jax.ShapeDtypeStruct((t, d), F32),
        compiler_params=_params("parallel", "arbitrary"),
        name=name,
    )(x, a, c, w, w)


def _top_with_index(s, k):
    nrows = s.shape[0]
    riota = lax.broadcasted_iota(jnp.int32, s.shape, 0).astype(F32)
    vals, idxs = [], []
    for _ in range(k):
        m = jnp.max(s, axis=0, keepdims=True)
        i = jnp.min(jnp.where(s == m, riota, float(nrows)), axis=0, keepdims=True)
        vals.append(m)
        idxs.append(i)
        s = jnp.where(riota == i, NEG_INF, s)
    return vals, idxs


def _stack_rows(rows, nrows, fill):
    r = lax.broadcasted_iota(jnp.int32, (nrows, rows[0].shape[1]), 0)
    out = jnp.full(r.shape, fill, F32)
    for j, row in enumerate(rows):
        out = jnp.where(r == j, row, out)
    return out


def _candidates(t1, i1, t2, i2):
    k = PEER_TOPK
    t2_all = _stack_rows(t2, k, NEG_INF)
    i2_all = _stack_rows(i2, k, 0.0)
    tile_row = lax.broadcasted_iota(jnp.int32, (SUBLANES, t2_all.shape[1]), 0)
    vals, codes, tail_v, tail_c = [], [], [], []
    for j1 in range(k):
        cnt = k // (j1 + 1)
        base = i1[j1] * float(PEER_NKEYS)
        if cnt > SUBLANES:
            vals.append(t1[j1] + t2_all)
            codes.append(base + i2_all)
        elif cnt >= SUBLANES // 2:
            v = t1[j1] + t2_all[:SUBLANES]
            vals.append(v if cnt == SUBLANES else jnp.where(tile_row < cnt, v, NEG_INF))
            codes.append(base + i2_all[:SUBLANES])
        else:
            tail_v += [t1[j1] + t2[j2] for j2 in range(cnt)]
            tail_c += [base + i2[j2] for j2 in range(cnt)]
    tail_rows = -(-len(tail_v) // SUBLANES) * SUBLANES
    vals.append(_stack_rows(tail_v, tail_rows, NEG_INF))
    codes.append(_stack_rows(tail_c, tail_rows, 0.0))
    return jnp.concatenate(vals, axis=0), jnp.concatenate(codes, axis=0)


def _peer_select_kernel(x_ref, g_ref, hq_ref, sk_ref, xn_ref, gate_ref, at_ref, bt_ref, gt_ref,
                        atm_ref, btm_ref, gtm_ref, m_ref):
    tm = x_ref.shape[0]
    xn_ref[...] = _rms_rows(x_ref[...], g_ref[...]).astype(BF16)

    for h in range(PEER_HEADS):
        c0 = h * 2 * PEER_NKEYS
        q1 = hq_ref[:, c0:c0 + PEER_NKEYS].astype(BF16)
        q2 = hq_ref[:, c0 + PEER_NKEYS:c0 + 2 * PEER_NKEYS].astype(BF16)
        s1 = _dot_nt(sk_ref[h, 0], q1)
        s2 = _dot_nt(sk_ref[h, 1], q2)
        t1, i1 = _top_with_index(s1, PEER_TOPK)
        t2, i2 = _top_with_index(s2, PEER_TOPK)
        cand, code = _candidates(t1, i1, t2, i2)
        sc, rows = _top_with_index(cand, PEER_TOPK)
        riota = lax.broadcasted_iota(jnp.int32, cand.shape, 0).astype(F32)
        ex = [jnp.exp(c - sc[0]) for c in sc]
        z = ex[0]
        for v in ex[1:]:
            z = z + v
        for j in range(PEER_TOPK):
            picked = jnp.sum(jnp.where(riota == rows[j], code, 0.0), axis=0, keepdims=True)
            n1 = jnp.floor(picked * (1.0 / PEER_NKEYS))
            slot = h * PEER_TOPK + j
            at_ref[slot:slot + 1, :] = n1
            bt_ref[slot:slot + 1, :] = picked - n1 * float(PEER_NKEYS)
            gt_ref[slot:slot + 1, :] = ex[j] / z

    atm_ref[...] = at_ref[...].T
    btm_ref[...] = bt_ref[...].T
    gtm_ref[...] = gt_ref[...].T

    niota = lax.broadcasted_iota(jnp.int32, (PEER_NKEYS, LANES), 0).astype(F32)

    def group(gi, carry):
        t0 = pl.multiple_of(gi * GATE_GROUP, GATE_GROUP)
        for tt in range(GATE_GROUP):
            a_row = atm_ref[pl.ds(t0 + tt, 1), :]
            b_row = btm_ref[pl.ds(t0 + tt, 1), :]
            g_row = gtm_ref[pl.ds(t0 + tt, 1), :]
            lhs = jnp.where(niota == a_row, g_row, 0.0).astype(BF16)
            rhs = jnp.where(niota == b_row, 1.0, 0.0).astype(BF16)
            m_ref[tt * GATE_PITCH:tt * GATE_PITCH + PEER_NKEYS, :] = _dot_nt(lhs, rhs)
        for n1 in range(PEER_NKEYS):
            rows_n1 = m_ref[pl.ds(n1, GATE_GROUP, stride=GATE_PITCH), :]
            gate_ref[pl.ds(t0, GATE_GROUP), n1 * PEER_NKEYS:(n1 + 1) * PEER_NKEYS] = rows_n1.astype(BF16)
        return carry

    lax.fori_loop(0, tm // GATE_GROUP, group, 0)


def _peer_select(x, g, hq, subkeys, layer, tm, name):
    t, d = x.shape
    nk = PEER_NKEYS
    slots = PEER_HEADS * PEER_TOPK
    return pl.pallas_call(
        _peer_select_kernel,
        grid=(t // tm,),
        in_specs=[pl.BlockSpec((tm, d), lambda i: (i, 0)),
                  pl.BlockSpec((1, d), lambda i: (0, 0)),
                  pl.BlockSpec((tm, PEER_HEADS * 2 * nk), lambda i: (i, 0)),
                  pl.BlockSpec((None, PEER_HEADS, 2, nk, nk), lambda i: (layer, 0, 0, 0, 0))],
        out_specs=[pl.BlockSpec((tm, d), lambda i: (i, 0)),
                   pl.BlockSpec((tm, PEER_N), lambda i: (i, 0))],
        out_shape=[jax.ShapeDtypeStruct((t, d), BF16),
                   jax.ShapeDtypeStruct((t, PEER_N), BF16)],
        scratch_shapes=[pltpu.VMEM((slots, tm), F32)] * 3 + [pltpu.VMEM((tm, slots), F32)] * 3
                       + [pltpu.VMEM((GATE_GROUP * GATE_PITCH, nk), F32)],
        compiler_params=_params("parallel"),
        name=name,
    )(x, g, hq, subkeys)


def _peer_dense_kernel(xn_ref, gate_ref, u_ref, v_ref, o_ref):
    e = pl.program_id(1)
    a = _dot_nt(xn_ref[...], u_ref[...])
    twice_gelu = a * (1.0 + lax.erf(a * SQRT_HALF))
    weighted = twice_gelu.astype(BF16) * (gate_ref[...] * jnp.asarray(0.5, BF16))

    @pl.when(e == 0)
    def _():
        o_ref[...] = jnp.zeros(o_ref.shape, F32)

    o_ref[...] += _dot(weighted, v_ref[...])


def _peer_dense(xn, gate, u, v, layer, tm, name):
    t, d = xn.shape
    ec = PEER_EC
    return pl.pallas_call(
        _peer_dense_kernel,
        grid=(t // tm, PEER_N // ec),
        in_specs=[pl.BlockSpec((tm, d), lambda i, e: (i, 0)),
                  pl.BlockSpec((tm, ec), lambda i, e: (i, e)),
                  pl.BlockSpec((None, ec, d), lambda i, e: (layer, e, 0)),
                  pl.BlockSpec((None, ec, d), lambda i, e: (layer, e, 0))],
        out_specs=pl.BlockSpec((tm, d), lambda i, e: (i, 0)),
        out_shape=jax.ShapeDtypeStruct((t, d), F32),
        compiler_params=_params("parallel", "arbitrary"),
        name=name,
    )(xn, gate, u, v)


def _ple_kernel(x_ref, peer_ref, p_ref, g_ref, wg_ref, wp_ref, o_ref):
    x = x_ref[...] + peer_ref[...]
    xn = _rms_rows(x, g_ref[...]).astype(BF16)
    z = _dot(xn, wg_ref[...])
    gate = 1.0 / (1.0 + jnp.exp(-z))
    o_ref[...] = x + gate * _dot(p_ref[...].astype(BF16), wp_ref[...])


def _ple(x, peer, p, g, wg, wp, layer, tm, name):
    t, d = x.shape
    return pl.pallas_call(
        _ple_kernel,
        grid=(t // tm,),
        in_specs=[pl.BlockSpec((tm, d), lambda i: (i, 0)),
                  pl.BlockSpec((tm, d), lambda i: (i, 0)),
                  pl.BlockSpec((tm, PLE_DIM), lambda i: (i, 0)),
                  pl.BlockSpec((1, d), lambda i: (0, 0)),
                  pl.BlockSpec((None, d, d), lambda i: (layer, 0, 0)),
                  pl.BlockSpec((None, PLE_DIM, d), lambda i: (layer, 0, 0))],
        out_specs=pl.BlockSpec((tm, d), lambda i: (i, 0)),
        out_shape=jax.ShapeDtypeStruct((t, d), F32),
        compiler_params=_params("parallel"),
        name=name,
    )(x, peer, p, g, wg, wp)


def _lambda_init(layer_idx):
    return 0.8 - 0.6 * math.exp(-0.3 * layer_idx)


def _token_tail(x, a, c, wts, lw, layer, p_l, tm, tag):
    x1 = _outproj(x, a, c, wts["w_out"], layer, tm, D_MODEL, "outproj_" + tag)
    hq = _norm_mm(x1, lw["ffn_norm_g"], wts["peer_wq"], layer, tm, D_MODEL, "peer_query_" + tag)
    xn, gate = _peer_select(x1, lw["ffn_norm_g"], hq, wts["peer_subkeys"], layer, min(tm, SELECT_TM),
                            "peer_select_" + tag)
    peer = _peer_dense(xn, gate, wts["peer_u"], wts["peer_v"], layer, tm, "peer_dense_" + tag)
    return _ple(x1, peer, p_l, lw["ple_norm_g"], wts["ple_gate_w"], wts["ple_proj_w"], layer,
                min(tm, PLE_TM), "ple_" + tag)


def kernel(x_prompt, x_sample, cache_k, cache_v, state_conv, page_table, p_prompt, p_sample,
           attn_norm_g, w_in, q_norm_g, k_norm_g, lam_q1, lam_k1, lam_q2, lam_k2, subln_g,
           conv_w, w_out, ffn_norm_g, peer_wq, peer_subkeys, peer_u, peer_v,
           ple_norm_g, ple_gate_w, ple_proj_w):
    batch, seq, d = x_prompt.shape
    dec_b, dec_seq, _ = x_sample.shape
    depth = w_in.shape[0]
    n_pool = cache_k.shape[1]
    past_len = page_table.shape[1] * cache_k.shape[2]
    tp, ts = batch * seq, dec_b * dec_seq

    rope_p = _rope_tables(jnp.arange(seq))
    rope_s = _rope_tables(past_len + (jnp.arange(ts) % dec_seq))
    ck = cache_k.transpose(0, 1, 3, 4, 5, 2).reshape(depth, n_pool, ATTN_WIDTH, PAGE_SIZE)
    cv = cache_v.reshape(depth, n_pool, PAGE_SIZE * N_HEADS, V_DIM)

    yp = x_prompt.reshape(tp, d)
    ys = x_sample.reshape(ts, d)
    outs = [[] for _ in range(6)]
    row = lambda v: v.reshape(1, -1)
    wts = {
        "w_in": w_in.astype(BF16),
        "w_out": w_out.astype(BF16),
        "peer_wq": peer_wq.astype(BF16),
        "peer_subkeys": peer_subkeys.astype(BF16),
        "peer_u": peer_u.astype(BF16),
        "peer_v": peer_v.astype(BF16),
        "ple_gate_w": ple_gate_w.astype(BF16),
        "ple_proj_w": ple_proj_w.astype(BF16),
    }
    for l in range(depth):
        lam0 = _lambda_init(l)
        lw = {"ffn_norm_g": row(ffn_norm_g[l]), "ple_norm_g": row(ple_norm_g[l])}
        g_in = row(attn_norm_g[l])
        qkg = jnp.stack([jnp.tile(q_norm_g[l], 2), jnp.tile(k_norm_g[l], 2)]).reshape(2, 1, LANES)
        lamv = jnp.pad(jnp.stack([lam_q1[l], lam_k1[l], lam_q2[l], lam_k2[l]]),
                       ((0, 0), (0, LANES - HEAD_DIM)))
        sg = row(subln_g[l])

        y, k_t, v_t = _inproj(yp, g_in, wts["w_in"], l, qkg, rope_p, PROMPT_TM, "inproj_prompt",
                                kv_batch=(batch, seq))
        a = _prompt_attention(y, lamv, sg, lam0, batch, seq)
        c, cstate = _conv_prompt(y, conv_w[l], batch, seq, PROMPT_TM)
        outs[0].append(k_t.reshape(batch, N_HEADS, 2, HEAD_DIM, seq).transpose(0, 4, 1, 2, 3))
        outs[1].append(v_t.reshape(batch, seq, N_HEADS, V_DIM))
        outs[2].append(cstate)
        yp = _token_tail(yp, a, c, wts, lw, l, p_prompt[l].reshape(tp, PLE_DIM), PROMPT_TM, "prompt")

        y = _inproj(ys, g_in, wts["w_in"], l, qkg, rope_s, ts, "inproj_sample")
        a = _sample_attention(y.reshape(dec_b, dec_seq, IN_COLS), ck, cv, page_table, l, lamv, sg, lam0)
        c, u = _conv_sample(y, conv_w[l], state_conv[l], dec_seq)
        outs[3].append(y[:, ATTN_WIDTH:2 * ATTN_WIDTH].reshape(dec_b, dec_seq, N_HEADS, 2, HEAD_DIM))
        outs[4].append(y[:, 2 * ATTN_WIDTH:3 * ATTN_WIDTH].reshape(dec_b, dec_seq, N_HEADS, V_DIM))
        outs[5].append(u.reshape(dec_b, dec_seq, CONV_WIDTH)[:, dec_seq - (CONV_K - 1):])
        ys = _token_tail(ys, a.reshape(ts, ATTN_WIDTH), c, wts, lw, l, p_sample[l].reshape(ts, PLE_DIM), ts,
                         "sample")

    return (yp.reshape(batch, seq, d), ys.reshape(dec_b, dec_seq, d),
            jnp.stack(outs[0]), jnp.stack(outs[1]), jnp.stack(outs[2]),
            jnp.stack(outs[3]), jnp.stack(outs[4]), jnp.stack(outs[5]))
```

```python
import functools
import math

import jax
import jax.numpy as jnp
from jax import lax
from jax.experimental import pallas as pl
from jax.experimental.pallas import tpu as pltpu

F32 = jnp.float32
BF16 = jnp.bfloat16

D_MODEL = 2048
N_HEADS = 8
HEAD_DIM = 64
V_DIM = 128
ATTN_WIDTH = 1024
CONV_WIDTH = 1024
IN_COLS = 3 * ATTN_WIDTH + 3 * CONV_WIDTH
ROT_DIM = 16
ROPE_THETA = 500000.0
CONV_K = 3
PLE_DIM = 256
PEER_HEADS = 8
PEER_NKEYS = 128
PEER_N = PEER_NKEYS * PEER_NKEYS
PEER_TOPK = 16
EPS = 1e-6
PAGE_SIZE = 128

LANES = 128
SUBLANES = 8
VMEM_LIMIT_BYTES = 56 * 1024 * 1024
NEG_INF = float("-inf")
MASK_VALUE = float(jnp.finfo(jnp.float32).min)
SQRT_HALF = 0.7071067811865476

PROMPT_TM = 512
ATTN_TQ = 512
ATTN_RB = 256
PAGES_PER_STEP = 16
PEER_EC = 1024
DENSE_RB = 256
SELECT_TM = 256
GATE_GROUP = 16
GATE_PITCH = PEER_NKEYS + SUBLANES
PLE_TM = 256


def _params(*sem):
    return pltpu.CompilerParams(dimension_semantics=sem, vmem_limit_bytes=VMEM_LIMIT_BYTES)


def _rms_rows(x, g):
    ms = jnp.mean(x * x, axis=-1, keepdims=True)
    return x * lax.rsqrt(ms + EPS) * g


def _dot(a, b):
    return jnp.dot(a, b, preferred_element_type=F32)


def _dot_nt(a, b):
    return lax.dot_general(a, b, (((1,), (1,)), ((), ())), preferred_element_type=F32)


def _norm_mm_kernel(x_ref, g_ref, w_ref, o_ref, hn_ref):
    @pl.when(pl.program_id(1) == 0)
    def _():
        hn_ref[...] = _rms_rows(x_ref[...], g_ref[...]).astype(BF16)

    o_ref[...] = _dot(hn_ref[...], w_ref[...])


def _inproj_kernel(x_ref, g_ref, w_ref, qkg_ref, ra_ref, rp_ref, rm_ref, o_ref, *rest, emit_kv):
    hn_ref = rest[-1]
    j = pl.program_id(1)

    @pl.when(j == 0)
    def _():
        hn_ref[...] = _rms_rows(x_ref[...], g_ref[...]).astype(BF16)

    @pl.when(j == 0)
    def _():
        r = lax.broadcasted_iota(jnp.int32, (LANES, LANES), 0) // HEAD_DIM
        c = lax.broadcasted_iota(jnp.int32, (LANES, LANES), 1) // HEAD_DIM
        group_sum = jnp.where(r == c, 1.0, 0.0).astype(BF16)
        tm = hn_ref.shape[0]
        blk = min(tm, DENSE_RB)
        for b in range(tm // blk):
            rs = slice(b * blk, (b + 1) * blk)
            yb = _dot(hn_ref[rs, :], w_ref[...])
            for h in range(2 * N_HEADS):
                is_key = h >= N_HEADS
                yh = yb[:, h * LANES:(h + 1) * LANES]
                sq = yh * yh
                hi = sq.astype(BF16)
                lo = (sq - hi.astype(F32)).astype(BF16)
                ss = _dot(hi, group_sum) + _dot(lo, group_sum)
                yn = yh * lax.rsqrt(ss * (1.0 / HEAD_DIM) + EPS) * qkg_ref[1 if is_key else 0]
                out = (yn * ra_ref[rs, :]
                       + pltpu.roll(yn, ROT_DIM // 2, 1) * rp_ref[rs, :]
                       + pltpu.roll(yn, LANES - ROT_DIM // 2, 1) * rm_ref[rs, :])
                o_ref[rs, h * LANES:(h + 1) * LANES] = out
                if emit_kv and is_key:
                    hk = h - N_HEADS
                    rest[0][hk * LANES:(hk + 1) * LANES, rs] = out.T

    @pl.when(j > 0)
    def _():
        o_ref[...] = _dot(hn_ref[...], w_ref[...])

    if emit_kv:
        @pl.when(j == 1)
        def _():
            heads = jnp.stack([o_ref[:, h * V_DIM:(h + 1) * V_DIM] for h in range(N_HEADS)], axis=0)
            rest[1][...] = jnp.swapaxes(heads, 0, 1)


def _norm_mm(x, g, w, layer, tm, tn, name):
    t, d = x.shape
    n = w.shape[2]
    return pl.pallas_call(
        _norm_mm_kernel,
        grid=(t // tm, n // tn),
        in_specs=[pl.BlockSpec((tm, d), lambda i, j: (i, 0)),
                  pl.BlockSpec((1, d), lambda i, j: (0, 0)),
                  pl.BlockSpec((None, d, tn), lambda i, j: (layer, 0, j))],
        out_specs=pl.BlockSpec((tm, tn), lambda i, j: (i, j)),
        out_shape=jax.ShapeDtypeStruct((t, n), F32),
        scratch_shapes=[pltpu.VMEM((tm, d), BF16)],
        compiler_params=_params("parallel", "arbitrary"),
        name=name,
    )(x, g, w)


def _inproj(x, g, w, layer, qkg, rope, tm, name, kv_batch=None):
    t, d = x.shape
    tn = 2 * ATTN_WIDTH
    ra, rp, rm = rope
    nrep = ra.shape[0] // tm
    rope_spec = pl.BlockSpec((tm, LANES), lambda i, j: (i % nrep, 0))
    out_specs = [pl.BlockSpec((tm, tn), lambda i, j: (i, j))]
    out_shape = [jax.ShapeDtypeStruct((t, IN_COLS), F32)]
    if kv_batch is not None:
        batch, seq = kv_batch
        tps = seq // tm
        out_specs += [pl.BlockSpec((None, ATTN_WIDTH, tm), lambda i, j: (i // tps, 0, i % tps)),
                      pl.BlockSpec((tm, N_HEADS, V_DIM), lambda i, j: (i, 0, 0))]
        out_shape += [jax.ShapeDtypeStruct((batch, ATTN_WIDTH, seq), F32),
                      jax.ShapeDtypeStruct((t, N_HEADS, V_DIM), F32)]
    res = pl.pallas_call(
        functools.partial(_inproj_kernel, emit_kv=kv_batch is not None),
        grid=(t // tm, IN_COLS // tn),
        in_specs=[pl.BlockSpec((tm, d), lambda i, j: (i, 0)),
                  pl.BlockSpec((1, d), lambda i, j: (0, 0)),
                  pl.BlockSpec((None, d, tn), lambda i, j: (layer, 0, j)),
                  pl.BlockSpec((2, 1, LANES), lambda i, j: (0, 0, 0)),
                  rope_spec, rope_spec, rope_spec],
        out_specs=out_specs,
        out_shape=out_shape,
        scratch_shapes=[pltpu.VMEM((tm, d), BF16)],
        compiler_params=_params("parallel", "arbitrary"),
        name=name,
    )(x, g, w, qkg, ra, rp, rm)
    return res if kv_batch is not None else res[0]


def _rope_tables(pos):
    inv_freq = ROPE_THETA ** (-jnp.arange(0, ROT_DIM, 2, dtype=F32) / ROT_DIM)
    ang = pos.astype(F32)[:, None] * inv_freq[None, :]
    cos, sin = jnp.cos(ang), jnp.sin(ang)
    p = pos.shape[0]
    half = ROT_DIM // 2
    one = jnp.ones((p, HEAD_DIM - ROT_DIM), F32)
    zero = jnp.zeros((p, HEAD_DIM - ROT_DIM), F32)
    zh = jnp.zeros((p, half), F32)
    a = jnp.concatenate([cos, cos, one], axis=-1)
    bp = jnp.concatenate([zh, sin, zero], axis=-1)
    bm = jnp.concatenate([-sin, zh, zero], axis=-1)
    return tuple(jnp.concatenate([t, t], axis=-1) for t in (a, bp, bm))


def _lambda_full(lamv_ref, lam0):
    lv = lamv_ref[...]
    s1 = jnp.sum(lv[0:1] * lv[1:2], axis=-1, keepdims=True)
    s2 = jnp.sum(lv[2:3] * lv[3:4], axis=-1, keepdims=True)
    return jnp.exp(s1) - jnp.exp(s2) + lam0


def _pattn_kernel(q_ref, k_ref, v_ref, lamv_ref, sg_ref, o_ref, kb_ref, vb_ref, m_ref, acc_ref,
                  *, lam0, tq):
    qi = pl.program_id(2)
    seq = k_ref.shape[0]

    @pl.when(qi == 0)
    def _():
        kb_ref[...] = k_ref[...].astype(BF16)
        vb_ref[:, :V_DIM] = v_ref[...].astype(BF16)
        lane = lax.broadcasted_iota(jnp.int32, (seq, LANES), 1)
        vb_ref[:, V_DIM:] = jnp.where(lane == 0, 1.0, 0.0).astype(BF16)

    q = q_ref[...] * (HEAD_DIM ** -0.5)
    lane = lax.broadcasted_iota(jnp.int32, (tq, LANES), 1)
    q0 = jnp.where(lane < HEAD_DIM, q, 0.0).astype(BF16)
    q1 = jnp.where(lane >= HEAD_DIM, q, 0.0).astype(BF16)
    qq = jnp.concatenate([q0, q1], axis=0)

    m_ref[...] = jnp.full(m_ref.shape, NEG_INF, F32)
    acc_ref[...] = jnp.zeros(acc_ref.shape, F32)

    def chunk(kj, masked):
        off = pl.multiple_of(kj * tq, tq)
        ks = kb_ref[pl.ds(off, tq), :]
        vs = vb_ref[pl.ds(off, tq), :]
        for rb in range(2 * tq // ATTN_RB):
            rs = slice(rb * ATTN_RB, (rb + 1) * ATTN_RB)
            s = _dot_nt(qq[rs], ks)
            if masked:
                row = lax.broadcasted_iota(jnp.int32, s.shape, 0) + rb * ATTN_RB
                col = lax.broadcasted_iota(jnp.int32, s.shape, 1)
                s = jnp.where(col <= (row & (tq - 1)), s, MASK_VALUE)
            m_old = m_ref[rs, :]
            m_new = jnp.maximum(m_old, jnp.max(s, axis=-1, keepdims=True))
            alpha = jnp.exp(m_old - m_new)
            p = jnp.exp(s - jnp.concatenate([m_new] * (tq // LANES), axis=1)).astype(BF16)
            acc_ref[rs, :] = jnp.concatenate([alpha, alpha], axis=1) * acc_ref[rs, :] + _dot(p, vs)
            m_ref[rs, :] = m_new

    def body(kj, carry):
        chunk(kj, False)
        return carry

    lax.fori_loop(0, qi, body, 0)
    chunk(qi, True)

    acc = acc_ref[...]
    o = acc[:, :V_DIM] / acc[:, V_DIM:V_DIM + 1]
    lam = _lambda_full(lamv_ref, lam0)
    a = o[:tq] - lam * o[tq:]
    o_ref[...] = (_rms_rows(a, sg_ref[...]) * (1.0 - lam0)).astype(o_ref.dtype)


def _prompt_attention(y, lamv, sg, lam0, batch, seq):
    tq = ATTN_TQ
    nq = seq // tq
    kern = functools.partial(_pattn_kernel, lam0=lam0, tq=tq)
    return pl.pallas_call(
        kern,
        grid=(batch, N_HEADS, nq),
        in_specs=[pl.BlockSpec((tq, LANES), lambda b, h, i: (b * nq + i, h)),
                  pl.BlockSpec((seq, LANES), lambda b, h, i: (b, N_HEADS + h)),
                  pl.BlockSpec((seq, LANES), lambda b, h, i: (b, 2 * N_HEADS + h)),
                  pl.BlockSpec((4, LANES), lambda b, h, i: (0, 0)),
                  pl.BlockSpec((1, LANES), lambda b, h, i: (0, 0))],
        out_specs=pl.BlockSpec((tq, LANES), lambda b, h, i: (b * nq + i, h)),
        out_shape=jax.ShapeDtypeStruct((batch * seq, ATTN_WIDTH), BF16),
        scratch_shapes=[pltpu.VMEM((seq, LANES), BF16),
                        pltpu.VMEM((seq, 2 * LANES), BF16),
                        pltpu.VMEM((2 * tq, LANES), F32),
                        pltpu.VMEM((2 * tq, 2 * LANES), F32)],
        compiler_params=_params("parallel", "parallel", "arbitrary"),
        name="prompt_attention",
    )(y, y, y, lamv, sg)


def _sattn_kernel(pt_ref, q_ref, kn_ref, vn_ref, lamv_ref, sg_ref, *rest, lam0, npg):
    k_refs = rest[:npg]
    v_refs = rest[npg:2 * npg]
    o_ref, qt_ref, knp_ref, vnp_ref, m_ref, l_ref, acc_ref = rest[2 * npg:]
    g = pl.program_id(1)
    dec_seq = q_ref.shape[0]
    rows = 2 * dec_seq
    nrow = N_HEADS * rows

    @pl.when(g == 0)
    def _():
        q = q_ref[...] * (HEAD_DIM ** -0.5)
        row = lax.broadcasted_iota(jnp.int32, (nrow, ATTN_WIDTH), 0)
        lane = lax.broadcasted_iota(jnp.int32, (nrow, ATTN_WIDTH), 1)
        qq = jnp.zeros((nrow, ATTN_WIDTH), F32)
        for i in range(dec_seq):
            qq = jnp.where((row & (dec_seq - 1)) == i, q[i:i + 1, :], qq)
        qt_ref[...] = jnp.where(lane // HEAD_DIM == row // dec_seq, qq, 0.0).astype(BF16)
        knp_ref[...] = jnp.zeros(knp_ref.shape, F32)
        vnp_ref[...] = jnp.zeros(vnp_ref.shape, F32)
        m_ref[...] = jnp.full(m_ref.shape, NEG_INF, F32)
        l_ref[...] = jnp.zeros(l_ref.shape, F32)
        acc_ref[...] = jnp.zeros(acc_ref.shape, F32)

    def attend(st, weighted_values):
        m_old = m_ref[...]
        m_new = jnp.maximum(m_old, jnp.max(st, axis=-1, keepdims=True))
        alpha = jnp.exp(m_old - m_new)
        p = jnp.exp(st - m_new)
        l_ref[...] = alpha * l_ref[...] + jnp.sum(p, axis=-1, keepdims=True)
        acc_ref[...] = alpha * acc_ref[...] + weighted_values(p)
        m_ref[...] = m_new

    kc = jnp.concatenate([r[...].astype(BF16) for r in k_refs], axis=1)
    st = _dot(qt_ref[...], kc)

    def page_values(p):
        outs = []
        for h in range(N_HEADS):
            vh = jnp.concatenate(
                [r[pl.ds(h, PAGE_SIZE, stride=N_HEADS), :].astype(BF16) for r in v_refs], axis=0)
            outs.append(_dot(p[h * rows:(h + 1) * rows, :].astype(BF16), vh))
        return jnp.concatenate(outs, axis=0)

    attend(st, page_values)

    @pl.when(g == pl.num_programs(1) - 1)
    def _():
        knp_ref[0:dec_seq, :] = kn_ref[...]
        vnp_ref[0:dec_seq, :] = vn_ref[...]
        sn = _dot_nt(qt_ref[...], knp_ref[...].astype(BF16))
        row = lax.broadcasted_iota(jnp.int32, sn.shape, 0)
        col = lax.broadcasted_iota(jnp.int32, sn.shape, 1)
        sn = jnp.where(col <= (row & (dec_seq - 1)), sn, MASK_VALUE)

        def new_values(p):
            o = _dot(p.astype(BF16), vnp_ref[...].astype(BF16))
            return jnp.concatenate(
                [o[h * rows:(h + 1) * rows, h * V_DIM:(h + 1) * V_DIM] for h in range(N_HEADS)], axis=0)

        attend(sn, new_values)

        o = acc_ref[...] / l_ref[...]
        lam = _lambda_full(lamv_ref, lam0)
        a = o - lam * pltpu.roll(o, nrow - dec_seq, 0)
        a = _rms_rows(a, sg_ref[...]) * (1.0 - lam0)
        for h in range(N_HEADS):
            o_ref[:, h * V_DIM:(h + 1) * V_DIM] = a[h * rows:h * rows + dec_seq, :]


def _sample_attention(y3, cache_kt, cache_vr, page_table, layer, lamv, sg, lam0):
    dec_b, dec_seq, _ = y3.shape
    n_pages = page_table.shape[1]
    npg = PAGES_PER_STEP
    assert n_pages % npg == 0 and dec_seq & (dec_seq - 1) == 0
    nrow = N_HEADS * 2 * dec_seq
    pt = page_table.reshape(-1)

    def page_spec(p):
        return pl.BlockSpec((None, None, ATTN_WIDTH, PAGE_SIZE),
                            lambda b, g, pt_ref: (layer, pt_ref[b * n_pages + g * npg + p], 0, 0))

    def col_spec(c):
        return pl.BlockSpec((None, dec_seq, ATTN_WIDTH), lambda b, g, pt_ref: (b, 0, c))

    grid_spec = pltpu.PrefetchScalarGridSpec(
        num_scalar_prefetch=1,
        grid=(dec_b, n_pages // npg),
        in_specs=[col_spec(0), col_spec(1), col_spec(2),
                  pl.BlockSpec((4, LANES), lambda b, g, pt_ref: (0, 0)),
                  pl.BlockSpec((1, LANES), lambda b, g, pt_ref: (0, 0))]
                 + [page_spec(p) for p in range(npg)] + [page_spec(p) for p in range(npg)],
        out_specs=pl.BlockSpec((None, dec_seq, ATTN_WIDTH), lambda b, g, pt_ref: (b, 0, 0)),
        scratch_shapes=[pltpu.VMEM((nrow, ATTN_WIDTH), BF16),
                        pltpu.VMEM((PAGE_SIZE, ATTN_WIDTH), F32),
                        pltpu.VMEM((PAGE_SIZE, ATTN_WIDTH), F32),
                        pltpu.VMEM((nrow, 1), F32),
                        pltpu.VMEM((nrow, 1), F32),
                        pltpu.VMEM((nrow, V_DIM), F32)],
    )
    kern = functools.partial(_sattn_kernel, lam0=lam0, npg=npg)
    return pl.pallas_call(
        kern,
        grid_spec=grid_spec,
        out_shape=jax.ShapeDtypeStruct((dec_b, dec_seq, ATTN_WIDTH), F32),
        compiler_params=_params("parallel", "arbitrary"),
        name="sample_attention",
    )(pt, y3, y3, y3, lamv, sg, *([cache_kt] * npg), *([cache_vr] * npg))


def _conv_prompt_kernel(gb_ref, gc_ref, xc_ref, hgc_ref, hxc_ref, w_ref, c_ref, st_ref, *, tiles_per_seq):
    i = pl.program_id(0)
    u = gc_ref[...] * xc_ref[...]
    tm = u.shape[0]
    uh = hgc_ref[...] * hxc_ref[...]
    uh = jnp.where(i % tiles_per_seq == 0, 0.0, uh)
    row = lax.broadcasted_iota(jnp.int32, u.shape, 0)
    prev1, prev2 = uh[SUBLANES - 1:SUBLANES], uh[SUBLANES - 2:SUBLANES - 1]
    u1 = jnp.where(row == 0, prev1, pltpu.roll(u, 1, 0))
    u2 = jnp.where(row == 0, prev2, jnp.where(row == 1, prev1, pltpu.roll(u, 2, 0)))
    w = w_ref[...]
    conv = w[0:1] * u2 + w[1:2] * u1 + w[2:3] * u
    c_ref[...] = (gb_ref[...] * conv).astype(c_ref.dtype)
    st_ref[...] = u[tm - (CONV_K - 1):tm]


def _conv_prompt(y, conv_w, batch, seq, tm):
    t = y.shape[0]
    tps = seq // tm
    hb = tm // SUBLANES
    kern = functools.partial(_conv_prompt_kernel, tiles_per_seq=tps)
    wide = lambda c: pl.BlockSpec((tm, CONV_WIDTH), lambda i: (i, c))
    halo = lambda c: pl.BlockSpec((SUBLANES, CONV_WIDTH), lambda i: (jnp.maximum(i * hb - 1, 0), c))
    return pl.pallas_call(
        kern,
        grid=(t // tm,),
        in_specs=[wide(3), wide(4), wide(5), halo(4), halo(5),
                  pl.BlockSpec((CONV_K, CONV_WIDTH), lambda i: (0, 0))],
        out_specs=[pl.BlockSpec((tm, CONV_WIDTH), lambda i: (i, 0)),
                   pl.BlockSpec((None, CONV_K - 1, CONV_WIDTH), lambda i: (i // tps, 0, 0))],
        out_shape=[jax.ShapeDtypeStruct((t, CONV_WIDTH), BF16),
                   jax.ShapeDtypeStruct((batch, CONV_K - 1, CONV_WIDTH), F32)],
        compiler_params=_params("arbitrary"),
        name="conv_prompt",
    )(y, y, y, y, y, conv_w)


def _conv_sample_kernel(gb_ref, gc_ref, xc_ref, e1_ref, e2_ref, w_ref, c_ref, u_ref, *, dec_seq):
    u = gc_ref[...] * xc_ref[...]
    pos = lax.broadcasted_iota(jnp.int32, u.shape, 0) & (dec_seq - 1)
    u1 = jnp.where(pos == 0, e1_ref[...], pltpu.roll(u, 1, 0))
    u2 = jnp.where(pos <= 1, e2_ref[...], pltpu.roll(u, 2, 0))
    w = w_ref[...]
    conv = w[0:1] * u2 + w[1:2] * u1 + w[2:3] * u
    c_ref[...] = (gb_ref[...] * conv).astype(c_ref.dtype)
    u_ref[...] = u


def _conv_sample(y, conv_w, prev, dec_seq):
    t = y.shape[0]
    dec_b = t // dec_seq
    zeros = jnp.zeros((dec_b, dec_seq - 1, CONV_WIDTH), F32)
    e1 = jnp.concatenate([prev[:, 1:2], zeros], axis=1).reshape(t, CONV_WIDTH)
    e2 = jnp.concatenate([prev, zeros[:, 1:]], axis=1).reshape(t, CONV_WIDTH)
    kern = functools.partial(_conv_sample_kernel, dec_seq=dec_seq)
    wide = lambda c: pl.BlockSpec((t, CONV_WIDTH), lambda i: (0, c))
    full = pl.BlockSpec((t, CONV_WIDTH), lambda i: (0, 0))
    return pl.pallas_call(
        kern,
        grid=(1,),
        in_specs=[wide(3), wide(4), wide(5), full, full,
                  pl.BlockSpec((CONV_K, CONV_WIDTH), lambda i: (0, 0))],
        out_specs=[full, full],
        out_shape=[jax.ShapeDtypeStruct((t, CONV_WIDTH), BF16),
                   jax.ShapeDtypeStruct((t, CONV_WIDTH), F32)],
        compiler_params=_params("arbitrary"),
        name="conv_sample",
    )(y, y, y, e1, e2, conv_w)


def _outproj_kernel(x_ref, a_ref, c_ref, wa_ref, wc_ref, o_ref):
    o_ref[...] = (x_ref[...] + _dot(a_ref[...].astype(BF16), wa_ref[...])
                  + _dot(c_ref[...].astype(BF16), wc_ref[...]))


def _outproj(x, a, c, w, layer, tm, tn, name):
    t, d = x.shape
    return pl.pallas_call(
        _outproj_kernel,
        grid=(t // tm, d // tn),
        in_specs=[pl.BlockSpec((tm, tn), lambda i, j: (i, j)),
                  pl.BlockSpec((tm, ATTN_WIDTH), lambda i, j: (i, 0)),
                  pl.BlockSpec((tm, CONV_WIDTH), lambda i, j: (i, 0)),
                  pl.BlockSpec((None, ATTN_WIDTH, tn), lambda i, j: (layer, 0, j)),
                  pl.BlockSpec((None, CONV_WIDTH, tn), lambda i, j: (layer, 1, j))],
        out_specs=pl.BlockSpec((tm, tn), lambda i, j: (i, j)),
        out_shape=jax.ShapeDtypeStruct((t, d), F32),
        compiler_params=_params("parallel", "arbitrary"),
        name=name,
    )(x, a, c, w, w)


def _top_with_index(s, k):
    nrows = s.shape[0]
    riota = lax.broadcasted_iota(jnp.int32, s.shape, 0).astype(F32)
    vals, idxs = [], []
    for _ in range(k):
        m = jnp.max(s, axis=0, keepdims=True)
        i = jnp.min(jnp.where(s == m, riota, float(nrows)), axis=0, keepdims=True)
        vals.append(m)
        idxs.append(i)
        s = jnp.where(riota == i, NEG_INF, s)
    return vals, idxs


def _stack_rows(rows, nrows, fill):
    r = lax.broadcasted_iota(jnp.int32, (nrows, rows[0].shape[1]), 0)
    out = jnp.full(r.shape, fill, F32)
    for j, row in enumerate(rows):
        out = jnp.where(r == j, row, out)
    return out


def _candidates(t1, i1, t2, i2):
    k = PEER_TOPK
    t2_all = _stack_rows(t2, k, NEG_INF)
    i2_all = _stack_rows(i2, k, 0.0)
    tile_row = lax.broadcasted_iota(jnp.int32, (SUBLANES, t2_all.shape[1]), 0)
    vals, codes, tail_v, tail_c = [], [], [], []
    for j1 in range(k):
        cnt = k // (j1 + 1)
        base = i1[j1] * float(PEER_NKEYS)
        if cnt > SUBLANES:
            vals.append(t1[j1] + t2_all)
            codes.append(base + i2_all)
        elif cnt >= SUBLANES // 2:
            v = t1[j1] + t2_all[:SUBLANES]
            vals.append(v if cnt == SUBLANES else jnp.where(tile_row < cnt, v, NEG_INF))
            codes.append(base + i2_all[:SUBLANES])
        else:
            tail_v += [t1[j1] + t2[j2] for j2 in range(cnt)]
            tail_c += [base + i2[j2] for j2 in range(cnt)]
    tail_rows = -(-len(tail_v) // SUBLANES) * SUBLANES
    vals.append(_stack_rows(tail_v, tail_rows, NEG_INF))
    codes.append(_stack_rows(tail_c, tail_rows, 0.0))
    return jnp.concatenate(vals, axis=0), jnp.concatenate(codes, axis=0)


def _peer_select_kernel(x_ref, g_ref, hq_ref, sk_ref, xn_ref, gate_ref, at_ref, bt_ref, gt_ref,
                        atm_ref, btm_ref, gtm_ref, m_ref):
    tm = x_ref.shape[0]
    xn_ref[...] = _rms_rows(x_ref[...], g_ref[...]).astype(BF16)

    for h in range(PEER_HEADS):
        c0 = h * 2 * PEER_NKEYS
        q1 = hq_ref[:, c0:c0 + PEER_NKEYS].astype(BF16)
        q2 = hq_ref[:, c0 + PEER_NKEYS:c0 + 2 * PEER_NKEYS].astype(BF16)
        s1 = _dot_nt(sk_ref[h, 0], q1)
        s2 = _dot_nt(sk_ref[h, 1], q2)
        t1, i1 = _top_with_index(s1, PEER_TOPK)
        t2, i2 = _top_with_index(s2, PEER_TOPK)
        cand, code = _candidates(t1, i1, t2, i2)
        sc, rows = _top_with_index(cand, PEER_TOPK)
        riota = lax.broadcasted_iota(jnp.int32, cand.shape, 0).astype(F32)
        ex = [jnp.exp(c - sc[0]) for c in sc]
        z = ex[0]
        for v in ex[1:]:
            z = z + v
        for j in range(PEER_TOPK):
            picked = jnp.sum(jnp.where(riota == rows[j], code, 0.0), axis=0, keepdims=True)
            n1 = jnp.floor(picked * (1.0 / PEER_NKEYS))
            slot = h * PEER_TOPK + j
            at_ref[slot:slot + 1, :] = n1
            bt_ref[slot:slot + 1, :] = picked - n1 * float(PEER_NKEYS)
            gt_ref[slot:slot + 1, :] = ex[j] / z

    atm_ref[...] = at_ref[...].T
    btm_ref[...] = bt_ref[...].T
    gtm_ref[...] = gt_ref[...].T

    niota = lax.broadcasted_iota(jnp.int32, (PEER_NKEYS, LANES), 0).astype(F32)

    def group(gi, carry):
        t0 = pl.multiple_of(gi * GATE_GROUP, GATE_GROUP)
        for tt in range(GATE_GROUP):
            a_row = atm_ref[pl.ds(t0 + tt, 1), :]
            b_row = btm_ref[pl.ds(t0 + tt, 1), :]
            g_row = gtm_ref[pl.ds(t0 + tt, 1), :]
            lhs = jnp.where(niota == a_row, g_row, 0.0).astype(BF16)
            rhs = jnp.where(niota == b_row, 1.0, 0.0).astype(BF16)
            m_ref[tt * GATE_PITCH:tt * GATE_PITCH + PEER_NKEYS, :] = _dot_nt(lhs, rhs)
        for n1 in range(PEER_NKEYS):
            rows_n1 = m_ref[pl.ds(n1, GATE_GROUP, stride=GATE_PITCH), :]
            gate_ref[pl.ds(t0, GATE_GROUP), n1 * PEER_NKEYS:(n1 + 1) * PEER_NKEYS] = rows_n1.astype(BF16)
        return carry

    lax.fori_loop(0, tm // GATE_GROUP, group, 0)


def _peer_select(x, g, hq, subkeys, layer, tm, name):
    t, d = x.shape
    nk = PEER_NKEYS
    slots = PEER_HEADS * PEER_TOPK
    return pl.pallas_call(
        _peer_select_kernel,
        grid=(t // tm,),
        in_specs=[pl.BlockSpec((tm, d), lambda i: (i, 0)),
                  pl.BlockSpec((1, d), lambda i: (0, 0)),
                  pl.BlockSpec((tm, PEER_HEADS * 2 * nk), lambda i: (i, 0)),
                  pl.BlockSpec((None, PEER_HEADS, 2, nk, nk), lambda i: (layer, 0, 0, 0, 0))],
        out_specs=[pl.BlockSpec((tm, d), lambda i: (i, 0)),
                   pl.BlockSpec((tm, PEER_N), lambda i: (i, 0))],
        out_shape=[jax.ShapeDtypeStruct((t, d), BF16),
                   jax.ShapeDtypeStruct((t, PEER_N), BF16)],
        scratch_shapes=[pltpu.VMEM((slots, tm), F32)] * 3 + [pltpu.VMEM((tm, slots), F32)] * 3
                       + [pltpu.VMEM((GATE_GROUP * GATE_PITCH, nk), F32)],
        compiler_params=_params("parallel"),
        name=name,
    )(x, g, hq, subkeys)


def _peer_dense_kernel(xn_ref, gate_ref, u_ref, v_ref, o_ref):
    e = pl.program_id(1)
    tm = xn_ref.shape[0]

    @pl.when(e == 0)
    def _():
        o_ref[...] = jnp.zeros(o_ref.shape, F32)

    blk = min(tm, DENSE_RB)
    for k in range(tm // blk):
        rs = slice(k * blk, (k + 1) * blk)
        a = _dot_nt(xn_ref[rs, :], u_ref[...])
        twice_gelu = a * (1.0 + lax.erf(a * SQRT_HALF))
        weighted = twice_gelu.astype(BF16) * (gate_ref[rs, :] * jnp.asarray(0.5, BF16))
        o_ref[rs, :] += _dot(weighted, v_ref[...])


def _peer_dense(xn, gate, u, v, layer, tm, name):
    t, d = xn.shape
    ec = PEER_EC
    return pl.pallas_call(
        _peer_dense_kernel,
        grid=(t // tm, PEER_N // ec),
        in_specs=[pl.BlockSpec((tm, d), lambda i, e: (i, 0)),
                  pl.BlockSpec((tm, ec), lambda i, e: (i, e)),
                  pl.BlockSpec((None, ec, d), lambda i, e: (layer, e, 0)),
                  pl.BlockSpec((None, ec, d), lambda i, e: (layer, e, 0))],
        out_specs=pl.BlockSpec((tm, d), lambda i, e: (i, 0)),
        out_shape=jax.ShapeDtypeStruct((t, d), F32),
        compiler_params=_params("parallel", "arbitrary"),
        name=name,
    )(xn, gate, u, v)


def _ple_kernel(x_ref, peer_ref, p_ref, g_ref, wg_ref, wp_ref, o_ref):
    x = x_ref[...] + peer_ref[...]
    xn = _rms_rows(x, g_ref[...]).astype(BF16)
    z = _dot(xn, wg_ref[...])
    gate = 1.0 / (1.0 + jnp.exp(-z))
    o_ref[...] = x + gate * _dot(p_ref[...].astype(BF16), wp_ref[...])


def _ple(x, peer, p, g, wg, wp, layer, tm, name):
    t, d = x.shape
    return pl.pallas_call(
        _ple_kernel,
        grid=(t // tm,),
        in_specs=[pl.BlockSpec((tm, d), lambda i: (i, 0)),
                  pl.BlockSpec((tm, d), lambda i: (i, 0)),
                  pl.BlockSpec((tm, PLE_DIM), lambda i: (i, 0)),
                  pl.BlockSpec((1, d), lambda i: (0, 0)),
                  pl.BlockSpec((None, d, d), lambda i: (layer, 0, 0)),
                  pl.BlockSpec((None, PLE_DIM, d), lambda i: (layer, 0, 0))],
        out_specs=pl.BlockSpec((tm, d), lambda i: (i, 0)),
        out_shape=jax.ShapeDtypeStruct((t, d), F32),
        compiler_params=_params("parallel"),
        name=name,
    )(x, peer, p, g, wg, wp)


def _lambda_init(layer_idx):
    return 0.8 - 0.6 * math.exp(-0.3 * layer_idx)


def _token_tail(x, a, c, wts, lw, layer, p_l, tm, tag):
    x1 = _outproj(x, a, c, wts["w_out"], layer, tm, D_MODEL, "outproj_" + tag)
    hq = _norm_mm(x1, lw["ffn_norm_g"], wts["peer_wq"], layer, tm, D_MODEL, "peer_query_" + tag)
    xn, gate = _peer_select(x1, lw["ffn_norm_g"], hq, wts["peer_subkeys"], layer, min(tm, SELECT_TM),
                            "peer_select_" + tag)
    peer = _peer_dense(xn, gate, wts["peer_u"], wts["peer_v"], layer, tm, "peer_dense_" + tag)
    return _ple(x1, peer, p_l, lw["ple_norm_g"], wts["ple_gate_w"], wts["ple_proj_w"], layer,
                min(tm, PLE_TM), "ple_" + tag)


def kernel(x_prompt, x_sample, cache_k, cache_v, state_conv, page_table, p_prompt, p_sample,
           attn_norm_g, w_in, q_norm_g, k_norm_g, lam_q1, lam_k1, lam_q2, lam_k2, subln_g,
           conv_w, w_out, ffn_norm_g, peer_wq, peer_subkeys, peer_u, peer_v,
           ple_norm_g, ple_gate_w, ple_proj_w):
    batch, seq, d = x_prompt.shape
    dec_b, dec_seq, _ = x_sample.shape
    depth = w_in.shape[0]
    n_pool = cache_k.shape[1]
    past_len = page_table.shape[1] * cache_k.shape[2]
    tp, ts = batch * seq, dec_b * dec_seq

    rope_p = _rope_tables(jnp.arange(seq))
    rope_s = _rope_tables(past_len + (jnp.arange(ts) % dec_seq))
    ck = cache_k.transpose(0, 1, 3, 4, 5, 2).reshape(depth, n_pool, ATTN_WIDTH, PAGE_SIZE)
    cv = cache_v.reshape(depth, n_pool, PAGE_SIZE * N_HEADS, V_DIM)

    yp = x_prompt.reshape(tp, d)
    ys = x_sample.reshape(ts, d)
    outs = [[] for _ in range(6)]
    row = lambda v: v.reshape(1, -1)
    wts = {
        "w_in": w_in.astype(BF16),
        "w_out": w_out.astype(BF16),
        "peer_wq": peer_wq.astype(BF16),
        "peer_subkeys": peer_subkeys.astype(BF16),
        "peer_u": peer_u.astype(BF16),
        "peer_v": peer_v.astype(BF16),
        "ple_gate_w": ple_gate_w.astype(BF16),
        "ple_proj_w": ple_proj_w.astype(BF16),
    }
    for l in range(depth):
        lam0 = _lambda_init(l)
        lw = {"ffn_norm_g": row(ffn_norm_g[l]), "ple_norm_g": row(ple_norm_g[l])}
        g_in = row(attn_norm_g[l])
        qkg = jnp.stack([jnp.tile(q_norm_g[l], 2), jnp.tile(k_norm_g[l], 2)]).reshape(2, 1, LANES)
        lamv = jnp.pad(jnp.stack([lam_q1[l], lam_k1[l], lam_q2[l], lam_k2[l]]),
                       ((0, 0), (0, LANES - HEAD_DIM)))
        sg = row(subln_g[l])

        y, k_t, v_t = _inproj(yp, g_in, wts["w_in"], l, qkg, rope_p, PROMPT_TM, "inproj_prompt",
                                kv_batch=(batch, seq))
        a = _prompt_attention(y, lamv, sg, lam0, batch, seq)
        c, cstate = _conv_prompt(y, conv_w[l], batch, seq, PROMPT_TM)
        outs[0].append(k_t.reshape(batch, N_HEADS, 2, HEAD_DIM, seq).transpose(0, 4, 1, 2, 3))
        outs[1].append(v_t.reshape(batch, seq, N_HEADS, V_DIM))
        outs[2].append(cstate)
        yp = _token_tail(yp, a, c, wts, lw, l, p_prompt[l].reshape(tp, PLE_DIM), PROMPT_TM, "prompt")

        y = _inproj(ys, g_in, wts["w_in"], l, qkg, rope_s, ts, "inproj_sample")
        a = _sample_attention(y.reshape(dec_b, dec_seq, IN_COLS), ck, cv, page_table, l, lamv, sg, lam0)
        c, u = _conv_sample(y, conv_w[l], state_conv[l], dec_seq)
        outs[3].append(y[:, ATTN_WIDTH:2 * ATTN_WIDTH].reshape(dec_b, dec_seq, N_HEADS, 2, HEAD_DIM))
        outs[4].append(y[:, 2 * ATTN_WIDTH:3 * ATTN_WIDTH].reshape(dec_b, dec_seq, N_HEADS, V_DIM))
        outs[5].append(u.reshape(dec_b, dec_seq, CONV_WIDTH)[:, dec_seq - (CONV_K - 1):])
        ys = _token_tail(ys, a.reshape(ts, ATTN_WIDTH), c, wts, lw, l, p_sample[l].reshape(ts, PLE_DIM), ts,
                         "sample")

    return (yp.reshape(batch, seq, d), ys.reshape(dec_b, dec_seq, d),
            jnp.stack(outs[0]), jnp.stack(outs[1]), jnp.stack(outs[2]),
            jnp.stack(outs[3]), jnp.stack(outs[4]), jnp.stack(outs[5]))
```

```python
import functools
import math

import jax
import jax.numpy as jnp
from jax import lax
from jax.experimental import pallas as pl
from jax.experimental.pallas import tpu as pltpu

F32 = jnp.float32
BF16 = jnp.bfloat16

D_MODEL = 2048
N_HEADS = 8
HEAD_DIM = 64
V_DIM = 128
ATTN_WIDTH = 1024
CONV_WIDTH = 1024
IN_COLS = 3 * ATTN_WIDTH + 3 * CONV_WIDTH
ROT_DIM = 16
ROPE_THETA = 500000.0
CONV_K = 3
PLE_DIM = 256
PEER_HEADS = 8
PEER_NKEYS = 128
PEER_N = PEER_NKEYS * PEER_NKEYS
PEER_TOPK = 16
EPS = 1e-6
PAGE_SIZE = 128

LANES = 128
SUBLANES = 8
VMEM_LIMIT_BYTES = 56 * 1024 * 1024
NEG_INF = float("-inf")
MASK_VALUE = float(jnp.finfo(jnp.float32).min)
SQRT_HALF = 0.7071067811865476

PROMPT_TM = 512
ATTN_TQ = 1024
ATTN_RB = 256
PAGES_PER_STEP = 16
PEER_EC = 1024
DENSE_RB = 256
SELECT_TM = 256
GATE_GROUP = 16
GATE_PITCH = PEER_NKEYS + SUBLANES
PLE_TM = 256


def _params(*sem):
    return pltpu.CompilerParams(dimension_semantics=sem, vmem_limit_bytes=VMEM_LIMIT_BYTES)


def _rms_rows(x, g):
    ms = jnp.mean(x * x, axis=-1, keepdims=True)
    return x * lax.rsqrt(ms + EPS) * g


def _dot(a, b):
    return jnp.dot(a, b, preferred_element_type=F32)


def _dot_nt(a, b):
    return lax.dot_general(a, b, (((1,), (1,)), ((), ())), preferred_element_type=F32)


def _norm_mm_kernel(x_ref, g_ref, w_ref, o_ref, hn_ref):
    @pl.when(pl.program_id(1) == 0)
    def _():
        hn_ref[...] = _rms_rows(x_ref[...], g_ref[...]).astype(BF16)

    o_ref[...] = _dot(hn_ref[...], w_ref[...])


def _inproj_kernel(x_ref, g_ref, w_ref, qkg_ref, ra_ref, rp_ref, rm_ref, o_ref, *rest, emit_kv):
    hn_ref = rest[-1]
    j = pl.program_id(1)

    @pl.when(j == 0)
    def _():
        hn_ref[...] = _rms_rows(x_ref[...], g_ref[...]).astype(BF16)

    @pl.when(j == 0)
    def _():
        r = lax.broadcasted_iota(jnp.int32, (LANES, LANES), 0) // HEAD_DIM
        c = lax.broadcasted_iota(jnp.int32, (LANES, LANES), 1) // HEAD_DIM
        group_sum = jnp.where(r == c, 1.0, 0.0).astype(BF16)
        tm = hn_ref.shape[0]
        blk = min(tm, DENSE_RB)
        for b in range(tm // blk):
            rs = slice(b * blk, (b + 1) * blk)
            yb = _dot(hn_ref[rs, :], w_ref[...])
            for h in range(2 * N_HEADS):
                is_key = h >= N_HEADS
                yh = yb[:, h * LANES:(h + 1) * LANES]
                sq = yh * yh
                hi = sq.astype(BF16)
                lo = (sq - hi.astype(F32)).astype(BF16)
                ss = _dot(hi, group_sum) + _dot(lo, group_sum)
                yn = yh * lax.rsqrt(ss * (1.0 / HEAD_DIM) + EPS) * qkg_ref[1 if is_key else 0]
                out = (yn * ra_ref[rs, :]
                       + pltpu.roll(yn, ROT_DIM // 2, 1) * rp_ref[rs, :]
                       + pltpu.roll(yn, LANES - ROT_DIM // 2, 1) * rm_ref[rs, :])
                o_ref[rs, h * LANES:(h + 1) * LANES] = out
                if emit_kv and is_key:
                    hk = h - N_HEADS
                    rest[0][hk * LANES:(hk + 1) * LANES, rs] = out.T

    @pl.when(j > 0)
    def _():
        o_ref[...] = _dot(hn_ref[...], w_ref[...])

    if emit_kv:
        @pl.when(j == 1)
        def _():
            heads = jnp.stack([o_ref[:, h * V_DIM:(h + 1) * V_DIM] for h in range(N_HEADS)], axis=0)
            rest[1][...] = jnp.swapaxes(heads, 0, 1)


def _norm_mm(x, g, w, layer, tm, tn, name):
    t, d = x.shape
    n = w.shape[2]
    return pl.pallas_call(
        _norm_mm_kernel,
        grid=(t // tm, n // tn),
        in_specs=[pl.BlockSpec((tm, d), lambda i, j: (i, 0)),
                  pl.BlockSpec((1, d), lambda i, j: (0, 0)),
                  pl.BlockSpec((None, d, tn), lambda i, j: (layer, 0, j))],
        out_specs=pl.BlockSpec((tm, tn), lambda i, j: (i, j)),
        out_shape=jax.ShapeDtypeStruct((t, n), F32),
        scratch_shapes=[pltpu.VMEM((tm, d), BF16)],
        compiler_params=_params("parallel", "arbitrary"),
        name=name,
    )(x, g, w)


def _inproj(x, g, w, layer, qkg, rope, tm, name, kv_batch=None):
    t, d = x.shape
    tn = 2 * ATTN_WIDTH
    ra, rp, rm = rope
    nrep = ra.shape[0] // tm
    rope_spec = pl.BlockSpec((tm, LANES), lambda i, j: (i % nrep, 0))
    out_specs = [pl.BlockSpec((tm, tn), lambda i, j: (i, j))]
    out_shape = [jax.ShapeDtypeStruct((t, IN_COLS), F32)]
    if kv_batch is not None:
        batch, seq = kv_batch
        tps = seq // tm
        out_specs += [pl.BlockSpec((None, ATTN_WIDTH, tm), lambda i, j: (i // tps, 0, i % tps)),
                      pl.BlockSpec((tm, N_HEADS, V_DIM), lambda i, j: (i, 0, 0))]
        out_shape += [jax.ShapeDtypeStruct((batch, ATTN_WIDTH, seq), F32),
                      jax.ShapeDtypeStruct((t, N_HEADS, V_DIM), F32)]
    res = pl.pallas_call(
        functools.partial(_inproj_kernel, emit_kv=kv_batch is not None),
        grid=(t // tm, IN_COLS // tn),
        in_specs=[pl.BlockSpec((tm, d), lambda i, j: (i, 0)),
                  pl.BlockSpec((1, d), lambda i, j: (0, 0)),
                  pl.BlockSpec((None, d, tn), lambda i, j: (layer, 0, j)),
                  pl.BlockSpec((2, 1, LANES), lambda i, j: (0, 0, 0)),
                  rope_spec, rope_spec, rope_spec],
        out_specs=out_specs,
        out_shape=out_shape,
        scratch_shapes=[pltpu.VMEM((tm, d), BF16)],
        compiler_params=_params("parallel", "arbitrary"),
        name=name,
    )(x, g, w, qkg, ra, rp, rm)
    return res if kv_batch is not None else res[0]


def _rope_tables(pos):
    inv_freq = ROPE_THETA ** (-jnp.arange(0, ROT_DIM, 2, dtype=F32) / ROT_DIM)
    ang = pos.astype(F32)[:, None] * inv_freq[None, :]
    cos, sin = jnp.cos(ang), jnp.sin(ang)
    p = pos.shape[0]
    half = ROT_DIM // 2
    one = jnp.ones((p, HEAD_DIM - ROT_DIM), F32)
    zero = jnp.zeros((p, HEAD_DIM - ROT_DIM), F32)
    zh = jnp.zeros((p, half), F32)
    a = jnp.concatenate([cos, cos, one], axis=-1)
    bp = jnp.concatenate([zh, sin, zero], axis=-1)
    bm = jnp.concatenate([-sin, zh, zero], axis=-1)
    return tuple(jnp.concatenate([t, t], axis=-1) for t in (a, bp, bm))


def _lambda_full(lamv_ref, lam0):
    lv = lamv_ref[...]
    s1 = jnp.sum(lv[0:1] * lv[1:2], axis=-1, keepdims=True)
    s2 = jnp.sum(lv[2:3] * lv[3:4], axis=-1, keepdims=True)
    return jnp.exp(s1) - jnp.exp(s2) + lam0


def _pattn_kernel(q_ref, k_ref, v_ref, lamv_ref, sg_ref, o_ref, kb_ref, vb_ref, m_ref, acc_ref,
                  *, lam0, tq):
    qi = pl.program_id(2)
    seq = k_ref.shape[0]

    @pl.when(qi == 0)
    def _():
        kb_ref[...] = k_ref[...].astype(BF16)
        vb_ref[:, :V_DIM] = v_ref[...].astype(BF16)
        lane = lax.broadcasted_iota(jnp.int32, (seq, LANES), 1)
        vb_ref[:, V_DIM:] = jnp.where(lane == 0, 1.0, 0.0).astype(BF16)

    q = q_ref[...] * (HEAD_DIM ** -0.5)
    lane = lax.broadcasted_iota(jnp.int32, (tq, LANES), 1)
    q0 = jnp.where(lane < HEAD_DIM, q, 0.0).astype(BF16)
    q1 = jnp.where(lane >= HEAD_DIM, q, 0.0).astype(BF16)
    qq = jnp.concatenate([q0, q1], axis=0)

    m_ref[...] = jnp.full(m_ref.shape, NEG_INF, F32)
    acc_ref[...] = jnp.zeros(acc_ref.shape, F32)

    def chunk(kj, masked):
        off = pl.multiple_of(kj * tq, tq)
        ks = kb_ref[pl.ds(off, tq), :]
        vs = vb_ref[pl.ds(off, tq), :]
        for rb in range(2 * tq // ATTN_RB):
            rs = slice(rb * ATTN_RB, (rb + 1) * ATTN_RB)
            s = _dot_nt(qq[rs], ks)
            if masked:
                row = lax.broadcasted_iota(jnp.int32, s.shape, 0) + rb * ATTN_RB
                col = lax.broadcasted_iota(jnp.int32, s.shape, 1)
                s = jnp.where(col <= (row & (tq - 1)), s, MASK_VALUE)
            m_old = m_ref[rs, :]
            m_new = jnp.maximum(m_old, jnp.max(s, axis=-1, keepdims=True))
            alpha = jnp.exp(m_old - m_new)
            p = jnp.exp(s - jnp.concatenate([m_new] * (tq // LANES), axis=1)).astype(BF16)
            acc_ref[rs, :] = jnp.concatenate([alpha, alpha], axis=1) * acc_ref[rs, :] + _dot(p, vs)
            m_ref[rs, :] = m_new

    def body(kj, carry):
        chunk(kj, False)
        return carry

    lax.fori_loop(0, qi, body, 0)
    chunk(qi, True)

    acc = acc_ref[...]
    o = acc[:, :V_DIM] / acc[:, V_DIM:V_DIM + 1]
    lam = _lambda_full(lamv_ref, lam0)
    a = o[:tq] - lam * o[tq:]
    o_ref[...] = (_rms_rows(a, sg_ref[...]) * (1.0 - lam0)).astype(o_ref.dtype)


def _prompt_attention(y, lamv, sg, lam0, batch, seq):
    tq = ATTN_TQ
    nq = seq // tq
    kern = functools.partial(_pattn_kernel, lam0=lam0, tq=tq)
    return pl.pallas_call(
        kern,
        grid=(batch, N_HEADS, nq),
        in_specs=[pl.BlockSpec((tq, LANES), lambda b, h, i: (b * nq + i, h)),
                  pl.BlockSpec((seq, LANES), lambda b, h, i: (b, N_HEADS + h)),
                  pl.BlockSpec((seq, LANES), lambda b, h, i: (b, 2 * N_HEADS + h)),
                  pl.BlockSpec((4, LANES), lambda b, h, i: (0, 0)),
                  pl.BlockSpec((1, LANES), lambda b, h, i: (0, 0))],
        out_specs=pl.BlockSpec((tq, LANES), lambda b, h, i: (b * nq + i, h)),
        out_shape=jax.ShapeDtypeStruct((batch * seq, ATTN_WIDTH), BF16),
        scratch_shapes=[pltpu.VMEM((seq, LANES), BF16),
                        pltpu.VMEM((seq, 2 * LANES), BF16),
                        pltpu.VMEM((2 * tq, LANES), F32),
                        pltpu.VMEM((2 * tq, 2 * LANES), F32)],
        compiler_params=_params("parallel", "parallel", "arbitrary"),
        name="prompt_attention",
    )(y, y, y, lamv, sg)


def _sattn_kernel(pt_ref, q_ref, kn_ref, vn_ref, lamv_ref, sg_ref, *rest, lam0, npg):
    k_refs = rest[:npg]
    v_refs = rest[npg:2 * npg]
    o_ref, qt_ref, knp_ref, vnp_ref, m_ref, l_ref, acc_ref = rest[2 * npg:]
    g = pl.program_id(1)
    dec_seq = q_ref.shape[0]
    rows = 2 * dec_seq
    nrow = N_HEADS * rows

    @pl.when(g == 0)
    def _():
        q = q_ref[...] * (HEAD_DIM ** -0.5)
        row = lax.broadcasted_iota(jnp.int32, (nrow, ATTN_WIDTH), 0)
        lane = lax.broadcasted_iota(jnp.int32, (nrow, ATTN_WIDTH), 1)
        qq = jnp.zeros((nrow, ATTN_WIDTH), F32)
        for i in range(dec_seq):
            qq = jnp.where((row & (dec_seq - 1)) == i, q[i:i + 1, :], qq)
        qt_ref[...] = jnp.where(lane // HEAD_DIM == row // dec_seq, qq, 0.0).astype(BF16)
        knp_ref[...] = jnp.zeros(knp_ref.shape, F32)
        vnp_ref[...] = jnp.zeros(vnp_ref.shape, F32)
        m_ref[...] = jnp.full(m_ref.shape, NEG_INF, F32)
        l_ref[...] = jnp.zeros(l_ref.shape, F32)
        acc_ref[...] = jnp.zeros(acc_ref.shape, F32)

    def attend(st, weighted_values):
        m_old = m_ref[...]
        m_new = jnp.maximum(m_old, jnp.max(st, axis=-1, keepdims=True))
        alpha = jnp.exp(m_old - m_new)
        p = jnp.exp(st - m_new)
        l_ref[...] = alpha * l_ref[...] + jnp.sum(p, axis=-1, keepdims=True)
        acc_ref[...] = alpha * acc_ref[...] + weighted_values(p)
        m_ref[...] = m_new

    kc = jnp.concatenate([r[...].astype(BF16) for r in k_refs], axis=1)
    st = _dot(qt_ref[...], kc)

    def page_values(p):
        outs = []
        for h in range(N_HEADS):
            vh = jnp.concatenate(
                [r[pl.ds(h, PAGE_SIZE, stride=N_HEADS), :].astype(BF16) for r in v_refs], axis=0)
            outs.append(_dot(p[h * rows:(h + 1) * rows, :].astype(BF16), vh))
        return jnp.concatenate(outs, axis=0)

    attend(st, page_values)

    @pl.when(g == pl.num_programs(1) - 1)
    def _():
        knp_ref[0:dec_seq, :] = kn_ref[...]
        vnp_ref[0:dec_seq, :] = vn_ref[...]
        sn = _dot_nt(qt_ref[...], knp_ref[...].astype(BF16))
        row = lax.broadcasted_iota(jnp.int32, sn.shape, 0)
        col = lax.broadcasted_iota(jnp.int32, sn.shape, 1)
        sn = jnp.where(col <= (row & (dec_seq - 1)), sn, MASK_VALUE)

        def new_values(p):
            o = _dot(p.astype(BF16), vnp_ref[...].astype(BF16))
            return jnp.concatenate(
                [o[h * rows:(h + 1) * rows, h * V_DIM:(h + 1) * V_DIM] for h in range(N_HEADS)], axis=0)

        attend(sn, new_values)

        o = acc_ref[...] / l_ref[...]
        lam = _lambda_full(lamv_ref, lam0)
        a = o - lam * pltpu.roll(o, nrow - dec_seq, 0)
        a = _rms_rows(a, sg_ref[...]) * (1.0 - lam0)
        for h in range(N_HEADS):
            o_ref[:, h * V_DIM:(h + 1) * V_DIM] = a[h * rows:h * rows + dec_seq, :]


def _sample_attention(y3, cache_kt, cache_vr, page_table, layer, lamv, sg, lam0):
    dec_b, dec_seq, _ = y3.shape
    n_pages = page_table.shape[1]
    npg = PAGES_PER_STEP
    assert n_pages % npg == 0 and dec_seq & (dec_seq - 1) == 0
    nrow = N_HEADS * 2 * dec_seq
    pt = page_table.reshape(-1)

    def page_spec(p):
        return pl.BlockSpec((None, None, ATTN_WIDTH, PAGE_SIZE),
                            lambda b, g, pt_ref: (layer, pt_ref[b * n_pages + g * npg + p], 0, 0))

    def col_spec(c):
        return pl.BlockSpec((None, dec_seq, ATTN_WIDTH), lambda b, g, pt_ref: (b, 0, c))

    grid_spec = pltpu.PrefetchScalarGridSpec(
        num_scalar_prefetch=1,
        grid=(dec_b, n_pages // npg),
        in_specs=[col_spec(0), col_spec(1), col_spec(2),
                  pl.BlockSpec((4, LANES), lambda b, g, pt_ref: (0, 0)),
                  pl.BlockSpec((1, LANES), lambda b, g, pt_ref: (0, 0))]
                 + [page_spec(p) for p in range(npg)] + [page_spec(p) for p in range(npg)],
        out_specs=pl.BlockSpec((None, dec_seq, ATTN_WIDTH), lambda b, g, pt_ref: (b, 0, 0)),
        scratch_shapes=[pltpu.VMEM((nrow, ATTN_WIDTH), BF16),
                        pltpu.VMEM((PAGE_SIZE, ATTN_WIDTH), F32),
                        pltpu.VMEM((PAGE_SIZE, ATTN_WIDTH), F32),
                        pltpu.VMEM((nrow, 1), F32),
                        pltpu.VMEM((nrow, 1), F32),
                        pltpu.VMEM((nrow, V_DIM), F32)],
    )
    kern = functools.partial(_sattn_kernel, lam0=lam0, npg=npg)
    return pl.pallas_call(
        kern,
        grid_spec=grid_spec,
        out_shape=jax.ShapeDtypeStruct((dec_b, dec_seq, ATTN_WIDTH), F32),
        compiler_params=_params("parallel", "arbitrary"),
        name="sample_attention",
    )(pt, y3, y3, y3, lamv, sg, *([cache_kt] * npg), *([cache_vr] * npg))


def _conv_prompt_kernel(gb_ref, gc_ref, xc_ref, hgc_ref, hxc_ref, w_ref, c_ref, st_ref, *, tiles_per_seq):
    i = pl.program_id(0)
    u = gc_ref[...] * xc_ref[...]
    tm = u.shape[0]
    uh = hgc_ref[...] * hxc_ref[...]
    uh = jnp.where(i % tiles_per_seq == 0, 0.0, uh)
    row = lax.broadcasted_iota(jnp.int32, u.shape, 0)
    prev1, prev2 = uh[SUBLANES - 1:SUBLANES], uh[SUBLANES - 2:SUBLANES - 1]
    u1 = jnp.where(row == 0, prev1, pltpu.roll(u, 1, 0))
    u2 = jnp.where(row == 0, prev2, jnp.where(row == 1, prev1, pltpu.roll(u, 2, 0)))
    w = w_ref[...]
    conv = w[0:1] * u2 + w[1:2] * u1 + w[2:3] * u
    c_ref[...] = (gb_ref[...] * conv).astype(c_ref.dtype)
    st_ref[...] = u[tm - (CONV_K - 1):tm]


def _conv_prompt(y, conv_w, batch, seq, tm):
    t = y.shape[0]
    tps = seq // tm
    hb = tm // SUBLANES
    kern = functools.partial(_conv_prompt_kernel, tiles_per_seq=tps)
    wide = lambda c: pl.BlockSpec((tm, CONV_WIDTH), lambda i: (i, c))
    halo = lambda c: pl.BlockSpec((SUBLANES, CONV_WIDTH), lambda i: (jnp.maximum(i * hb - 1, 0), c))
    return pl.pallas_call(
        kern,
        grid=(t // tm,),
        in_specs=[wide(3), wide(4), wide(5), halo(4), halo(5),
                  pl.BlockSpec((CONV_K, CONV_WIDTH), lambda i: (0, 0))],
        out_specs=[pl.BlockSpec((tm, CONV_WIDTH), lambda i: (i, 0)),
                   pl.BlockSpec((None, CONV_K - 1, CONV_WIDTH), lambda i: (i // tps, 0, 0))],
        out_shape=[jax.ShapeDtypeStruct((t, CONV_WIDTH), BF16),
                   jax.ShapeDtypeStruct((batch, CONV_K - 1, CONV_WIDTH), F32)],
        compiler_params=_params("arbitrary"),
        name="conv_prompt",
    )(y, y, y, y, y, conv_w)


def _conv_sample_kernel(gb_ref, gc_ref, xc_ref, e1_ref, e2_ref, w_ref, c_ref, u_ref, *, dec_seq):
    u = gc_ref[...] * xc_ref[...]
    pos = lax.broadcasted_iota(jnp.int32, u.shape, 0) & (dec_seq - 1)
    u1 = jnp.where(pos == 0, e1_ref[...], pltpu.roll(u, 1, 0))
    u2 = jnp.where(pos <= 1, e2_ref[...], pltpu.roll(u, 2, 0))
    w = w_ref[...]
    conv = w[0:1] * u2 + w[1:2] * u1 + w[2:3] * u
    c_ref[...] = (gb_ref[...] * conv).astype(c_ref.dtype)
    u_ref[...] = u


def _conv_sample(y, conv_w, prev, dec_seq):
    t = y.shape[0]
    dec_b = t // dec_seq
    zeros = jnp.zeros((dec_b, dec_seq - 1, CONV_WIDTH), F32)
    e1 = jnp.concatenate([prev[:, 1:2], zeros], axis=1).reshape(t, CONV_WIDTH)
    e2 = jnp.concatenate([prev, zeros[:, 1:]], axis=1).reshape(t, CONV_WIDTH)
    kern = functools.partial(_conv_sample_kernel, dec_seq=dec_seq)
    wide = lambda c: pl.BlockSpec((t, CONV_WIDTH), lambda i: (0, c))
    full = pl.BlockSpec((t, CONV_WIDTH), lambda i: (0, 0))
    return pl.pallas_call(
        kern,
        grid=(1,),
        in_specs=[wide(3), wide(4), wide(5), full, full,
                  pl.BlockSpec((CONV_K, CONV_WIDTH), lambda i: (0, 0))],
        out_specs=[full, full],
        out_shape=[jax.ShapeDtypeStruct((t, CONV_WIDTH), BF16),
                   jax.ShapeDtypeStruct((t, CONV_WIDTH), F32)],
        compiler_params=_params("arbitrary"),
        name="conv_sample",
    )(y, y, y, e1, e2, conv_w)


def _outproj_kernel(x_ref, a_ref, c_ref, wa_ref, wc_ref, o_ref):
    o_ref[...] = (x_ref[...] + _dot(a_ref[...].astype(BF16), wa_ref[...])
                  + _dot(c_ref[...].astype(BF16), wc_ref[...]))


def _outproj(x, a, c, w, layer, tm, tn, name):
    t, d = x.shape
    return pl.pallas_call(
        _outproj_kernel,
        grid=(t // tm, d // tn),
        in_specs=[pl.BlockSpec((tm, tn), lambda i, j: (i, j)),
                  pl.BlockSpec((tm, ATTN_WIDTH), lambda i, j: (i, 0)),
                  pl.BlockSpec((tm, CONV_WIDTH), lambda i, j: (i, 0)),
                  pl.BlockSpec((None, ATTN_WIDTH, tn), lambda i, j: (layer, 0, j)),
                  pl.BlockSpec((None, CONV_WIDTH, tn), lambda i, j: (layer, 1, j))],
        out_specs=pl.BlockSpec((tm, tn), lambda i, j: (i, j)),
        out_shape=jax.ShapeDtypeStruct((t, d), F32),
        compiler_params=_params("parallel", "arbitrary"),
        name=name,
    )(x, a, c, w, w)


def _top_with_index(s, k):
    nrows = s.shape[0]
    riota = lax.broadcasted_iota(jnp.int32, s.shape, 0).astype(F32)
    vals, idxs = [], []
    for _ in range(k):
        m = jnp.max(s, axis=0, keepdims=True)
        i = jnp.min(jnp.where(s == m, riota, float(nrows)), axis=0, keepdims=True)
        vals.append(m)
        idxs.append(i)
        s = jnp.where(riota == i, NEG_INF, s)
    return vals, idxs


def _stack_rows(rows, nrows, fill):
    r = lax.broadcasted_iota(jnp.int32, (nrows, rows[0].shape[1]), 0)
    out = jnp.full(r.shape, fill, F32)
    for j, row in enumerate(rows):
        out = jnp.where(r == j, row, out)
    return out


def _candidates(t1, i1, t2, i2):
    k = PEER_TOPK
    t2_all = _stack_rows(t2, k, NEG_INF)
    i2_all = _stack_rows(i2, k, 0.0)
    tile_row = lax.broadcasted_iota(jnp.int32, (SUBLANES, t2_all.shape[1]), 0)
    vals, codes, tail_v, tail_c = [], [], [], []
    for j1 in range(k):
        cnt = k // (j1 + 1)
        base = i1[j1] * float(PEER_NKEYS)
        if cnt > SUBLANES:
            vals.append(t1[j1] + t2_all)
            codes.append(base + i2_all)
        elif cnt >= SUBLANES // 2:
            v = t1[j1] + t2_all[:SUBLANES]
            vals.append(v if cnt == SUBLANES else jnp.where(tile_row < cnt, v, NEG_INF))
            codes.append(base + i2_all[:SUBLANES])
        else:
            tail_v += [t1[j1] + t2[j2] for j2 in range(cnt)]
            tail_c += [base + i2[j2] for j2 in range(cnt)]
    tail_rows = -(-len(tail_v) // SUBLANES) * SUBLANES
    vals.append(_stack_rows(tail_v, tail_rows, NEG_INF))
    codes.append(_stack_rows(tail_c, tail_rows, 0.0))
    return jnp.concatenate(vals, axis=0), jnp.concatenate(codes, axis=0)


def _peer_select_kernel(x_ref, g_ref, hq_ref, sk_ref, xn_ref, gate_ref, at_ref, bt_ref, gt_ref,
                        atm_ref, btm_ref, gtm_ref, m_ref):
    tm = x_ref.shape[0]
    xn_ref[...] = _rms_rows(x_ref[...], g_ref[...]).astype(BF16)

    for h in range(PEER_HEADS):
        c0 = h * 2 * PEER_NKEYS
        q1 = hq_ref[:, c0:c0 + PEER_NKEYS].astype(BF16)
        q2 = hq_ref[:, c0 + PEER_NKEYS:c0 + 2 * PEER_NKEYS].astype(BF16)
        s1 = _dot_nt(sk_ref[h, 0], q1)
        s2 = _dot_nt(sk_ref[h, 1], q2)
        t1, i1 = _top_with_index(s1, PEER_TOPK)
        t2, i2 = _top_with_index(s2, PEER_TOPK)
        cand, code = _candidates(t1, i1, t2, i2)
        sc, rows = _top_with_index(cand, PEER_TOPK)
        riota = lax.broadcasted_iota(jnp.int32, cand.shape, 0).astype(F32)
        ex = [jnp.exp(c - sc[0]) for c in sc]
        z = ex[0]
        for v in ex[1:]:
            z = z + v
        for j in range(PEER_TOPK):
            picked = jnp.sum(jnp.where(riota == rows[j], code, 0.0), axis=0, keepdims=True)
            n1 = jnp.floor(picked * (1.0 / PEER_NKEYS))
            slot = h * PEER_TOPK + j
            at_ref[slot:slot + 1, :] = n1
            bt_ref[slot:slot + 1, :] = picked - n1 * float(PEER_NKEYS)
            gt_ref[slot:slot + 1, :] = ex[j] / z

    atm_ref[...] = at_ref[...].T
    btm_ref[...] = bt_ref[...].T
    gtm_ref[...] = gt_ref[...].T

    niota = lax.broadcasted_iota(jnp.int32, (PEER_NKEYS, LANES), 0).astype(F32)

    def group(gi, carry):
        t0 = pl.multiple_of(gi * GATE_GROUP, GATE_GROUP)
        for tt in range(GATE_GROUP):
            a_row = atm_ref[pl.ds(t0 + tt, 1), :]
            b_row = btm_ref[pl.ds(t0 + tt, 1), :]
            g_row = gtm_ref[pl.ds(t0 + tt, 1), :]
            lhs = jnp.where(niota == a_row, g_row, 0.0).astype(BF16)
            rhs = jnp.where(niota == b_row, 1.0, 0.0).astype(BF16)
            m_ref[tt * GATE_PITCH:tt * GATE_PITCH + PEER_NKEYS, :] = _dot_nt(lhs, rhs)
        for n1 in range(PEER_NKEYS):
            rows_n1 = m_ref[pl.ds(n1, GATE_GROUP, stride=GATE_PITCH), :]
            gate_ref[pl.ds(t0, GATE_GROUP), n1 * PEER_NKEYS:(n1 + 1) * PEER_NKEYS] = rows_n1.astype(BF16)
        return carry

    lax.fori_loop(0, tm // GATE_GROUP, group, 0)


def _peer_select(x, g, hq, subkeys, layer, tm, name):
    t, d = x.shape
    nk = PEER_NKEYS
    slots = PEER_HEADS * PEER_TOPK
    return pl.pallas_call(
        _peer_select_kernel,
        grid=(t // tm,),
        in_specs=[pl.BlockSpec((tm, d), lambda i: (i, 0)),
                  pl.BlockSpec((1, d), lambda i: (0, 0)),
                  pl.BlockSpec((tm, PEER_HEADS * 2 * nk), lambda i: (i, 0)),
                  pl.BlockSpec((None, PEER_HEADS, 2, nk, nk), lambda i: (layer, 0, 0, 0, 0))],
        out_specs=[pl.BlockSpec((tm, d), lambda i: (i, 0)),
                   pl.BlockSpec((tm, PEER_N), lambda i: (i, 0))],
        out_shape=[jax.ShapeDtypeStruct((t, d), BF16),
                   jax.ShapeDtypeStruct((t, PEER_N), BF16)],
        scratch_shapes=[pltpu.VMEM((slots, tm), F32)] * 3 + [pltpu.VMEM((tm, slots), F32)] * 3
                       + [pltpu.VMEM((GATE_GROUP * GATE_PITCH, nk), F32)],
        compiler_params=_params("parallel"),
        name=name,
    )(x, g, hq, subkeys)


def _peer_dense_kernel(xn_ref, gate_ref, u_ref, v_ref, o_ref):
    e = pl.program_id(1)
    tm = xn_ref.shape[0]

    @pl.when(e == 0)
    def _():
        o_ref[...] = jnp.zeros(o_ref.shape, F32)

    blk = min(tm, DENSE_RB)
    for k in range(tm // blk):
        rs = slice(k * blk, (k + 1) * blk)
        a = _dot_nt(xn_ref[rs, :], u_ref[...])
        twice_gelu = a * (1.0 + lax.erf(a * SQRT_HALF))
        weighted = twice_gelu.astype(BF16) * (gate_ref[rs, :] * jnp.asarray(0.5, BF16))
        o_ref[rs, :] += _dot(weighted, v_ref[...])


def _peer_dense(xn, gate, u, v, layer, tm, name):
    t, d = xn.shape
    ec = PEER_EC
    return pl.pallas_call(
        _peer_dense_kernel,
        grid=(t // tm, PEER_N // ec),
        in_specs=[pl.BlockSpec((tm, d), lambda i, e: (i, 0)),
                  pl.BlockSpec((tm, ec), lambda i, e: (i, e)),
                  pl.BlockSpec((None, ec, d), lambda i, e: (layer, e, 0)),
                  pl.BlockSpec((None, ec, d), lambda i, e: (layer, e, 0))],
        out_specs=pl.BlockSpec((tm, d), lambda i, e: (i, 0)),
        out_shape=jax.ShapeDtypeStruct((t, d), F32),
        compiler_params=_params("parallel", "arbitrary"),
        name=name,
    )(xn, gate, u, v)


def _ple_kernel(x_ref, peer_ref, p_ref, g_ref, wg_ref, wp_ref, o_ref):
    x = x_ref[...] + peer_ref[...]
    xn = _rms_rows(x, g_ref[...]).astype(BF16)
    z = _dot(xn, wg_ref[...])
    gate = 1.0 / (1.0 + jnp.exp(-z))
    o_ref[...] = x + gate * _dot(p_ref[...].astype(BF16), wp_ref[...])


def _ple(x, peer, p, g, wg, wp, layer, tm, name):
    t, d = x.shape
    return pl.pallas_call(
        _ple_kernel,
        grid=(t // tm,),
        in_specs=[pl.BlockSpec((tm, d), lambda i: (i, 0)),
                  pl.BlockSpec((tm, d), lambda i: (i, 0)),
                  pl.BlockSpec((tm, PLE_DIM), lambda i: (i, 0)),
                  pl.BlockSpec((1, d), lambda i: (0, 0)),
                  pl.BlockSpec((None, d, d), lambda i: (layer, 0, 0)),
                  pl.BlockSpec((None, PLE_DIM, d), lambda i: (layer, 0, 0))],
        out_specs=pl.BlockSpec((tm, d), lambda i: (i, 0)),
        out_shape=jax.ShapeDtypeStruct((t, d), F32),
        compiler_params=_params("parallel"),
        name=name,
    )(x, peer, p, g, wg, wp)


def _lambda_init(layer_idx):
    return 0.8 - 0.6 * math.exp(-0.3 * layer_idx)


def _token_tail(x, a, c, wts, lw, layer, p_l, tm, tag):
    x1 = _outproj(x, a, c, wts["w_out"], layer, tm, D_MODEL, "outproj_" + tag)
    hq = _norm_mm(x1, lw["ffn_norm_g"], wts["peer_wq"], layer, tm, D_MODEL, "peer_query_" + tag)
    xn, gate = _peer_select(x1, lw["ffn_norm_g"], hq, wts["peer_subkeys"], layer, min(tm, SELECT_TM),
                            "peer_select_" + tag)
    peer = _peer_dense(xn, gate, wts["peer_u"], wts["peer_v"], layer, tm, "peer_dense_" + tag)
    return _ple(x1, peer, p_l, lw["ple_norm_g"], wts["ple_gate_w"], wts["ple_proj_w"], layer,
                min(tm, PLE_TM), "ple_" + tag)


def kernel(x_prompt, x_sample, cache_k, cache_v, state_conv, page_table, p_prompt, p_sample,
           attn_norm_g, w_in, q_norm_g, k_norm_g, lam_q1, lam_k1, lam_q2, lam_k2, subln_g,
           conv_w, w_out, ffn_norm_g, peer_wq, peer_subkeys, peer_u, peer_v,
           ple_norm_g, ple_gate_w, ple_proj_w):
    batch, seq, d = x_prompt.shape
    dec_b, dec_seq, _ = x_sample.shape
    depth = w_in.shape[0]
    n_pool = cache_k.shape[1]
    past_len = page_table.shape[1] * cache_k.shape[2]
    tp, ts = batch * seq, dec_b * dec_seq

    rope_p = _rope_tables(jnp.arange(seq))
    rope_s = _rope_tables(past_len + (jnp.arange(ts) % dec_seq))
    ck = cache_k.transpose(0, 1, 3, 4, 5, 2).reshape(depth, n_pool, ATTN_WIDTH, PAGE_SIZE)
    cv = cache_v.reshape(depth, n_pool, PAGE_SIZE * N_HEADS, V_DIM)

    yp = x_prompt.reshape(tp, d)
    ys = x_sample.reshape(ts, d)
    outs = [[] for _ in range(6)]
    row = lambda v: v.reshape(1, -1)
    wts = {
        "w_in": w_in.astype(BF16),
        "w_out": w_out.astype(BF16),
        "peer_wq": peer_wq.astype(BF16),
        "peer_subkeys": peer_subkeys.astype(BF16),
        "peer_u": peer_u.astype(BF16),
        "peer_v": peer_v.astype(BF16),
        "ple_gate_w": ple_gate_w.astype(BF16),
        "ple_proj_w": ple_proj_w.astype(BF16),
    }
    for l in range(depth):
        lam0 = _lambda_init(l)
        lw = {"ffn_norm_g": row(ffn_norm_g[l]), "ple_norm_g": row(ple_norm_g[l])}
        g_in = row(attn_norm_g[l])
        qkg = jnp.stack([jnp.tile(q_norm_g[l], 2), jnp.tile(k_norm_g[l], 2)]).reshape(2, 1, LANES)
        lamv = jnp.pad(jnp.stack([lam_q1[l], lam_k1[l], lam_q2[l], lam_k2[l]]),
                       ((0, 0), (0, LANES - HEAD_DIM)))
        sg = row(subln_g[l])

        y, k_t, v_t = _inproj(yp, g_in, wts["w_in"], l, qkg, rope_p, PROMPT_TM, "inproj_prompt",
                                kv_batch=(batch, seq))
        a = _prompt_attention(y, lamv, sg, lam0, batch, seq)
        c, cstate = _conv_prompt(y, conv_w[l], batch, seq, PROMPT_TM)
        outs[0].append(k_t.reshape(batch, N_HEADS, 2, HEAD_DIM, seq).transpose(0, 4, 1, 2, 3))
        outs[1].append(v_t.reshape(batch, seq, N_HEADS, V_DIM))
        outs[2].append(cstate)
        yp = _token_tail(yp, a, c, wts, lw, l, p_prompt[l].reshape(tp, PLE_DIM), PROMPT_TM, "prompt")

        y = _inproj(ys, g_in, wts["w_in"], l, qkg, rope_s, ts, "inproj_sample")
        a = _sample_attention(y.reshape(dec_b, dec_seq, IN_COLS), ck, cv, page_table, l, lamv, sg, lam0)
        c, u = _conv_sample(y, conv_w[l], state_conv[l], dec_seq)
        outs[3].append(y[:, ATTN_WIDTH:2 * ATTN_WIDTH].reshape(dec_b, dec_seq, N_HEADS, 2, HEAD_DIM))
        outs[4].append(y[:, 2 * ATTN_WIDTH:3 * ATTN_WIDTH].reshape(dec_b, dec_seq, N_HEADS, V_DIM))
        outs[5].append(u.reshape(dec_b, dec_seq, CONV_WIDTH)[:, dec_seq - (CONV_K - 1):])
        ys = _token_tail(ys, a.reshape(ts, ATTN_WIDTH), c, wts, lw, l, p_sample[l].reshape(ts, PLE_DIM), ts,
                         "sample")

    return (yp.reshape(batch, seq, d), ys.reshape(dec_b, dec_seq, d),
            jnp.stack(outs[0]), jnp.stack(outs[1]), jnp.stack(outs[2]),
            jnp.stack(outs[3]), jnp.stack(outs[4]), jnp.stack(outs[5]))
```

```python
import functools
import math

import jax
import jax.numpy as jnp
from jax import lax
from jax.experimental import pallas as pl
from jax.experimental.pallas import tpu as pltpu

F32 = jnp.float32
BF16 = jnp.bfloat16

D_MODEL = 2048
N_HEADS = 8
HEAD_DIM = 64
V_DIM = 128
ATTN_WIDTH = 1024
CONV_WIDTH = 1024
IN_COLS = 3 * ATTN_WIDTH + 3 * CONV_WIDTH
ROT_DIM = 16
ROPE_THETA = 500000.0
CONV_K = 3
PLE_DIM = 256
PEER_HEADS = 8
PEER_NKEYS = 128
PEER_N = PEER_NKEYS * PEER_NKEYS
PEER_TOPK = 16
EPS = 1e-6
PAGE_SIZE = 128

LANES = 128
SUBLANES = 8
VMEM_LIMIT_BYTES = 56 * 1024 * 1024
NEG_INF = float("-inf")
MASK_VALUE = float(jnp.finfo(jnp.float32).min)
SQRT_HALF = 0.7071067811865476

PROMPT_TM = 512
ATTN_TQ = 1024
ATTN_RB = 256
PAGES_PER_STEP = 16
PEER_EC = 1024
DENSE_RB = 256
SELECT_TM = 256
GATE_GROUP = 16
GATE_PITCH = PEER_NKEYS + SUBLANES
PLE_TM = 256


def _params(*sem):
    return pltpu.CompilerParams(dimension_semantics=sem, vmem_limit_bytes=VMEM_LIMIT_BYTES)


def _rms_rows(x, g):
    ms = jnp.mean(x * x, axis=-1, keepdims=True)
    return x * lax.rsqrt(ms + EPS) * g


def _dot(a, b):
    return jnp.dot(a, b, preferred_element_type=F32)


def _dot_nt(a, b):
    return lax.dot_general(a, b, (((1,), (1,)), ((), ())), preferred_element_type=F32)


def _norm_mm_kernel(x_ref, g_ref, w_ref, o_ref, hn_ref):
    @pl.when(pl.program_id(1) == 0)
    def _():
        hn_ref[...] = _rms_rows(x_ref[...], g_ref[...]).astype(BF16)

    o_ref[...] = _dot(hn_ref[...], w_ref[...])


def _inproj_kernel(x_ref, g_ref, w_ref, qkg_ref, ra_ref, rp_ref, rm_ref, o_ref, *rest, emit_kv):
    hn_ref = rest[-1]
    j = pl.program_id(1)

    @pl.when(j == 0)
    def _():
        hn_ref[...] = _rms_rows(x_ref[...], g_ref[...]).astype(BF16)

    @pl.when(j == 0)
    def _():
        r = lax.broadcasted_iota(jnp.int32, (LANES, LANES), 0) // HEAD_DIM
        c = lax.broadcasted_iota(jnp.int32, (LANES, LANES), 1) // HEAD_DIM
        group_sum = jnp.where(r == c, 1.0, 0.0).astype(BF16)
        tm = hn_ref.shape[0]
        blk = min(tm, DENSE_RB)
        for b in range(tm // blk):
            rs = slice(b * blk, (b + 1) * blk)
            yb = _dot(hn_ref[rs, :], w_ref[...])
            for h in range(2 * N_HEADS):
                is_key = h >= N_HEADS
                yh = yb[:, h * LANES:(h + 1) * LANES]
                sq = yh * yh
                hi = sq.astype(BF16)
                lo = (sq - hi.astype(F32)).astype(BF16)
                ss = _dot(hi, group_sum) + _dot(lo, group_sum)
                yn = yh * lax.rsqrt(ss * (1.0 / HEAD_DIM) + EPS) * qkg_ref[1 if is_key else 0]
                out = (yn * ra_ref[rs, :]
                       + pltpu.roll(yn, ROT_DIM // 2, 1) * rp_ref[rs, :]
                       + pltpu.roll(yn, LANES - ROT_DIM // 2, 1) * rm_ref[rs, :])
                o_ref[rs, h * LANES:(h + 1) * LANES] = out
                if emit_kv and is_key:
                    hk = h - N_HEADS
                    rest[0][hk * LANES:(hk + 1) * LANES, rs] = out.T

    @pl.when(j > 0)
    def _():
        o_ref[...] = _dot(hn_ref[...], w_ref[...])

    if emit_kv:
        @pl.when(j == 1)
        def _():
            heads = jnp.stack([o_ref[:, h * V_DIM:(h + 1) * V_DIM] for h in range(N_HEADS)], axis=0)
            rest[1][...] = jnp.swapaxes(heads, 0, 1)


def _norm_mm(x, g, w, layer, tm, tn, name):
    t, d = x.shape
    n = w.shape[2]
    return pl.pallas_call(
        _norm_mm_kernel,
        grid=(t // tm, n // tn),
        in_specs=[pl.BlockSpec((tm, d), lambda i, j: (i, 0)),
                  pl.BlockSpec((1, d), lambda i, j: (0, 0)),
                  pl.BlockSpec((None, d, tn), lambda i, j: (layer, 0, j))],
        out_specs=pl.BlockSpec((tm, tn), lambda i, j: (i, j)),
        out_shape=jax.ShapeDtypeStruct((t, n), F32),
        scratch_shapes=[pltpu.VMEM((tm, d), BF16)],
        compiler_params=_params("parallel", "arbitrary"),
        name=name,
    )(x, g, w)


def _inproj(x, g, w, layer, qkg, rope, tm, name, kv_batch=None):
    t, d = x.shape
    tn = 2 * ATTN_WIDTH
    ra, rp, rm = rope
    nrep = ra.shape[0] // tm
    rope_spec = pl.BlockSpec((tm, LANES), lambda i, j: (i % nrep, 0))
    out_specs = [pl.BlockSpec((tm, tn), lambda i, j: (i, j))]
    out_shape = [jax.ShapeDtypeStruct((t, IN_COLS), F32)]
    if kv_batch is not None:
        batch, seq = kv_batch
        tps = seq // tm
        out_specs += [pl.BlockSpec((None, ATTN_WIDTH, tm), lambda i, j: (i // tps, 0, i % tps)),
                      pl.BlockSpec((tm, N_HEADS, V_DIM), lambda i, j: (i, 0, 0))]
        out_shape += [jax.ShapeDtypeStruct((batch, ATTN_WIDTH, seq), F32),
                      jax.ShapeDtypeStruct((t, N_HEADS, V_DIM), F32)]
    res = pl.pallas_call(
        functools.partial(_inproj_kernel, emit_kv=kv_batch is not None),
        grid=(t // tm, IN_COLS // tn),
        in_specs=[pl.BlockSpec((tm, d), lambda i, j: (i, 0)),
                  pl.BlockSpec((1, d), lambda i, j: (0, 0)),
                  pl.BlockSpec((None, d, tn), lambda i, j: (layer, 0, j)),
                  pl.BlockSpec((2, 1, LANES), lambda i, j: (0, 0, 0)),
                  rope_spec, rope_spec, rope_spec],
        out_specs=out_specs,
        out_shape=out_shape,
        scratch_shapes=[pltpu.VMEM((tm, d), BF16)],
        compiler_params=_params("parallel", "arbitrary"),
        name=name,
    )(x, g, w, qkg, ra, rp, rm)
    return res if kv_batch is not None else res[0]


def _rope_tables(pos):
    inv_freq = ROPE_THETA ** (-jnp.arange(0, ROT_DIM, 2, dtype=F32) / ROT_DIM)
    ang = pos.astype(F32)[:, None] * inv_freq[None, :]
    cos, sin = jnp.cos(ang), jnp.sin(ang)
    p = pos.shape[0]
    half = ROT_DIM // 2
    one = jnp.ones((p, HEAD_DIM - ROT_DIM), F32)
    zero = jnp.zeros((p, HEAD_DIM - ROT_DIM), F32)
    zh = jnp.zeros((p, half), F32)
    a = jnp.concatenate([cos, cos, one], axis=-1)
    bp = jnp.concatenate([zh, sin, zero], axis=-1)
    bm = jnp.concatenate([-sin, zh, zero], axis=-1)
    return tuple(jnp.concatenate([t, t], axis=-1) for t in (a, bp, bm))


def _lambda_full(lamv_ref, lam0):
    lv = lamv_ref[...]
    s1 = jnp.sum(lv[0:1] * lv[1:2], axis=-1, keepdims=True)
    s2 = jnp.sum(lv[2:3] * lv[3:4], axis=-1, keepdims=True)
    return jnp.exp(s1) - jnp.exp(s2) + lam0


def _pattn_kernel(q_ref, k_ref, v_ref, lamv_ref, sg_ref, o_ref, kb_ref, vb_ref, m_ref, acc_ref,
                  *, lam0, tq):
    qi = pl.program_id(2)
    seq = k_ref.shape[0]

    @pl.when(qi == 0)
    def _():
        kb_ref[...] = k_ref[...].astype(BF16)
        vb_ref[:, :V_DIM] = v_ref[...].astype(BF16)
        lane = lax.broadcasted_iota(jnp.int32, (seq, LANES), 1)
        vb_ref[:, V_DIM:] = jnp.where(lane == 0, 1.0, 0.0).astype(BF16)

    q = q_ref[...] * (HEAD_DIM ** -0.5)
    lane = lax.broadcasted_iota(jnp.int32, (tq, LANES), 1)
    q0 = jnp.where(lane < HEAD_DIM, q, 0.0).astype(BF16)
    q1 = jnp.where(lane >= HEAD_DIM, q, 0.0).astype(BF16)
    qq = jnp.concatenate([q0, q1], axis=0)

    m_ref[...] = jnp.full(m_ref.shape, NEG_INF, F32)
    acc_ref[...] = jnp.zeros(acc_ref.shape, F32)

    def chunk(kj, masked):
        off = pl.multiple_of(kj * tq, tq)
        ks = kb_ref[pl.ds(off, tq), :]
        vs = vb_ref[pl.ds(off, tq), :]
        for rb in range(2 * tq // ATTN_RB):
            rs = slice(rb * ATTN_RB, (rb + 1) * ATTN_RB)
            s = _dot_nt(qq[rs], ks)
            if masked:
                row = lax.broadcasted_iota(jnp.int32, s.shape, 0) + rb * ATTN_RB
                col = lax.broadcasted_iota(jnp.int32, s.shape, 1)
                s = jnp.where(col <= (row & (tq - 1)), s, MASK_VALUE)
            m_old = m_ref[rs, :]
            m_new = jnp.maximum(m_old, jnp.max(s, axis=-1, keepdims=True))
            alpha = jnp.exp(m_old - m_new)
            p = jnp.exp(s - jnp.concatenate([m_new] * (tq // LANES), axis=1)).astype(BF16)
            acc_ref[rs, :] = jnp.concatenate([alpha, alpha], axis=1) * acc_ref[rs, :] + _dot(p, vs)
            m_ref[rs, :] = m_new

    def body(kj, carry):
        chunk(kj, False)
        return carry

    lax.fori_loop(0, qi, body, 0)
    chunk(qi, True)

    acc = acc_ref[...]
    o = acc[:, :V_DIM] / acc[:, V_DIM:V_DIM + 1]
    lam = _lambda_full(lamv_ref, lam0)
    a = o[:tq] - lam * o[tq:]
    o_ref[...] = (_rms_rows(a, sg_ref[...]) * (1.0 - lam0)).astype(o_ref.dtype)


def _prompt_attention(y, lamv, sg, lam0, batch, seq):
    tq = ATTN_TQ
    nq = seq // tq
    kern = functools.partial(_pattn_kernel, lam0=lam0, tq=tq)
    return pl.pallas_call(
        kern,
        grid=(batch, N_HEADS, nq),
        in_specs=[pl.BlockSpec((tq, LANES), lambda b, h, i: (b * nq + i, h)),
                  pl.BlockSpec((seq, LANES), lambda b, h, i: (b, N_HEADS + h)),
                  pl.BlockSpec((seq, LANES), lambda b, h, i: (b, 2 * N_HEADS + h)),
                  pl.BlockSpec((4, LANES), lambda b, h, i: (0, 0)),
                  pl.BlockSpec((1, LANES), lambda b, h, i: (0, 0))],
        out_specs=pl.BlockSpec((tq, LANES), lambda b, h, i: (b * nq + i, h)),
        out_shape=jax.ShapeDtypeStruct((batch * seq, ATTN_WIDTH), BF16),
        scratch_shapes=[pltpu.VMEM((seq, LANES), BF16),
                        pltpu.VMEM((seq, 2 * LANES), BF16),
                        pltpu.VMEM((2 * tq, LANES), F32),
                        pltpu.VMEM((2 * tq, 2 * LANES), F32)],
        compiler_params=_params("parallel", "parallel", "arbitrary"),
        name="prompt_attention",
    )(y, y, y, lamv, sg)


def _sattn_kernel(pt_ref, q_ref, kn_ref, vn_ref, lamv_ref, sg_ref, *rest, lam0, npg):
    k_refs = rest[:npg]
    v_refs = rest[npg:2 * npg]
    o_ref, qt_ref, knp_ref, vnp_ref, m_ref, l_ref, acc_ref = rest[2 * npg:]
    g = pl.program_id(1)
    dec_seq = q_ref.shape[0]
    rows = 2 * dec_seq
    nrow = N_HEADS * rows

    @pl.when(g == 0)
    def _():
        q = q_ref[...] * (HEAD_DIM ** -0.5)
        row = lax.broadcasted_iota(jnp.int32, (nrow, ATTN_WIDTH), 0)
        lane = lax.broadcasted_iota(jnp.int32, (nrow, ATTN_WIDTH), 1)
        qq = jnp.zeros((nrow, ATTN_WIDTH), F32)
        for i in range(dec_seq):
            qq = jnp.where((row & (dec_seq - 1)) == i, q[i:i + 1, :], qq)
        qt_ref[...] = jnp.where(lane // HEAD_DIM == row // dec_seq, qq, 0.0).astype(BF16)
        knp_ref[...] = jnp.zeros(knp_ref.shape, F32)
        vnp_ref[...] = jnp.zeros(vnp_ref.shape, F32)
        m_ref[...] = jnp.full(m_ref.shape, NEG_INF, F32)
        l_ref[...] = jnp.zeros(l_ref.shape, F32)
        acc_ref[...] = jnp.zeros(acc_ref.shape, F32)

    def attend(st, weighted_values):
        m_old = m_ref[...]
        m_new = jnp.maximum(m_old, jnp.max(st, axis=-1, keepdims=True))
        alpha = jnp.exp(m_old - m_new)
        p = jnp.exp(st - m_new)
        l_ref[...] = alpha * l_ref[...] + jnp.sum(p, axis=-1, keepdims=True)
        acc_ref[...] = alpha * acc_ref[...] + weighted_values(p)
        m_ref[...] = m_new

    kc = jnp.concatenate([r[...].astype(BF16) for r in k_refs], axis=1)
    st = _dot(qt_ref[...], kc)

    def page_values(p):
        outs = []
        for h in range(N_HEADS):
            vh = jnp.concatenate(
                [r[pl.ds(h, PAGE_SIZE, stride=N_HEADS), :].astype(BF16) for r in v_refs], axis=0)
            outs.append(_dot(p[h * rows:(h + 1) * rows, :].astype(BF16), vh))
        return jnp.concatenate(outs, axis=0)

    attend(st, page_values)

    @pl.when(g == pl.num_programs(1) - 1)
    def _():
        knp_ref[0:dec_seq, :] = kn_ref[...]
        vnp_ref[0:dec_seq, :] = vn_ref[...]
        sn = _dot_nt(qt_ref[...], knp_ref[...].astype(BF16))
        row = lax.broadcasted_iota(jnp.int32, sn.shape, 0)
        col = lax.broadcasted_iota(jnp.int32, sn.shape, 1)
        sn = jnp.where(col <= (row & (dec_seq - 1)), sn, MASK_VALUE)

        def new_values(p):
            o = _dot(p.astype(BF16), vnp_ref[...].astype(BF16))
            return jnp.concatenate(
                [o[h * rows:(h + 1) * rows, h * V_DIM:(h + 1) * V_DIM] for h in range(N_HEADS)], axis=0)

        attend(sn, new_values)

        o = acc_ref[...] / l_ref[...]
        lam = _lambda_full(lamv_ref, lam0)
        a = o - lam * pltpu.roll(o, nrow - dec_seq, 0)
        a = _rms_rows(a, sg_ref[...]) * (1.0 - lam0)
        for h in range(N_HEADS):
            o_ref[:, h * V_DIM:(h + 1) * V_DIM] = a[h * rows:h * rows + dec_seq, :]


def _sample_attention(y3, cache_kt, cache_vr, page_table, layer, lamv, sg, lam0):
    dec_b, dec_seq, _ = y3.shape
    n_pages = page_table.shape[1]
    npg = PAGES_PER_STEP
    assert n_pages % npg == 0 and dec_seq & (dec_seq - 1) == 0
    nrow = N_HEADS * 2 * dec_seq
    pt = page_table.reshape(-1)

    def page_spec(p):
        return pl.BlockSpec((None, None, ATTN_WIDTH, PAGE_SIZE),
                            lambda b, g, pt_ref: (layer, pt_ref[b * n_pages + g * npg + p], 0, 0))

    def col_spec(c):
        return pl.BlockSpec((None, dec_seq, ATTN_WIDTH), lambda b, g, pt_ref: (b, 0, c))

    grid_spec = pltpu.PrefetchScalarGridSpec(
        num_scalar_prefetch=1,
        grid=(dec_b, n_pages // npg),
        in_specs=[col_spec(0), col_spec(1), col_spec(2),
                  pl.BlockSpec((4, LANES), lambda b, g, pt_ref: (0, 0)),
                  pl.BlockSpec((1, LANES), lambda b, g, pt_ref: (0, 0))]
                 + [page_spec(p) for p in range(npg)] + [page_spec(p) for p in range(npg)],
        out_specs=pl.BlockSpec((None, dec_seq, ATTN_WIDTH), lambda b, g, pt_ref: (b, 0, 0)),
        scratch_shapes=[pltpu.VMEM((nrow, ATTN_WIDTH), BF16),
                        pltpu.VMEM((PAGE_SIZE, ATTN_WIDTH), F32),
                        pltpu.VMEM((PAGE_SIZE, ATTN_WIDTH), F32),
                        pltpu.VMEM((nrow, 1), F32),
                        pltpu.VMEM((nrow, 1), F32),
                        pltpu.VMEM((nrow, V_DIM), F32)],
    )
    kern = functools.partial(_sattn_kernel, lam0=lam0, npg=npg)
    return pl.pallas_call(
        kern,
        grid_spec=grid_spec,
        out_shape=jax.ShapeDtypeStruct((dec_b, dec_seq, ATTN_WIDTH), F32),
        compiler_params=_params("parallel", "arbitrary"),
        name="sample_attention",
    )(pt, y3, y3, y3, lamv, sg, *([cache_kt] * npg), *([cache_vr] * npg))


def _conv_sample_kernel(gb_ref, gc_ref, xc_ref, e1_ref, e2_ref, w_ref, c_ref, u_ref, *, dec_seq):
    u = gc_ref[...] * xc_ref[...]
    pos = lax.broadcasted_iota(jnp.int32, u.shape, 0) & (dec_seq - 1)
    u1 = jnp.where(pos == 0, e1_ref[...], pltpu.roll(u, 1, 0))
    u2 = jnp.where(pos <= 1, e2_ref[...], pltpu.roll(u, 2, 0))
    w = w_ref[...]
    conv = w[0:1] * u2 + w[1:2] * u1 + w[2:3] * u
    c_ref[...] = (gb_ref[...] * conv).astype(c_ref.dtype)
    u_ref[...] = u


def _conv_sample(y, conv_w, prev, dec_seq):
    t = y.shape[0]
    dec_b = t // dec_seq
    zeros = jnp.zeros((dec_b, dec_seq - 1, CONV_WIDTH), F32)
    e1 = jnp.concatenate([prev[:, 1:2], zeros], axis=1).reshape(t, CONV_WIDTH)
    e2 = jnp.concatenate([prev, zeros[:, 1:]], axis=1).reshape(t, CONV_WIDTH)
    kern = functools.partial(_conv_sample_kernel, dec_seq=dec_seq)
    wide = lambda c: pl.BlockSpec((t, CONV_WIDTH), lambda i: (0, c))
    full = pl.BlockSpec((t, CONV_WIDTH), lambda i: (0, 0))
    return pl.pallas_call(
        kern,
        grid=(1,),
        in_specs=[wide(3), wide(4), wide(5), full, full,
                  pl.BlockSpec((CONV_K, CONV_WIDTH), lambda i: (0, 0))],
        out_specs=[full, full],
        out_shape=[jax.ShapeDtypeStruct((t, CONV_WIDTH), BF16),
                   jax.ShapeDtypeStruct((t, CONV_WIDTH), F32)],
        compiler_params=_params("arbitrary"),
        name="conv_sample",
    )(y, y, y, e1, e2, conv_w)


def _outproj_kernel(x_ref, a_ref, c_ref, wa_ref, wc_ref, o_ref):
    o_ref[...] = (x_ref[...] + _dot(a_ref[...].astype(BF16), wa_ref[...])
                  + _dot(c_ref[...].astype(BF16), wc_ref[...]))


def _outproj(x, a, c, w, layer, tm, tn, name):
    t, d = x.shape
    return pl.pallas_call(
        _outproj_kernel,
        grid=(t // tm, d // tn),
        in_specs=[pl.BlockSpec((tm, tn), lambda i, j: (i, j)),
                  pl.BlockSpec((tm, ATTN_WIDTH), lambda i, j: (i, 0)),
                  pl.BlockSpec((tm, CONV_WIDTH), lambda i, j: (i, 0)),
                  pl.BlockSpec((None, ATTN_WIDTH, tn), lambda i, j: (layer, 0, j)),
                  pl.BlockSpec((None, CONV_WIDTH, tn), lambda i, j: (layer, 1, j))],
        out_specs=pl.BlockSpec((tm, tn), lambda i, j: (i, j)),
        out_shape=jax.ShapeDtypeStruct((t, d), F32),
        compiler_params=_params("parallel", "arbitrary"),
        name=name,
    )(x, a, c, w, w)


def _conv_outproj_kernel(x_ref, a_ref, gb_ref, gc_ref, xc_ref, hgc_ref, hxc_ref, cw_ref, wa_ref, wc_ref,
                         o_ref, st_ref, *, tiles_per_seq):
    i = pl.program_id(0)
    u = gc_ref[...] * xc_ref[...]
    tm = u.shape[0]
    uh = hgc_ref[...] * hxc_ref[...]
    uh = jnp.where(i % tiles_per_seq == 0, 0.0, uh)
    row = lax.broadcasted_iota(jnp.int32, u.shape, 0)
    prev1, prev2 = uh[SUBLANES - 1:SUBLANES], uh[SUBLANES - 2:SUBLANES - 1]
    u1 = jnp.where(row == 0, prev1, pltpu.roll(u, 1, 0))
    u2 = jnp.where(row == 0, prev2, jnp.where(row == 1, prev1, pltpu.roll(u, 2, 0)))
    w = cw_ref[...]
    c = (gb_ref[...] * (w[0:1] * u2 + w[1:2] * u1 + w[2:3] * u)).astype(BF16)
    st_ref[...] = u[tm - (CONV_K - 1):tm]
    o_ref[...] = x_ref[...] + _dot(a_ref[...].astype(BF16), wa_ref[...]) + _dot(c, wc_ref[...])


def _conv_outproj(x, a, y, conv_w, w, layer, batch, seq, tm):
    t, d = x.shape
    tps = seq // tm
    hb = tm // SUBLANES
    kern = functools.partial(_conv_outproj_kernel, tiles_per_seq=tps)
    wide = lambda c: pl.BlockSpec((tm, CONV_WIDTH), lambda i: (i, c))
    halo = lambda c: pl.BlockSpec((SUBLANES, CONV_WIDTH), lambda i: (jnp.maximum(i * hb - 1, 0), c))
    resident = lambda r: pl.BlockSpec((None, ATTN_WIDTH, d), lambda i: (layer, r, 0),
                                      pipeline_mode=pl.Buffered(1))
    return pl.pallas_call(
        kern,
        grid=(t // tm,),
        in_specs=[pl.BlockSpec((tm, d), lambda i: (i, 0)),
                  pl.BlockSpec((tm, ATTN_WIDTH), lambda i: (i, 0)),
                  wide(3), wide(4), wide(5), halo(4), halo(5),
                  pl.BlockSpec((CONV_K, CONV_WIDTH), lambda i: (0, 0)),
                  resident(0), resident(1)],
        out_specs=[pl.BlockSpec((tm, d), lambda i: (i, 0)),
                   pl.BlockSpec((None, CONV_K - 1, CONV_WIDTH), lambda i: (i // tps, 0, 0))],
        out_shape=[jax.ShapeDtypeStruct((t, d), F32),
                   jax.ShapeDtypeStruct((batch, CONV_K - 1, CONV_WIDTH), F32)],
        compiler_params=_params("arbitrary"),
        name="conv_outproj_prompt",
    )(x, a, y, y, y, y, y, conv_w, w, w)


def _top_with_index(s, k):
    nrows = s.shape[0]
    riota = lax.broadcasted_iota(jnp.int32, s.shape, 0).astype(F32)
    vals, idxs = [], []
    for _ in range(k):
        m = jnp.max(s, axis=0, keepdims=True)
        i = jnp.min(jnp.where(s == m, riota, float(nrows)), axis=0, keepdims=True)
        vals.append(m)
        idxs.append(i)
        s = jnp.where(riota == i, NEG_INF, s)
    return vals, idxs


def _stack_rows(rows, nrows, fill):
    r = lax.broadcasted_iota(jnp.int32, (nrows, rows[0].shape[1]), 0)
    out = jnp.full(r.shape, fill, F32)
    for j, row in enumerate(rows):
        out = jnp.where(r == j, row, out)
    return out


def _candidates(t1, i1, t2, i2):
    k = PEER_TOPK
    t2_all = _stack_rows(t2, k, NEG_INF)
    i2_all = _stack_rows(i2, k, 0.0)
    tile_row = lax.broadcasted_iota(jnp.int32, (SUBLANES, t2_all.shape[1]), 0)
    vals, codes, tail_v, tail_c = [], [], [], []
    for j1 in range(k):
        cnt = k // (j1 + 1)
        base = i1[j1] * float(PEER_NKEYS)
        if cnt > SUBLANES:
            vals.append(t1[j1] + t2_all)
            codes.append(base + i2_all)
        elif cnt >= SUBLANES // 2:
            v = t1[j1] + t2_all[:SUBLANES]
            vals.append(v if cnt == SUBLANES else jnp.where(tile_row < cnt, v, NEG_INF))
            codes.append(base + i2_all[:SUBLANES])
        else:
            tail_v += [t1[j1] + t2[j2] for j2 in range(cnt)]
            tail_c += [base + i2[j2] for j2 in range(cnt)]
    tail_rows = -(-len(tail_v) // SUBLANES) * SUBLANES
    vals.append(_stack_rows(tail_v, tail_rows, NEG_INF))
    codes.append(_stack_rows(tail_c, tail_rows, 0.0))
    return jnp.concatenate(vals, axis=0), jnp.concatenate(codes, axis=0)


def _peer_select_kernel(x_ref, g_ref, hq_ref, sk_ref, xn_ref, gate_ref, at_ref, bt_ref, gt_ref,
                        atm_ref, btm_ref, gtm_ref, m_ref):
    tm = x_ref.shape[0]
    xn_ref[...] = _rms_rows(x_ref[...], g_ref[...]).astype(BF16)

    for h in range(PEER_HEADS):
        c0 = h * 2 * PEER_NKEYS
        q1 = hq_ref[:, c0:c0 + PEER_NKEYS].astype(BF16)
        q2 = hq_ref[:, c0 + PEER_NKEYS:c0 + 2 * PEER_NKEYS].astype(BF16)
        s1 = _dot_nt(sk_ref[h, 0], q1)
        s2 = _dot_nt(sk_ref[h, 1], q2)
        t1, i1 = _top_with_index(s1, PEER_TOPK)
        t2, i2 = _top_with_index(s2, PEER_TOPK)
        cand, code = _candidates(t1, i1, t2, i2)
        sc, rows = _top_with_index(cand, PEER_TOPK)
        riota = lax.broadcasted_iota(jnp.int32, cand.shape, 0).astype(F32)
        ex = [jnp.exp(c - sc[0]) for c in sc]
        z = ex[0]
        for v in ex[1:]:
            z = z + v
        for j in range(PEER_TOPK):
            picked = jnp.sum(jnp.where(riota == rows[j], code, 0.0), axis=0, keepdims=True)
            n1 = jnp.floor(picked * (1.0 / PEER_NKEYS))
            slot = h * PEER_TOPK + j
            at_ref[slot:slot + 1, :] = n1
            bt_ref[slot:slot + 1, :] = picked - n1 * float(PEER_NKEYS)
            gt_ref[slot:slot + 1, :] = ex[j] / z

    atm_ref[...] = at_ref[...].T
    btm_ref[...] = bt_ref[...].T
    gtm_ref[...] = gt_ref[...].T

    niota = lax.broadcasted_iota(jnp.int32, (PEER_NKEYS, LANES), 0).astype(F32)

    def group(gi, carry):
        t0 = pl.multiple_of(gi * GATE_GROUP, GATE_GROUP)
        for tt in range(GATE_GROUP):
            a_row = atm_ref[pl.ds(t0 + tt, 1), :]
            b_row = btm_ref[pl.ds(t0 + tt, 1), :]
            g_row = gtm_ref[pl.ds(t0 + tt, 1), :]
            lhs = jnp.where(niota == a_row, g_row, 0.0).astype(BF16)
            rhs = jnp.where(niota == b_row, 1.0, 0.0).astype(BF16)
            m_ref[tt * GATE_PITCH:tt * GATE_PITCH + PEER_NKEYS, :] = _dot_nt(lhs, rhs)
        for n1 in range(PEER_NKEYS):
            rows_n1 = m_ref[pl.ds(n1, GATE_GROUP, stride=GATE_PITCH), :]
            gate_ref[pl.ds(t0, GATE_GROUP), n1 * PEER_NKEYS:(n1 + 1) * PEER_NKEYS] = rows_n1.astype(BF16)
        return carry

    lax.fori_loop(0, tm // GATE_GROUP, group, 0)


def _peer_select(x, g, hq, subkeys, layer, tm, name):
    t, d = x.shape
    nk = PEER_NKEYS
    slots = PEER_HEADS * PEER_TOPK
    return pl.pallas_call(
        _peer_select_kernel,
        grid=(t // tm,),
        in_specs=[pl.BlockSpec((tm, d), lambda i: (i, 0)),
                  pl.BlockSpec((1, d), lambda i: (0, 0)),
                  pl.BlockSpec((tm, PEER_HEADS * 2 * nk), lambda i: (i, 0)),
                  pl.BlockSpec((None, PEER_HEADS, 2, nk, nk), lambda i: (layer, 0, 0, 0, 0))],
        out_specs=[pl.BlockSpec((tm, d), lambda i: (i, 0)),
                   pl.BlockSpec((tm, PEER_N), lambda i: (i, 0))],
        out_shape=[jax.ShapeDtypeStruct((t, d), BF16),
                   jax.ShapeDtypeStruct((t, PEER_N), BF16)],
        scratch_shapes=[pltpu.VMEM((slots, tm), F32)] * 3 + [pltpu.VMEM((tm, slots), F32)] * 3
                       + [pltpu.VMEM((GATE_GROUP * GATE_PITCH, nk), F32)],
        compiler_params=_params("parallel"),
        name=name,
    )(x, g, hq, subkeys)


def _peer_dense_kernel(xn_ref, gate_ref, u_ref, v_ref, o_ref):
    e = pl.program_id(1)
    tm = xn_ref.shape[0]

    @pl.when(e == 0)
    def _():
        o_ref[...] = jnp.zeros(o_ref.shape, F32)

    blk = min(tm, DENSE_RB)
    for k in range(tm // blk):
        rs = slice(k * blk, (k + 1) * blk)
        a = _dot_nt(xn_ref[rs, :], u_ref[...])
        twice_gelu = a * (1.0 + lax.erf(a * SQRT_HALF))
        weighted = twice_gelu.astype(BF16) * (gate_ref[rs, :] * jnp.asarray(0.5, BF16))
        o_ref[rs, :] += _dot(weighted, v_ref[...])


def _peer_dense(xn, gate, u, v, layer, tm, name):
    t, d = xn.shape
    ec = PEER_EC
    return pl.pallas_call(
        _peer_dense_kernel,
        grid=(t // tm, PEER_N // ec),
        in_specs=[pl.BlockSpec((tm, d), lambda i, e: (i, 0)),
                  pl.BlockSpec((tm, ec), lambda i, e: (i, e)),
                  pl.BlockSpec((None, ec, d), lambda i, e: (layer, e, 0)),
                  pl.BlockSpec((None, ec, d), lambda i, e: (layer, e, 0))],
        out_specs=pl.BlockSpec((tm, d), lambda i, e: (i, 0)),
        out_shape=jax.ShapeDtypeStruct((t, d), F32),
        compiler_params=_params("parallel", "arbitrary"),
        name=name,
    )(xn, gate, u, v)


def _ple_kernel(x_ref, peer_ref, p_ref, g_ref, wg_ref, wp_ref, o_ref):
    x = x_ref[...] + peer_ref[...]
    xn = _rms_rows(x, g_ref[...]).astype(BF16)
    z = _dot(xn, wg_ref[...])
    gate = 1.0 / (1.0 + jnp.exp(-z))
    o_ref[...] = x + gate * _dot(p_ref[...].astype(BF16), wp_ref[...])


def _ple(x, peer, p, g, wg, wp, layer, tm, name):
    t, d = x.shape
    return pl.pallas_call(
        _ple_kernel,
        grid=(t // tm,),
        in_specs=[pl.BlockSpec((tm, d), lambda i: (i, 0)),
                  pl.BlockSpec((tm, d), lambda i: (i, 0)),
                  pl.BlockSpec((tm, PLE_DIM), lambda i: (i, 0)),
                  pl.BlockSpec((1, d), lambda i: (0, 0)),
                  pl.BlockSpec((None, d, d), lambda i: (layer, 0, 0)),
                  pl.BlockSpec((None, PLE_DIM, d), lambda i: (layer, 0, 0))],
        out_specs=pl.BlockSpec((tm, d), lambda i: (i, 0)),
        out_shape=jax.ShapeDtypeStruct((t, d), F32),
        compiler_params=_params("parallel"),
        name=name,
    )(x, peer, p, g, wg, wp)


def _lambda_init(layer_idx):
    return 0.8 - 0.6 * math.exp(-0.3 * layer_idx)


def _token_tail(x1, wts, lw, layer, p_l, tm, tag):
    hq = _norm_mm(x1, lw["ffn_norm_g"], wts["peer_wq"], layer, tm, D_MODEL, "peer_query_" + tag)
    xn, gate = _peer_select(x1, lw["ffn_norm_g"], hq, wts["peer_subkeys"], layer, min(tm, SELECT_TM),
                            "peer_select_" + tag)
    peer = _peer_dense(xn, gate, wts["peer_u"], wts["peer_v"], layer, tm, "peer_dense_" + tag)
    return _ple(x1, peer, p_l, lw["ple_norm_g"], wts["ple_gate_w"], wts["ple_proj_w"], layer,
                min(tm, PLE_TM), "ple_" + tag)


def kernel(x_prompt, x_sample, cache_k, cache_v, state_conv, page_table, p_prompt, p_sample,
           attn_norm_g, w_in, q_norm_g, k_norm_g, lam_q1, lam_k1, lam_q2, lam_k2, subln_g,
           conv_w, w_out, ffn_norm_g, peer_wq, peer_subkeys, peer_u, peer_v,
           ple_norm_g, ple_gate_w, ple_proj_w):
    batch, seq, d = x_prompt.shape
    dec_b, dec_seq, _ = x_sample.shape
    depth = w_in.shape[0]
    n_pool = cache_k.shape[1]
    past_len = page_table.shape[1] * cache_k.shape[2]
    tp, ts = batch * seq, dec_b * dec_seq

    rope_p = _rope_tables(jnp.arange(seq))
    rope_s = _rope_tables(past_len + (jnp.arange(ts) % dec_seq))
    ck = cache_k.transpose(0, 1, 3, 4, 5, 2).reshape(depth, n_pool, ATTN_WIDTH, PAGE_SIZE)
    cv = cache_v.reshape(depth, n_pool, PAGE_SIZE * N_HEADS, V_DIM)

    yp = x_prompt.reshape(tp, d)
    ys = x_sample.reshape(ts, d)
    outs = [[] for _ in range(6)]
    row = lambda v: v.reshape(1, -1)
    wts = {
        "w_in": w_in.astype(BF16),
        "w_out": w_out.astype(BF16),
        "peer_wq": peer_wq.astype(BF16),
        "peer_subkeys": peer_subkeys.astype(BF16),
        "peer_u": peer_u.astype(BF16),
        "peer_v": peer_v.astype(BF16),
        "ple_gate_w": ple_gate_w.astype(BF16),
        "ple_proj_w": ple_proj_w.astype(BF16),
    }
    for l in range(depth):
        lam0 = _lambda_init(l)
        lw = {"ffn_norm_g": row(ffn_norm_g[l]), "ple_norm_g": row(ple_norm_g[l])}
        g_in = row(attn_norm_g[l])
        qkg = jnp.stack([jnp.tile(q_norm_g[l], 2), jnp.tile(k_norm_g[l], 2)]).reshape(2, 1, LANES)
        lamv = jnp.pad(jnp.stack([lam_q1[l], lam_k1[l], lam_q2[l], lam_k2[l]]),
                       ((0, 0), (0, LANES - HEAD_DIM)))
        sg = row(subln_g[l])

        y, k_t, v_t = _inproj(yp, g_in, wts["w_in"], l, qkg, rope_p, PROMPT_TM, "inproj_prompt",
                                kv_batch=(batch, seq))
        a = _prompt_attention(y, lamv, sg, lam0, batch, seq)
        x1, cstate = _conv_outproj(yp, a, y, conv_w[l], wts["w_out"], l, batch, seq, PROMPT_TM)
        outs[0].append(k_t.reshape(batch, N_HEADS, 2, HEAD_DIM, seq).transpose(0, 4, 1, 2, 3))
        outs[1].append(v_t.reshape(batch, seq, N_HEADS, V_DIM))
        outs[2].append(cstate)
        yp = _token_tail(x1, wts, lw, l, p_prompt[l].reshape(tp, PLE_DIM), PROMPT_TM, "prompt")

        y = _inproj(ys, g_in, wts["w_in"], l, qkg, rope_s, ts, "inproj_sample")
        a = _sample_attention(y.reshape(dec_b, dec_seq, IN_COLS), ck, cv, page_table, l, lamv, sg, lam0)
        c, u = _conv_sample(y, conv_w[l], state_conv[l], dec_seq)
        outs[3].append(y[:, ATTN_WIDTH:2 * ATTN_WIDTH].reshape(dec_b, dec_seq, N_HEADS, 2, HEAD_DIM))
        outs[4].append(y[:, 2 * ATTN_WIDTH:3 * ATTN_WIDTH].reshape(dec_b, dec_seq, N_HEADS, V_DIM))
        outs[5].append(u.reshape(dec_b, dec_seq, CONV_WIDTH)[:, dec_seq - (CONV_K - 1):])
        x1 = _outproj(ys, a.reshape(ts, ATTN_WIDTH), c, wts["w_out"], l, ts, D_MODEL, "outproj_sample")
        ys = _token_tail(x1, wts, lw, l, p_sample[l].reshape(ts, PLE_DIM), ts, "sample")

    return (yp.reshape(batch, seq, d), ys.reshape(dec_b, dec_seq, d),
            jnp.stack(outs[0]), jnp.stack(outs[1]), jnp.stack(outs[2]),
            jnp.stack(outs[3]), jnp.stack(outs[4]), jnp.stack(outs[5]))
```
